```python
import jax, jax.numpy as jnp
from jax import lax
import numpy as np

D_MODEL = 1024
BATCH = 8
SEQ = 2048
DEPTH = 1

N_META = 16
EPS = 1e-6
ATT_HEADS = 8
ATT_KV_HEADS = 2
ATT_GROUP = ATT_HEADS // ATT_KV_HEADS
HEAD_DIM = 64
WINDOW = 128
ATT_BLOCK = 128
ROPE_THETA = 10000.0
GLA_HEADS = 4
GLA_DK = 256
GLA_DV = 512
GLA_DK_HEAD = GLA_DK // GLA_HEADS
GLA_DV_HEAD = GLA_DV // GLA_HEADS
GLA_RANK = 16
GLA_TAU = 16.0
GLA_CHUNK = 16
D_FF = 2816
CONV_WIDTH = 3

ATT_Q = ATT_HEADS * HEAD_DIM
ATT_KV = ATT_KV_HEADS * HEAD_DIM
SPLIT_SIZES = (ATT_Q, ATT_KV, ATT_KV, GLA_DK, GLA_DK, GLA_DV, GLA_DV, GLA_RANK, D_MODEL, D_MODEL)
SPLIT_POINTS = tuple(int(s) for s in np.cumsum(SPLIT_SIZES)[:-1])
D_IN = int(sum(SPLIT_SIZES))

kernel_name = "hybrid_swa_sink_gla_convglu_block"


def rmsnorm(x, g):
    xf = x.astype(jnp.float32)
    y = xf * lax.rsqrt(jnp.mean(xf * xf, axis=-1, keepdims=True) + EPS) * g.astype(jnp.float32)
    return y.astype(x.dtype)


def rope(x, pos):
    half = x.shape[-1] // 2
    inv_freq = ROPE_THETA ** (-jnp.arange(half, dtype=jnp.float32) / half)
    ang = pos.astype(jnp.float32)[:, None] * inv_freq[None, :]
    cos = jnp.concatenate([jnp.cos(ang), jnp.cos(ang)], -1)[None, :, None, :]
    sin = jnp.concatenate([jnp.sin(ang), jnp.sin(ang)], -1)[None, :, None, :]
    xf = x.astype(jnp.float32)
    rot = jnp.concatenate([-xf[..., half:], xf[..., :half]], -1)
    return (xf * cos + rot * sin).astype(x.dtype)


def swa_attention(q, k, v, sinks):
    B, L = q.shape[:2]
    pad = (-L) % ATT_BLOCK
    T = L + pad
    NB = T // ATT_BLOCK
    padf = lambda t: jnp.pad(t, ((0, 0), (pad, 0), (0, 0), (0, 0)))
    qb = padf(q).reshape(B, NB, ATT_BLOCK, ATT_KV_HEADS, ATT_GROUP, HEAD_DIM)
    kb = padf(k).reshape(B, NB, ATT_BLOCK, ATT_KV_HEADS, HEAD_DIM)
    vb = padf(v).reshape(B, NB, ATT_BLOCK, ATT_KV_HEADS, HEAD_DIM)
    prev = lambda t: jnp.concatenate([jnp.zeros_like(t[:, :1]), t[:, :-1]], axis=1)
    kk = jnp.concatenate([prev(kb), kb], axis=2)
    vv = jnp.concatenate([prev(vb), vb], axis=2)
    s = jnp.einsum('bnqkgd,bnskd->bnkgqs', qb, kk,
                   preferred_element_type=jnp.float32) * (HEAD_DIM ** -0.5)
    pos = jnp.arange(T) - pad
    qpos = pos.reshape(NB, ATT_BLOCK)
    kpos = jnp.concatenate([qpos - ATT_BLOCK, qpos], axis=1)
    rel = qpos[:, :, None] - kpos[:, None, :]
    valid = (rel >= 0) & (rel < WINDOW) & (kpos[:, None, :] >= 0)
    s = jnp.where(valid[None, :, None, None], s, -jnp.inf)
    sink = sinks.astype(jnp.float32).reshape(ATT_KV_HEADS, ATT_GROUP)[None, None, :, :, None, None]
    sink = jnp.broadcast_to(sink, s.shape[:-1] + (1,))
    p = jax.nn.softmax(jnp.concatenate([s, sink], axis=-1), axis=-1)[..., :-1]
    o = jnp.einsum('bnkgqs,bnskd->bnqkgd', p.astype(v.dtype), vv)
    return o.reshape(B, T, ATT_Q)[:, pad:]


def gla(q, k, v, log_a):
    B, L, H, Dk = q.shape
    Dv = v.shape[-1]
    C = GLA_CHUNK
    N = L // C
    to_chunks = lambda t: t.astype(jnp.float32).reshape(B, N, C, H, t.shape[-1]).transpose(1, 0, 3, 2, 4)
    qc, kc, vc, gc = (to_chunks(t) for t in (q * (Dk ** -0.5), k, v, log_a))
    causal = jnp.tril(jnp.ones((C, C), dtype=bool))[:, :, None]

    def step(S, inp):
        qi, ki, vi, gi = inp
        b = jnp.cumsum(gi, axis=2)
        o_inter = jnp.einsum('bhcd,bhde->bhce', qi * jnp.exp(b), S)
        diff = b[:, :, :, None, :] - b[:, :, None, :, :]
        decay = jnp.exp(jnp.where(causal, diff, -jnp.inf))
        a = jnp.einsum('bhid,bhjd,bhijd->bhij', qi, ki, decay)
        o = o_inter + jnp.einsum('bhij,bhje->bhie', a, vi)
        b_last = b[:, :, -1:, :]
        S = jnp.exp(b_last[:, :, 0, :])[..., None] * S + \
            jnp.einsum('bhcd,bhce->bhde', ki * jnp.exp(b_last - b), vi)
        return S, o

    S0 = jnp.zeros((B, H, Dk, Dv), jnp.float32)
    _, o = lax.scan(step, S0, (qc, kc, vc, gc))
    return o.transpose(1, 0, 3, 2, 4).reshape(B, L, H, Dv).astype(v.dtype)


def causal_dwconv(a, w, b):
    y = lax.conv_general_dilated(a, w[:, None, :].astype(a.dtype), window_strides=(1,),
                                 padding=[(CONV_WIDTH - 1, 0)],
                                 dimension_numbers=('NWC', 'WIO', 'NWC'),
                                 feature_group_count=a.shape[-1])
    return y + b.astype(a.dtype)


def setup_inputs(seed: int = 0) -> dict:
    key = jax.random.key(seed)
    ks = jax.random.split(key, 20)
    nrm = lambda k, shape, scale: jax.random.normal(k, shape, jnp.float32) * scale
    gain = lambda k, shape: 1.0 + 0.01 * jax.random.normal(k, shape, jnp.float32)
    return {
        "x": nrm(ks[0], (BATCH, SEQ, D_MODEL), 1.0),
        "meta_tokens": nrm(ks[1], (N_META, D_MODEL), 1.0),
        "mix_norm": gain(ks[2], (DEPTH, D_MODEL)),
        "w_in": nrm(ks[3], (DEPTH, D_MODEL, D_IN), D_MODEL ** -0.5),
        "b_in": nrm(ks[4], (DEPTH, D_IN), 0.01),
        "w_alpha": nrm(ks[5], (DEPTH, GLA_RANK, GLA_DK), GLA_RANK ** -0.5),
        "b_alpha": nrm(ks[6], (DEPTH, GLA_DK), 0.1),
        "attn_sinks": nrm(ks[7], (DEPTH, ATT_HEADS), 0.5),
        "gla_head_norm": gain(ks[8], (DEPTH, GLA_DV_HEAD)),
        "w_proj_attn": nrm(ks[9], (DEPTH, ATT_Q, D_MODEL), ATT_Q ** -0.5),
        "w_proj_gla": nrm(ks[10], (DEPTH, GLA_DV, D_MODEL), GLA_DV ** -0.5),
        "w_out": nrm(ks[11], (DEPTH, D_MODEL, D_MODEL), D_MODEL ** -0.5),
        "ffn_norm": gain(ks[12], (DEPTH, D_MODEL)),
        "w_up": nrm(ks[13], (DEPTH, D_MODEL, 2 * D_FF), D_MODEL ** -0.5),
        "conv_w": nrm(ks[14], (DEPTH, CONV_WIDTH, D_FF), CONV_WIDTH ** -0.5),
        "conv_b": nrm(ks[15], (DEPTH, D_FF), 0.01),
        "w_down": nrm(ks[16], (DEPTH, D_FF, D_MODEL), D_FF ** -0.5),
        "final_norm": gain(ks[17], (D_MODEL,)),
    }


def reference(x, meta_tokens, mix_norm, w_in, b_in, w_alpha, b_alpha, attn_sinks, gla_head_norm,
              w_proj_attn, w_proj_gla, w_out, ffn_norm, w_up, conv_w, conv_b, w_down, final_norm):
    B = x.shape[0]
    meta = jnp.broadcast_to(meta_tokens.astype(x.dtype)[None], (B, N_META, D_MODEL))
    h = jnp.concatenate([meta, x], axis=1)
    L = h.shape[1]
    pos = jnp.arange(L)
    for l in range(DEPTH):
        u = rmsnorm(h, mix_norm[l])
        proj = u @ w_in[l] + b_in[l]
        aq, ak, av, gq, gk, gv, gr, g_low, gate_a, gate_b = jnp.split(proj, SPLIT_POINTS, axis=-1)
        aq = rope(aq.reshape(B, L, ATT_HEADS, HEAD_DIM), pos)
        ak = rope(ak.reshape(B, L, ATT_KV_HEADS, HEAD_DIM), pos)
        av = av.reshape(B, L, ATT_KV_HEADS, HEAD_DIM)
        y_att = swa_attention(aq, ak, av, attn_sinks[l])
        z = (g_low @ w_alpha[l] + b_alpha[l]).astype(jnp.float32)
        log_a = (jax.nn.log_sigmoid(z) / GLA_TAU).reshape(B, L, GLA_HEADS, GLA_DK_HEAD)
        y_gla = gla(gq.reshape(B, L, GLA_HEADS, GLA_DK_HEAD), gk.reshape(B, L, GLA_HEADS, GLA_DK_HEAD),
                    gv.reshape(B, L, GLA_HEADS, GLA_DV_HEAD), log_a)
        y_gla = rmsnorm(y_gla, gla_head_norm[l]).reshape(B, L, GLA_DV) * jax.nn.silu(gr)
        mixed = jax.nn.sigmoid(gate_a) * (y_att @ w_proj_attn[l]) + \
            jax.nn.sigmoid(gate_b) * (y_gla @ w_proj_gla[l])
        h = h + mixed @ w_out[l]
        u = rmsnorm(h, ffn_norm[l])
        a, v = jnp.split(u @ w_up[l], 2, axis=-1)
        h = h + (jax.nn.gelu(causal_dwconv(a, conv_w[l], conv_b[l])) * v) @ w_down[l]
    return rmsnorm(h, final_norm)[:, N_META:]
```

```python
import functools

import numpy as np
import jax
import jax.numpy as jnp
from jax import lax
from jax.experimental import pallas as pl
from jax.experimental.pallas import tpu as pltpu

D_MODEL = 1024
N_META = 16
EPS = 1e-6
ATT_HEADS = 8
ATT_KV_HEADS = 2
HEAD_DIM = 64
BLK = 128
ROPE_THETA = 10000.0
GLA_HEADS = 4
GLA_DK = 256
GLA_DV = 512
GLA_DK_HEAD = GLA_DK // GLA_HEADS
GLA_DV_HEAD = GLA_DV // GLA_HEADS
GLA_RANK = 16
GLA_TAU = 16.0
D_FF = 2816
ATT_Q = ATT_HEADS * HEAD_DIM
ATT_KV = ATT_KV_HEADS * HEAD_DIM

META_PAD = BLK - N_META
LANES = 128
HALO = 16
NEG = -1e30
VMEM_LIMIT = 56 * 1024 * 1024

C_QKV = ATT_Q + 2 * ATT_KV
C_GLA = 2 * GLA_DK + GLA_DV
C_GR = GLA_DV
C_GATE = 2 * D_MODEL
C_LOW = LANES
O_QKV = 0
O_GLA = O_QKV + C_QKV
O_GR = O_GLA + C_GLA
O_GATE = O_GR + C_GR
O_LOW = O_GATE + C_GATE
C_ALL = O_LOW + C_LOW

GLA_LEVELS = (64, 32, 16, 8, 4, 2, 1)
N_LEVELS = len(GLA_LEVELS)

BF16 = jnp.bfloat16
F32 = jnp.float32


def _sigmoid(x):
    return 1.0 / (1.0 + jnp.exp(-x))


def _dot(a, b):
    return jnp.dot(a, b, preferred_element_type=F32)


def _dot_nt(a, b):
    return lax.dot_general(a, b, (((1,), (1,)), ((), ())), preferred_element_type=F32)


def _split3(x):
    hi = x.astype(BF16)
    r1 = x - hi.astype(F32)
    mid = r1.astype(BF16)
    lo = (r1 - mid.astype(F32)).astype(BF16)
    return hi, mid, lo


def _inproj_kernel(h_ref, g_ref, w_ref, b_ref, wal_hi_ref, wal_lo_ref, bal_ref,
                   qkv_ref, gla_ref, gr_ref, gate_ref, loga_ref):
    x = h_ref[...]
    var = jnp.mean(x * x, axis=-1, keepdims=True)
    u = (x * lax.rsqrt(var + EPS) * g_ref[...]).astype(BF16)

    def proj(off, width):
        return _dot(u, w_ref[:, off:off + width]) + b_ref[:, off:off + width]

    qkv_ref[...] = proj(O_QKV, C_QKV).astype(BF16)
    gla_ref[...] = proj(O_GLA, C_GLA).astype(BF16)
    gr = proj(O_GR, C_GR)
    gr_ref[...] = (gr * _sigmoid(gr)).astype(BF16)
    for half in range(2):
        off = half * D_MODEL
        gate_ref[:, off:off + D_MODEL] = _sigmoid(proj(O_GATE + off, D_MODEL)).astype(BF16)

    low = proj(O_LOW, C_LOW)
    low_hi = low.astype(BF16)
    low_lo = (low - low_hi.astype(F32)).astype(BF16)
    z = (_dot(low_hi, wal_hi_ref[...]) + _dot(low_lo, wal_hi_ref[...])
         + _dot(low_hi, wal_lo_ref[...]) + bal_ref[...])
    log_sig = jnp.minimum(z, 0.0) - jnp.log1p(jnp.exp(-jnp.abs(z)))
    loga_ref[...] = log_sig * (1.0 / GLA_TAU)


def _inproj(h_all, mix_norm, w_cat, b_cat, wal_hi, wal_lo, b_alpha, tm):
    rows = h_all.shape[0]
    assert rows % tm == 0
    const = lambda i: (0, 0)
    row = lambda i: (i, 0)
    return pl.pallas_call(
        _inproj_kernel,
        grid=(rows // tm,),
        in_specs=[
            pl.BlockSpec((tm, D_MODEL), row),
            pl.BlockSpec((1, D_MODEL), const),
            pl.BlockSpec((D_MODEL, C_ALL), const),
            pl.BlockSpec((1, C_ALL), const),
            pl.BlockSpec((LANES, GLA_DK), const),
            pl.BlockSpec((LANES, GLA_DK), const),
            pl.BlockSpec((1, GLA_DK), const),
        ],
        out_specs=[
            pl.BlockSpec((tm, C_QKV), row),
            pl.BlockSpec((tm, C_GLA), row),
            pl.BlockSpec((tm, C_GR), row),
            pl.BlockSpec((tm, C_GATE), row),
            pl.BlockSpec((tm, GLA_DK), row),
        ],
        out_shape=[
            jax.ShapeDtypeStruct((rows, C_QKV), BF16),
            jax.ShapeDtypeStruct((rows, C_GLA), BF16),
            jax.ShapeDtypeStruct((rows, C_GR), BF16),
            jax.ShapeDtypeStruct((rows, C_GATE), BF16),
            jax.ShapeDtypeStruct((rows, GLA_DK), F32),
        ],
        compiler_params=pltpu.CompilerParams(
            dimension_semantics=("arbitrary",), vmem_limit_bytes=VMEM_LIMIT),
        name="inproj",
    )(h_all, mix_norm, w_cat, b_cat, wal_hi, wal_lo, b_alpha)


def _seq_block_index(nblocks_x):
    return lambda s: (jnp.where(s == 0, nblocks_x, s - 1), 0)


def _seq_pos_index(nb):
    return lambda s: (jnp.where(s == 0, 0, lax.rem(jnp.maximum(s - 1, 0), nb) + 1), 0)


def _attn_kernel(nb, sinks_ref, qkv_ref, cos_ref, sin_ref, y_ref,
                 kprev, vprev, kmeta, vmeta):
    s = pl.program_id(0)
    is_meta = s == 0
    first_x = jnp.logical_and(s >= 1, lax.rem(jnp.maximum(s - 1, 0), nb) == 0)

    @pl.when(is_meta)
    def _():
        kprev[...] = jnp.zeros_like(kprev)
        vprev[...] = jnp.zeros_like(vprev)

    @pl.when(first_x)
    def _():
        kprev[...] = kmeta[...]
        vprev[...] = vmeta[...]

    cos = cos_ref[...]
    sin = sin_ref[...]
    lane = lax.broadcasted_iota(jnp.int32, (BLK, LANES), 1)
    first_half = (lane & (HEAD_DIM - 1)) < HEAD_DIM // 2

    def rope(xc):
        rot = jnp.where(first_half, pltpu.roll(xc, LANES - HEAD_DIM // 2, 1),
                        pltpu.roll(xc, HEAD_DIM // 2, 1))
        return xc * cos + rot * sin

    k_cur = rope(qkv_ref[:, ATT_Q:ATT_Q + ATT_KV].astype(F32)).astype(BF16)
    v_cur = qkv_ref[:, ATT_Q + ATT_KV:ATT_Q + 2 * ATT_KV]

    q_rows = []
    for hg in range(ATT_HEADS):
        chunk, off, grp = hg // 2, hg % 2, hg // (ATT_HEADS // ATT_KV_HEADS)
        qc = rope(qkv_ref[:, chunk * LANES:(chunk + 1) * LANES].astype(F32))
        if off != grp:
            qc = pltpu.roll(qc, HEAD_DIM, 1)
        on_group = (lane >= grp * HEAD_DIM) & (lane < (grp + 1) * HEAD_DIM)
        q_rows.append(jnp.where(on_group, qc * (HEAD_DIM ** -0.5), 0.0).astype(BF16))
    q_all = jnp.concatenate(q_rows, axis=0)

    kk = jnp.concatenate([kprev[...], k_cur], axis=0)
    vv = jnp.concatenate([vprev[...], v_cur], axis=0)
    sc = _dot_nt(q_all, kk).reshape(ATT_HEADS, BLK, 2 * BLK)

    rq = lax.broadcasted_iota(jnp.int32, (BLK, 2 * BLK), 0)
    ck = lax.broadcasted_iota(jnp.int32, (BLK, 2 * BLK), 1)
    in_prev = ck < BLK
    rk = jnp.where(in_prev, ck, ck - BLK)
    cur_min = jnp.where(is_meta, META_PAD, 0)
    prev_min = jnp.where(is_meta, BLK, jnp.where(first_x, META_PAD, 0))
    rk_lo = jnp.where(in_prev, jnp.maximum(rq + 1, prev_min), cur_min)
    rk_hi = jnp.where(in_prev, BLK - 1, rq)
    valid = (rk >= rk_lo) & (rk <= rk_hi)
    sc = jnp.where(valid[None], sc, NEG)

    hid = lax.broadcasted_iota(jnp.int32, (ATT_HEADS, 1, 1), 0)
    sink = jnp.zeros((ATT_HEADS, 1, 1), F32)
    for hg in range(ATT_HEADS):
        sink = jnp.where(hid == hg, sinks_ref[hg], sink)

    m = jnp.maximum(jnp.max(sc, axis=-1, keepdims=True), sink)
    p = jnp.exp(sc - m)
    denom = jnp.sum(p, axis=-1, keepdims=True) + jnp.exp(sink - m)
    o = _dot(p.astype(BF16).reshape(ATT_HEADS * BLK, 2 * BLK), vv)
    o = o.reshape(ATT_HEADS, BLK, LANES) * (1.0 / denom)

    for chunk in range(ATT_HEADS // 2):
        grp = (2 * chunk) // (ATT_HEADS // ATT_KV_HEADS)
        o0, o1 = o[2 * chunk], o[2 * chunk + 1]
        if grp != 0:
            o0 = pltpu.roll(o0, HEAD_DIM, 1)
        if grp != 1:
            o1 = pltpu.roll(o1, HEAD_DIM, 1)
        y_ref[:, chunk * LANES:(chunk + 1) * LANES] = jnp.where(
            lane < HEAD_DIM, o0, o1).astype(BF16)

    kprev[...] = k_cur
    vprev[...] = v_cur

    @pl.when(is_meta)
    def _():
        kmeta[...] = k_cur
        vmeta[...] = v_cur


def _attention(qkv, sinks, cos2, sin2, nblocks_x, nb):
    const = lambda s: (0, 0)
    return pl.pallas_call(
        functools.partial(_attn_kernel, nb),
        grid=(nblocks_x + 1,),
        in_specs=[
            pl.BlockSpec(memory_space=pltpu.SMEM),
            pl.BlockSpec((BLK, C_QKV), _seq_block_index(nblocks_x)),
            pl.BlockSpec((BLK, LANES), _seq_pos_index(nb)),
            pl.BlockSpec((BLK, LANES), _seq_pos_index(nb)),
        ],
        out_specs=pl.BlockSpec((BLK, ATT_Q), _seq_block_index(nblocks_x)),
        out_shape=jax.ShapeDtypeStruct((qkv.shape[0], ATT_Q), BF16),
        scratch_shapes=[pltpu.VMEM((BLK, ATT_KV), BF16)] * 4,
        compiler_params=pltpu.CompilerParams(
            dimension_semantics=("arbitrary",), vmem_limit_bytes=VMEM_LIMIT),
        name="swa_attention",
    )(sinks, qkv, cos2, sin2)


def _gla_tables():
    t = np.arange(BLK)
    tri = (t[:, None] >= t[None, :]).astype(np.float32)
    sel = np.zeros((N_LEVELS * BLK, BLK), np.float32)
    for li, half in enumerate(GLA_LEVELS):
        boundary = (t & ~(2 * half - 1)) + half - 1
        sel[li * BLK + t, boundary] = 1.0
    x = t[:, None] ^ t[None, :]
    top = np.floor(np.log2(np.maximum(x, 1))).astype(np.int64)
    lvl = np.array([GLA_LEVELS.index(1 << int(v)) for v in top.ravel()]).reshape(BLK, BLK)
    lvl = np.where(t[:, None] == t[None, :], N_LEVELS, lvl)
    lvl = np.where(t[:, None] < t[None, :], N_LEVELS + 1, lvl)
    return (jnp.asarray(tri, BF16), jnp.asarray(sel, BF16), jnp.asarray(lvl, jnp.int32))


def _gla_kernel(nb, gla_ref, loga_ref, gr_ref, tri_ref, sel_ref, lvl_ref, hn_ref, y_ref,
                state, state_meta):
    s = pl.program_id(0)
    is_meta = s == 0
    first_x = jnp.logical_and(s >= 1, lax.rem(jnp.maximum(s - 1, 0), nb) == 0)

    @pl.when(is_meta)
    def _():
        state[...] = jnp.zeros_like(state)

    @pl.when(first_x)
    def _():
        state[...] = state_meta[...]

    row = lax.broadcasted_iota(jnp.int32, (BLK, GLA_DK), 0)
    lane = lax.broadcasted_iota(jnp.int32, (BLK, GLA_DK), 1)
    is_token = row >= jnp.where(is_meta, META_PAD, 0)

    g3 = jnp.concatenate(_split3(loga_ref[...]), axis=1)
    b3 = _dot(tri_ref[...], g3)
    b = b3[:, :GLA_DK] + b3[:, GLA_DK:2 * GLA_DK] + b3[:, 2 * GLA_DK:]
    b_last = b[BLK - 1:BLK, :]

    b_hi = b.astype(BF16)
    b_mid = (b - b_hi.astype(F32)).astype(BF16)
    refs = _dot(sel_ref[...], jnp.concatenate([b_hi, b_mid], axis=1))

    qf = gla_ref[:, :GLA_DK].astype(F32) * (GLA_DK_HEAD ** -0.5)
    kf = jnp.where(is_token, gla_ref[:, GLA_DK:2 * GLA_DK].astype(F32), 0.0)

    head_lanes = [(lane >= h * GLA_DK_HEAD) & (lane < (h + 1) * GLA_DK_HEAD)
                  for h in range(GLA_HEADS)]

    def stack_heads(xq):
        return jnp.concatenate([jnp.where(hm, xq, 0.0).astype(BF16) for hm in head_lanes],
                               axis=0)

    o_inter = _dot(stack_heads(qf * jnp.exp(b)), state[...].astype(BF16))

    lvl = lvl_ref[...][None]
    a = jnp.zeros((GLA_HEADS, BLK, BLK), F32)
    a_diag = _dot_nt(stack_heads(qf), kf.astype(BF16)).reshape(GLA_HEADS, BLK, BLK)
    a = jnp.where(lvl == N_LEVELS, a_diag, a)
    for li, half in enumerate(GLA_LEVELS):
        r = (refs[li * BLK:(li + 1) * BLK, :GLA_DK]
             + refs[li * BLK:(li + 1) * BLK, GLA_DK:])
        upper = (row & half) != 0
        q_l = qf * jnp.exp(jnp.where(upper, b - r, NEG))
        k_l = kf * jnp.exp(jnp.where(upper, NEG, r - b))
        a_l = _dot_nt(stack_heads(q_l), k_l.astype(BF16)).reshape(GLA_HEADS, BLK, BLK)
        a = jnp.where(lvl == li, a_l, a)
    a = a.astype(BF16)

    k_out_t = (kf * jnp.exp(b_last - b)).T.astype(BF16)
    decay_t = jnp.broadcast_to(jnp.exp(b_last), (BLK, GLA_DK)).T

    hn = hn_ref[...]
    new_state = []
    for h in range(GLA_HEADS):
        v_h = gla_ref[:, 2 * GLA_DK + h * GLA_DV_HEAD:2 * GLA_DK + (h + 1) * GLA_DV_HEAD]
        o_h = o_inter[h * BLK:(h + 1) * BLK, :] + _dot(a[h], v_h)
        var = jnp.mean(o_h * o_h, axis=-1, keepdims=True)
        o_h = o_h * lax.rsqrt(var + EPS) * hn
        gate = gr_ref[:, h * GLA_DV_HEAD:(h + 1) * GLA_DV_HEAD].astype(F32)
        y_ref[:, h * GLA_DV_HEAD:(h + 1) * GLA_DV_HEAD] = (o_h * gate).astype(BF16)
        new_state.append(_dot(k_out_t[h * GLA_DK_HEAD:(h + 1) * GLA_DK_HEAD, :], v_h))
    state[...] = state[...] * decay_t + jnp.concatenate(new_state, axis=0)

    @pl.when(is_meta)
    def _():
        state_meta[...] = state[...]


def _gla(gla_qkv, loga, gr, head_norm, nblocks_x, nb):
    tri, sel, lvl = _gla_tables()
    const = lambda s: (0, 0)
    blk = _seq_block_index(nblocks_x)
    return pl.pallas_call(
        functools.partial(_gla_kernel, nb),
        grid=(nblocks_x + 1,),
        in_specs=[
            pl.BlockSpec((BLK, C_GLA), blk),
            pl.BlockSpec((BLK, GLA_DK), blk),
            pl.BlockSpec((BLK, C_GR), blk),
            pl.BlockSpec((BLK, BLK), const),
            pl.BlockSpec((N_LEVELS * BLK, BLK), const),
            pl.BlockSpec((BLK, BLK), const),
            pl.BlockSpec((1, GLA_DV_HEAD), const),
        ],
        out_specs=pl.BlockSpec((BLK, GLA_DV), blk),
        out_shape=jax.ShapeDtypeStruct((gla_qkv.shape[0], GLA_DV), BF16),
        scratch_shapes=[pltpu.VMEM((GLA_DK, GLA_DV_HEAD), F32)] * 2,
        compiler_params=pltpu.CompilerParams(
            dimension_semantics=("arbitrary",), vmem_limit_bytes=VMEM_LIMIT),
        name="gla",
    )(gla_qkv, loga, gr, tri, sel, lvl, head_norm)


def _merge_kernel(h_ref, ya_ref, yg_ref, gate_ref, wpa_ref, wpg_ref, wo_ref, out_ref):
    pa = _dot(ya_ref[...], wpa_ref[...])
    pg = _dot(yg_ref[...], wpg_ref[...])
    mixed = (gate_ref[:, :D_MODEL].astype(F32) * pa
             + gate_ref[:, D_MODEL:].astype(F32) * pg).astype(BF16)
    out_ref[...] = h_ref[...] + _dot(mixed, wo_ref[...])


def _merge(h_all, y_att, y_gla, gates, wpa, wpg, wo, tm):
    rows = h_all.shape[0]
    assert rows % tm == 0
    const = lambda i: (0, 0)
    row = lambda i: (i, 0)
    return pl.pallas_call(
        _merge_kernel,
        grid=(rows // tm,),
        in_specs=[
            pl.BlockSpec((tm, D_MODEL), row),
            pl.BlockSpec((tm, ATT_Q), row),
            pl.BlockSpec((tm, GLA_DV), row),
            pl.BlockSpec((tm, C_GATE), row),
            pl.BlockSpec((ATT_Q, D_MODEL), const),
            pl.BlockSpec((GLA_DV, D_MODEL), const),
            pl.BlockSpec((D_MODEL, D_MODEL), const),
        ],
        out_specs=pl.BlockSpec((tm, D_MODEL), row),
        out_shape=jax.ShapeDtypeStruct((rows, D_MODEL), F32),
        compiler_params=pltpu.CompilerParams(
            dimension_semantics=("arbitrary",), vmem_limit_bytes=VMEM_LIMIT),
        name="merge_outproj",
    )(h_all, y_att, y_gla, gates, wpa, wpg, wo)


def _ffn_kernel(tm, n_chunks, h_ref, halo_ref, g_ref, wup_ref, cw_ref, cb_ref, wdn_ref,
                gf_ref, out_ref, u_scr):
    def normed(x):
        var = jnp.mean(x * x, axis=-1, keepdims=True)
        return (x * lax.rsqrt(var + EPS) * g_ref[...]).astype(BF16)

    h = h_ref[...]
    u_scr[0:HALO, :] = normed(halo_ref[...])
    u_scr[HALO:, :] = normed(h)
    u = u_scr[...]

    fc = D_FF // n_chunks
    acc = None
    for c in range(n_chunks):
        a = _dot(u, wup_ref[:, c * fc:(c + 1) * fc])
        v = _dot(u[HALO:], wup_ref[:, D_FF + c * fc:D_FF + (c + 1) * fc])
        cw = cw_ref[:, c * fc:(c + 1) * fc]
        conv = (cw[0:1] * pltpu.roll(a, 2, 0) + cw[1:2] * pltpu.roll(a, 1, 0) + cw[2:3] * a
                + cb_ref[:, c * fc:(c + 1) * fc])[HALO:]
        inner = 0.7978845608028654 * (conv + 0.044715 * (conv * conv * conv))
        act = conv * (0.5 * (1.0 + jnp.tanh(inner)))
        part = _dot((act * v).astype(BF16), wdn_ref[c * fc:(c + 1) * fc, :])
        acc = part if acc is None else acc + part

    h2 = h + acc
    var = jnp.mean(h2 * h2, axis=-1, keepdims=True)
    out_ref[...] = h2 * lax.rsqrt(var + EPS) * gf_ref[...]


def _ffn(h_mid, ffn_norm, wup, conv_w, conv_b, wdn, final_norm, rows_x, seq, tm, n_chunks):
    assert seq % tm == 0 and rows_x % seq == 0
    tiles_per_seq = seq // tm
    meta_tail = (rows_x + BLK) // HALO - 1
    const = lambda i: (0, 0)
    row = lambda i: (i, 0)
    halo = lambda i: (jnp.where(lax.rem(i, tiles_per_seq) == 0, meta_tail,
                                i * (tm // HALO) - 1), 0)
    return pl.pallas_call(
        functools.partial(_ffn_kernel, tm, n_chunks),
        grid=(rows_x // tm,),
        in_specs=[
            pl.BlockSpec((tm, D_MODEL), row),
            pl.BlockSpec((HALO, D_MODEL), halo),
            pl.BlockSpec((1, D_MODEL), const),
            pl.BlockSpec((D_MODEL, 2 * D_FF), const),
            pl.BlockSpec((8, D_FF), const),
            pl.BlockSpec((1, D_FF), const),
            pl.BlockSpec((D_FF, D_MODEL), const),
            pl.BlockSpec((1, D_MODEL), const),
        ],
        out_specs=pl.BlockSpec((tm, D_MODEL), row),
        out_shape=jax.ShapeDtypeStruct((rows_x, D_MODEL), F32),
        scratch_shapes=[pltpu.VMEM((HALO + tm, D_MODEL), BF16)],
        compiler_params=pltpu.CompilerParams(
            dimension_semantics=("arbitrary",), vmem_limit_bytes=VMEM_LIMIT),
        name="conv_ffn",
    )(h_mid, h_mid, ffn_norm, wup, conv_w, conv_b, wdn, final_norm)


def _rope_tables(nb):
    half = HEAD_DIM // 2
    inv_freq = ROPE_THETA ** (-jnp.arange(half, dtype=F32) / half)
    pos = (jnp.arange((nb + 1) * BLK) - META_PAD).astype(F32)
    ang = pos[:, None] * inv_freq[None, :]
    cos2 = jnp.tile(jnp.cos(ang), (1, LANES // half))
    sin = jnp.sin(ang)
    sin2 = jnp.tile(jnp.concatenate([-sin, sin], axis=-1), (1, LANES // HEAD_DIM))
    return cos2, sin2


def kernel(x, meta_tokens, mix_norm, w_in, b_in, w_alpha, b_alpha, attn_sinks, gla_head_norm,
           w_proj_attn, w_proj_gla, w_out, ffn_norm, w_up, conv_w, conv_b, w_down, final_norm):
    batch, seq, _ = x.shape
    rows_x = batch * seq
    nb = seq // BLK
    nblocks_x = rows_x // BLK
    l = 0

    meta_block = jnp.concatenate(
        [jnp.zeros((META_PAD, D_MODEL), x.dtype), meta_tokens.astype(x.dtype)], axis=0)
    h_all = jnp.concatenate([x.reshape(rows_x, D_MODEL), meta_block], axis=0)

    lo0 = ATT_Q + 2 * ATT_KV + 2 * GLA_DK + 2 * GLA_DV
    wi, bi = w_in[l], b_in[l]
    pad_w = jnp.zeros((D_MODEL, C_LOW - GLA_RANK), wi.dtype)
    w_cat = jnp.concatenate(
        [wi[:, :lo0], wi[:, lo0 + GLA_RANK:], wi[:, lo0:lo0 + GLA_RANK], pad_w], axis=1).astype(BF16)
    b_cat = jnp.concatenate(
        [bi[:lo0], bi[lo0 + GLA_RANK:], bi[lo0:lo0 + GLA_RANK],
         jnp.zeros((C_LOW - GLA_RANK,), bi.dtype)])[None, :]
    wal = jnp.concatenate(
        [w_alpha[l], jnp.zeros((C_LOW - GLA_RANK, GLA_DK), w_alpha.dtype)], axis=0)
    wal_hi = wal.astype(BF16)
    wal_lo = (wal - wal_hi.astype(F32)).astype(BF16)

    qkv, gla_qkv, gr, gates, loga = _inproj(
        h_all, mix_norm[l][None, :], w_cat, b_cat, wal_hi, wal_lo, b_alpha[l][None, :], tm=384)

    cos2, sin2 = _rope_tables(nb)
    y_att = _attention(qkv, attn_sinks[l], cos2, sin2, nblocks_x, nb)
    y_gla = _gla(gla_qkv, loga, gr, gla_head_norm[l][None, :], nblocks_x, nb)

    h_mid = _merge(h_all, y_att, y_gla, gates, w_proj_attn[l].astype(BF16),
                   w_proj_gla[l].astype(BF16), w_out[l].astype(BF16), tm=384)

    cw = jnp.concatenate([conv_w[l], jnp.zeros((8 - conv_w.shape[1], D_FF), conv_w.dtype)], axis=0)
    out = _ffn(h_mid, ffn_norm[l][None, :], w_up[l].astype(BF16), cw, conv_b[l][None, :],
               w_down[l].astype(BF16), final_norm[None, :], rows_x, seq, tm=512, n_chunks=2)
    return out.reshape(batch, seq, D_MODEL)
```

```python
import functools

import numpy as np
import jax
import jax.numpy as jnp
from jax import lax
from jax.experimental import pallas as pl
from jax.experimental.pallas import tpu as pltpu

D_MODEL = 1024
N_META = 16
EPS = 1e-6
ATT_HEADS = 8
ATT_KV_HEADS = 2
HEAD_DIM = 64
BLK = 128
ROPE_THETA = 10000.0
GLA_HEADS = 4
GLA_DK = 256
GLA_DV = 512
GLA_DK_HEAD = GLA_DK // GLA_HEADS
GLA_DV_HEAD = GLA_DV // GLA_HEADS
GLA_RANK = 16
GLA_TAU = 16.0
D_FF = 2816
ATT_Q = ATT_HEADS * HEAD_DIM
ATT_KV = ATT_KV_HEADS * HEAD_DIM

META_PAD = BLK - N_META
LANES = 128
HALO = 16
NEG = -1e30
VMEM_LIMIT = 56 * 1024 * 1024

C_QKV = ATT_Q + 2 * ATT_KV
C_GLA = 2 * GLA_DK + GLA_DV
C_GR = GLA_DV
C_GATE = 2 * D_MODEL
C_LOW = LANES
O_QKV = 0
O_GLA = O_QKV + C_QKV
O_GR = O_GLA + C_GLA
O_GATE = O_GR + C_GR
O_LOW = O_GATE + C_GATE
C_ALL = O_LOW + C_LOW

GLA_LEVELS = (64, 32, 16, 8, 4, 2, 1)
N_LEVELS = len(GLA_LEVELS)

BF16 = jnp.bfloat16
F32 = jnp.float32


def _sigmoid(x):
    return 1.0 / (1.0 + jnp.exp(-x))


def _dot(a, b):
    return jnp.dot(a, b, preferred_element_type=F32)


def _dot_nt(a, b):
    return lax.dot_general(a, b, (((1,), (1,)), ((), ())), preferred_element_type=F32)


def _split3(x):
    hi = x.astype(BF16)
    r1 = x - hi.astype(F32)
    mid = r1.astype(BF16)
    lo = (r1 - mid.astype(F32)).astype(BF16)
    return hi, mid, lo


def _params():
    return pltpu.CompilerParams(dimension_semantics=("arbitrary",), vmem_limit_bytes=VMEM_LIMIT)


def _inproj_kernel(n_x_tiles, x_ref, meta_ref, g_ref, w_ref, b_ref, wal_hi_ref, wal_lo_ref,
                   bal_ref, qkv_ref, gla_ref, gr_ref, gate_ref, loga_ref):
    x = jnp.where(pl.program_id(0) == n_x_tiles, meta_ref[...], x_ref[...])
    var = jnp.mean(x * x, axis=-1, keepdims=True)
    u = (x * lax.rsqrt(var + EPS) * g_ref[...]).astype(BF16)

    def proj(off, width):
        return _dot(u, w_ref[:, off:off + width]) + b_ref[:, off:off + width]

    qkv_ref[...] = proj(O_QKV, C_QKV).astype(BF16)
    gla_ref[...] = proj(O_GLA, C_GLA).astype(BF16)
    gr = proj(O_GR, C_GR)
    gr_ref[...] = (gr * _sigmoid(gr)).astype(BF16)
    for half in range(2):
        off = half * D_MODEL
        gate_ref[:, off:off + D_MODEL] = _sigmoid(proj(O_GATE + off, D_MODEL)).astype(BF16)

    low = proj(O_LOW, C_LOW)
    low_hi = low.astype(BF16)
    low_lo = (low - low_hi.astype(F32)).astype(BF16)
    z = (_dot(low_hi, wal_hi_ref[...]) + _dot(low_lo, wal_hi_ref[...])
         + _dot(low_hi, wal_lo_ref[...]) + bal_ref[...])
    log_sig = jnp.minimum(z, 0.0) - jnp.log1p(jnp.exp(-jnp.abs(z)))
    loga_ref[...] = log_sig * (1.0 / GLA_TAU)


def _inproj(x_rows, meta_tile, mix_norm, w_cat, b_cat, wal_hi, wal_lo, b_alpha, tm):
    rows_x = x_rows.shape[0]
    assert rows_x % tm == 0 and meta_tile.shape[0] == tm
    n_x_tiles = rows_x // tm
    rows = rows_x + tm
    const = lambda i: (0, 0)
    row = lambda i: (i, 0)
    return pl.pallas_call(
        functools.partial(_inproj_kernel, n_x_tiles),
        grid=(n_x_tiles + 1,),
        in_specs=[
            pl.BlockSpec((tm, D_MODEL), lambda i: (jnp.minimum(i, n_x_tiles - 1), 0)),
            pl.BlockSpec((tm, D_MODEL), const),
            pl.BlockSpec((1, D_MODEL), const),
            pl.BlockSpec((D_MODEL, C_ALL), const),
            pl.BlockSpec((1, C_ALL), const),
            pl.BlockSpec((LANES, GLA_DK), const),
            pl.BlockSpec((LANES, GLA_DK), const),
            pl.BlockSpec((1, GLA_DK), const),
        ],
        out_specs=[
            pl.BlockSpec((tm, C_QKV), row),
            pl.BlockSpec((tm, C_GLA), row),
            pl.BlockSpec((tm, C_GR), row),
            pl.BlockSpec((tm, C_GATE), row),
            pl.BlockSpec((tm, GLA_DK), row),
        ],
        out_shape=[
            jax.ShapeDtypeStruct((rows, C_QKV), BF16),
            jax.ShapeDtypeStruct((rows, C_GLA), BF16),
            jax.ShapeDtypeStruct((rows, C_GR), BF16),
            jax.ShapeDtypeStruct((rows, C_GATE), BF16),
            jax.ShapeDtypeStruct((rows, GLA_DK), F32),
        ],
        compiler_params=_params(),
        name="inproj",
    )(x_rows, meta_tile, mix_norm, w_cat, b_cat, wal_hi, wal_lo, b_alpha)


def _attn_block(is_meta, first_x, sinks_ref, qkv_ref, cos_ref, sin_ref, kprev, vprev, kmeta, vmeta):
    cos = cos_ref[...]
    sin = sin_ref[...]
    lane = lax.broadcasted_iota(jnp.int32, (BLK, LANES), 1)
    first_half = (lane & (HEAD_DIM - 1)) < HEAD_DIM // 2

    def rope(xc):
        rot = jnp.where(first_half, pltpu.roll(xc, LANES - HEAD_DIM // 2, 1),
                        pltpu.roll(xc, HEAD_DIM // 2, 1))
        return xc * cos + rot * sin

    k_cur = rope(qkv_ref[:, ATT_Q:ATT_Q + ATT_KV].astype(F32)).astype(BF16)
    v_cur = qkv_ref[:, ATT_Q + ATT_KV:ATT_Q + 2 * ATT_KV]

    q_rows = []
    for hg in range(ATT_HEADS):
        chunk, off, grp = hg // 2, hg % 2, hg // (ATT_HEADS // ATT_KV_HEADS)
        qc = rope(qkv_ref[:, chunk * LANES:(chunk + 1) * LANES].astype(F32))
        if off != grp:
            qc = pltpu.roll(qc, HEAD_DIM, 1)
        on_group = (lane >= grp * HEAD_DIM) & (lane < (grp + 1) * HEAD_DIM)
        q_rows.append(jnp.where(on_group, qc * (HEAD_DIM ** -0.5), 0.0).astype(BF16))
    q_all = jnp.concatenate(q_rows, axis=0)

    k_prev = jnp.where(first_x, kmeta[...], kprev[...])
    v_prev = jnp.where(first_x, vmeta[...], vprev[...])
    kk = jnp.concatenate([k_prev, k_cur], axis=0)
    vv = jnp.concatenate([v_prev, v_cur], axis=0)
    sc = _dot_nt(q_all, kk).reshape(ATT_HEADS, BLK, 2 * BLK)

    rq = lax.broadcasted_iota(jnp.int32, (BLK, 2 * BLK), 0)
    ck = lax.broadcasted_iota(jnp.int32, (BLK, 2 * BLK), 1)
    in_prev = ck < BLK
    rk = jnp.where(in_prev, ck, ck - BLK)
    cur_min = jnp.where(is_meta, META_PAD, 0)
    prev_min = jnp.where(is_meta, BLK, jnp.where(first_x, META_PAD, 0))
    rk_lo = jnp.where(in_prev, jnp.maximum(rq + 1, prev_min), cur_min)
    rk_hi = jnp.where(in_prev, BLK - 1, rq)
    valid = (rk >= rk_lo) & (rk <= rk_hi)
    sc = jnp.where(valid[None], sc, NEG)

    hid = lax.broadcasted_iota(jnp.int32, (ATT_HEADS, 1, 1), 0)
    sink = jnp.zeros((ATT_HEADS, 1, 1), F32)
    for hg in range(ATT_HEADS):
        sink = jnp.where(hid == hg, sinks_ref[hg], sink)

    m = jnp.maximum(jnp.max(sc, axis=-1, keepdims=True), sink)
    p = jnp.exp(sc - m)
    denom = jnp.sum(p, axis=-1, keepdims=True) + jnp.exp(sink - m)
    o = _dot(p.astype(BF16).reshape(ATT_HEADS * BLK, 2 * BLK), vv)
    o = o.reshape(ATT_HEADS, BLK, LANES) * (1.0 / denom)

    chunks = []
    for chunk in range(ATT_HEADS // 2):
        grp = (2 * chunk) // (ATT_HEADS // ATT_KV_HEADS)
        o0, o1 = o[2 * chunk], o[2 * chunk + 1]
        if grp != 0:
            o0 = pltpu.roll(o0, HEAD_DIM, 1)
        if grp != 1:
            o1 = pltpu.roll(o1, HEAD_DIM, 1)
        chunks.append(jnp.where(lane < HEAD_DIM, o0, o1).astype(BF16))

    kprev[...] = k_cur
    vprev[...] = v_cur
    kmeta[...] = jnp.where(is_meta, k_cur, kmeta[...])
    vmeta[...] = jnp.where(is_meta, v_cur, vmeta[...])
    return jnp.concatenate(chunks, axis=1)


def _gla_tables():
    t = np.arange(BLK)
    tri = (t[:, None] >= t[None, :]).astype(np.float32)
    sel = np.zeros((N_LEVELS * BLK, BLK), np.float32)
    for li, half in enumerate(GLA_LEVELS):
        boundary = (t & ~(2 * half - 1)) + half - 1
        sel[li * BLK + t, boundary] = 1.0
    x = t[:, None] ^ t[None, :]
    top = np.floor(np.log2(np.maximum(x, 1))).astype(np.int64)
    lvl = np.array([GLA_LEVELS.index(1 << int(v)) for v in top.ravel()]).reshape(BLK, BLK)
    lvl = np.where(t[:, None] == t[None, :], N_LEVELS, lvl)
    lvl = np.where(t[:, None] < t[None, :], N_LEVELS + 1, lvl)
    return (jnp.asarray(tri, BF16), jnp.asarray(sel, BF16), jnp.asarray(lvl, jnp.int32))


def _gla_block(is_meta, first_x, gla_ref, loga_ref, gr_ref, tri_ref, sel_ref, lvl_ref, hn_ref,
               state, state_meta):
    row = lax.broadcasted_iota(jnp.int32, (BLK, GLA_DK), 0)
    lane = lax.broadcasted_iota(jnp.int32, (BLK, GLA_DK), 1)
    is_token = row >= jnp.where(is_meta, META_PAD, 0)

    g3 = jnp.concatenate(_split3(loga_ref[...]), axis=1)
    b3 = _dot(tri_ref[...], g3)
    b = b3[:, :GLA_DK] + b3[:, GLA_DK:2 * GLA_DK] + b3[:, 2 * GLA_DK:]
    b_last = b[BLK - 1:BLK, :]

    b_hi = b.astype(BF16)
    b_mid = (b - b_hi.astype(F32)).astype(BF16)
    refs = _dot(sel_ref[...], jnp.concatenate([b_hi, b_mid], axis=1))

    qf = gla_ref[:, :GLA_DK].astype(F32) * (GLA_DK_HEAD ** -0.5)
    kf = jnp.where(is_token, gla_ref[:, GLA_DK:2 * GLA_DK].astype(F32), 0.0)

    head_lanes = [(lane >= h * GLA_DK_HEAD) & (lane < (h + 1) * GLA_DK_HEAD)
                  for h in range(GLA_HEADS)]

    def stack_heads(xq):
        return jnp.concatenate([jnp.where(hm, xq, 0.0).astype(BF16) for hm in head_lanes],
                               axis=0)

    state_in = jnp.where(first_x, state_meta[...], state[...])
    o_inter = _dot(stack_heads(qf * jnp.exp(b)), state_in.astype(BF16))

    lvl = lvl_ref[...][None]
    a = jnp.zeros((GLA_HEADS, BLK, BLK), F32)
    a_diag = _dot_nt(stack_heads(qf), kf.astype(BF16)).reshape(GLA_HEADS, BLK, BLK)
    a = jnp.where(lvl == N_LEVELS, a_diag, a)
    for li, half in enumerate(GLA_LEVELS):
        r = (refs[li * BLK:(li + 1) * BLK, :GLA_DK]
             + refs[li * BLK:(li + 1) * BLK, GLA_DK:])
        upper = (row & half) != 0
        q_l = qf * jnp.exp(jnp.where(upper, b - r, NEG))
        k_l = kf * jnp.exp(jnp.where(upper, NEG, r - b))
        a_l = _dot_nt(stack_heads(q_l), k_l.astype(BF16)).reshape(GLA_HEADS, BLK, BLK)
        a = jnp.where(lvl == li, a_l, a)
    a = a.astype(BF16)

    k_out_t = (kf * jnp.exp(b_last - b)).T.astype(BF16)
    decay_t = jnp.broadcast_to(jnp.exp(b_last), (BLK, GLA_DK)).T

    hn = hn_ref[...]
    new_state, heads = [], []
    for h in range(GLA_HEADS):
        v_h = gla_ref[:, 2 * GLA_DK + h * GLA_DV_HEAD:2 * GLA_DK + (h + 1) * GLA_DV_HEAD]
        o_h = o_inter[h * BLK:(h + 1) * BLK, :] + _dot(a[h], v_h)
        var = jnp.mean(o_h * o_h, axis=-1, keepdims=True)
        o_h = o_h * lax.rsqrt(var + EPS) * hn
        gate = gr_ref[:, h * GLA_DV_HEAD:(h + 1) * GLA_DV_HEAD].astype(F32)
        heads.append((o_h * gate).astype(BF16))
        new_state.append(_dot(k_out_t[h * GLA_DK_HEAD:(h + 1) * GLA_DK_HEAD, :], v_h))
    state_new = state_in * decay_t + jnp.concatenate(new_state, axis=0)
    state[...] = state_new
    state_meta[...] = jnp.where(is_meta, state_new, state_meta[...])
    return jnp.concatenate(heads, axis=1)


def _mixer_kernel(nb, sinks_ref, x_ref, meta_ref, qkv_ref, gla_ref, gr_ref, gate_ref, loga_ref,
                  cos_ref, sin_ref, tri_ref, sel_ref, lvl_ref, hn_ref, wpa_ref, wpg_ref, wo_ref,
                  out_ref, kprev, vprev, kmeta, vmeta, state, state_meta):
    s = pl.program_id(0)
    is_meta = s == 0
    first_x = jnp.logical_and(s >= 1, lax.rem(jnp.maximum(s - 1, 0), nb) == 0)

    @pl.when(is_meta)
    def _():
        for ref in (kprev, vprev, kmeta, vmeta, state, state_meta):
            ref[...] = jnp.zeros_like(ref)

    y_att = _attn_block(is_meta, first_x, sinks_ref, qkv_ref, cos_ref, sin_ref,
                        kprev, vprev, kmeta, vmeta)
    y_gla = _gla_block(is_meta, first_x, gla_ref, loga_ref, gr_ref, tri_ref, sel_ref, lvl_ref,
                       hn_ref, state, state_meta)

    pa = _dot(y_att, wpa_ref[...])
    pg = _dot(y_gla, wpg_ref[...])
    mixed = (gate_ref[:, :D_MODEL].astype(F32) * pa
             + gate_ref[:, D_MODEL:].astype(F32) * pg).astype(BF16)
    h = jnp.where(is_meta, meta_ref[...], x_ref[...])
    out_ref[...] = h + _dot(mixed, wo_ref[...])


def _mixer(x_rows, meta_block, qkv, gla_qkv, gr, gates, loga, sinks, cos2, sin2, head_norm,
           wpa, wpg, wo, nb):
    rows_x = x_rows.shape[0]
    nblocks_x = rows_x // BLK
    tri, sel, lvl = _gla_tables()
    const = lambda s: (0, 0)
    blk = lambda s: (jnp.where(s == 0, nblocks_x, s - 1), 0)
    xblk = lambda s: (jnp.maximum(s - 1, 0), 0)
    pos = lambda s: (jnp.where(s == 0, 0, lax.rem(jnp.maximum(s - 1, 0), nb) + 1), 0)
    return pl.pallas_call(
        functools.partial(_mixer_kernel, nb),
        grid=(nblocks_x + 1,),
        in_specs=[
            pl.BlockSpec(memory_space=pltpu.SMEM),
            pl.BlockSpec((BLK, D_MODEL), xblk),
            pl.BlockSpec((BLK, D_MODEL), const),
            pl.BlockSpec((BLK, C_QKV), blk),
            pl.BlockSpec((BLK, C_GLA), blk),
            pl.BlockSpec((BLK, C_GR), blk),
            pl.BlockSpec((BLK, C_GATE), blk),
            pl.BlockSpec((BLK, GLA_DK), blk),
            pl.BlockSpec((BLK, LANES), pos),
            pl.BlockSpec((BLK, LANES), pos),
            pl.BlockSpec((BLK, BLK), const),
            pl.BlockSpec((N_LEVELS * BLK, BLK), const),
            pl.BlockSpec((BLK, BLK), const),
            pl.BlockSpec((1, GLA_DV_HEAD), const),
            pl.BlockSpec((ATT_Q, D_MODEL), const),
            pl.BlockSpec((GLA_DV, D_MODEL), const),
            pl.BlockSpec((D_MODEL, D_MODEL), const),
        ],
        out_specs=pl.BlockSpec((BLK, D_MODEL), blk),
        out_shape=jax.ShapeDtypeStruct((rows_x + BLK, D_MODEL), F32),
        scratch_shapes=[pltpu.VMEM((BLK, ATT_KV), BF16)] * 4
        + [pltpu.VMEM((GLA_DK, GLA_DV_HEAD), F32)] * 2,
        compiler_params=_params(),
        name="mixer",
    )(sinks, x_rows, meta_block, qkv, gla_qkv, gr, gates, loga, cos2, sin2, tri, sel, lvl,
      head_norm, wpa, wpg, wo)


def _ffn_kernel(tm, n_chunks, h_ref, halo_ref, g_ref, wup_ref, cw_ref, cb_ref, wdn_ref,
                gf_ref, out_ref, u_scr):
    def normed(x):
        var = jnp.mean(x * x, axis=-1, keepdims=True)
        return (x * lax.rsqrt(var + EPS) * g_ref[...]).astype(BF16)

    h = h_ref[...]
    u_scr[0:HALO, :] = normed(halo_ref[...])
    u_scr[HALO:, :] = normed(h)
    u = u_scr[...]

    fc = D_FF // n_chunks
    acc = None
    for c in range(n_chunks):
        a = _dot(u, wup_ref[:, c * fc:(c + 1) * fc])
        v = _dot(u[HALO:], wup_ref[:, D_FF + c * fc:D_FF + (c + 1) * fc])
        cw = cw_ref[:, c * fc:(c + 1) * fc]
        conv = (cw[0:1] * pltpu.roll(a, 2, 0) + cw[1:2] * pltpu.roll(a, 1, 0) + cw[2:3] * a
                + cb_ref[:, c * fc:(c + 1) * fc])[HALO:]
        inner = 0.7978845608028654 * (conv + 0.044715 * (conv * conv * conv))
        act = conv * (0.5 * (1.0 + jnp.tanh(inner)))
        part = _dot((act * v).astype(BF16), wdn_ref[c * fc:(c + 1) * fc, :])
        acc = part if acc is None else acc + part

    h2 = h + acc
    var = jnp.mean(h2 * h2, axis=-1, keepdims=True)
    out_ref[...] = h2 * lax.rsqrt(var + EPS) * gf_ref[...]


def _ffn(h_mid, ffn_norm, wup, conv_w, conv_b, wdn, final_norm, rows_x, seq, tm, n_chunks):
    assert seq % tm == 0 and rows_x % seq == 0
    tiles_per_seq = seq // tm
    meta_tail = (rows_x + BLK) // HALO - 1
    const = lambda i: (0, 0)
    row = lambda i: (i, 0)
    halo = lambda i: (jnp.where(lax.rem(i, tiles_per_seq) == 0, meta_tail,
                                i * (tm // HALO) - 1), 0)
    return pl.pallas_call(
        functools.partial(_ffn_kernel, tm, n_chunks),
        grid=(rows_x // tm,),
        in_specs=[
            pl.BlockSpec((tm, D_MODEL), row),
            pl.BlockSpec((HALO, D_MODEL), halo),
            pl.BlockSpec((1, D_MODEL), const),
            pl.BlockSpec((D_MODEL, 2 * D_FF), const),
            pl.BlockSpec((8, D_FF), const),
            pl.BlockSpec((1, D_FF), const),
            pl.BlockSpec((D_FF, D_MODEL), const),
            pl.BlockSpec((1, D_MODEL), const),
        ],
        out_specs=pl.BlockSpec((tm, D_MODEL), row),
        out_shape=jax.ShapeDtypeStruct((rows_x, D_MODEL), F32),
        scratch_shapes=[pltpu.VMEM((HALO + tm, D_MODEL), BF16)],
        compiler_params=_params(),
        name="conv_ffn",
    )(h_mid, h_mid, ffn_norm, wup, conv_w, conv_b, wdn, final_norm)


def _rope_tables(nb):
    half = HEAD_DIM // 2
    inv_freq = ROPE_THETA ** (-jnp.arange(half, dtype=F32) / half)
    pos = (jnp.arange((nb + 1) * BLK) - META_PAD).astype(F32)
    ang = pos[:, None] * inv_freq[None, :]
    cos2 = jnp.tile(jnp.cos(ang), (1, LANES // half))
    sin = jnp.sin(ang)
    sin2 = jnp.tile(jnp.concatenate([-sin, sin], axis=-1), (1, LANES // HEAD_DIM))
    return cos2, sin2


def kernel(x, meta_tokens, mix_norm, w_in, b_in, w_alpha, b_alpha, attn_sinks, gla_head_norm,
           w_proj_attn, w_proj_gla, w_out, ffn_norm, w_up, conv_w, conv_b, w_down, final_norm):
    batch, seq, _ = x.shape
    rows_x = batch * seq
    nb = seq // BLK
    l = 0
    tm_in = 512

    x_rows = x.reshape(rows_x, D_MODEL)
    meta_block = jnp.concatenate(
        [jnp.zeros((META_PAD, D_MODEL), x.dtype), meta_tokens.astype(x.dtype)], axis=0)
    meta_tile = jnp.concatenate(
        [meta_block, jnp.zeros((tm_in - BLK, D_MODEL), x.dtype)], axis=0)

    lo0 = ATT_Q + 2 * ATT_KV + 2 * GLA_DK + 2 * GLA_DV
    wi, bi = w_in[l], b_in[l]
    pad_w = jnp.zeros((D_MODEL, C_LOW - GLA_RANK), wi.dtype)
    w_cat = jnp.concatenate(
        [wi[:, :lo0], wi[:, lo0 + GLA_RANK:], wi[:, lo0:lo0 + GLA_RANK], pad_w], axis=1).astype(BF16)
    b_cat = jnp.concatenate(
        [bi[:lo0], bi[lo0 + GLA_RANK:], bi[lo0:lo0 + GLA_RANK],
         jnp.zeros((C_LOW - GLA_RANK,), bi.dtype)])[None, :]
    wal = jnp.concatenate(
        [w_alpha[l], jnp.zeros((C_LOW - GLA_RANK, GLA_DK), w_alpha.dtype)], axis=0)
    wal_hi = wal.astype(BF16)
    wal_lo = (wal - wal_hi.astype(F32)).astype(BF16)

    qkv, gla_qkv, gr, gates, loga = _inproj(
        x_rows, meta_tile, mix_norm[l][None, :], w_cat, b_cat, wal_hi, wal_lo,
        b_alpha[l][None, :], tm=tm_in)

    cos2, sin2 = _rope_tables(nb)
    h_mid = _mixer(x_rows, meta_block, qkv, gla_qkv, gr, gates, loga, attn_sinks[l], cos2, sin2,
                   gla_head_norm[l][None, :], w_proj_attn[l].astype(BF16),
                   w_proj_gla[l].astype(BF16), w_out[l].astype(BF16), nb)

    cw = jnp.concatenate([conv_w[l], jnp.zeros((8 - conv_w.shape[1], D_FF), conv_w.dtype)], axis=0)
    out = _ffn(h_mid, ffn_norm[l][None, :], w_up[l].astype(BF16), cw, conv_b[l][None, :],
               w_down[l].astype(BF16), final_norm[None, :], rows_x, seq, tm=512, n_chunks=2)
    return out.reshape(batch, seq, D_MODEL)
```

```python
import functools

import numpy as np
import jax
import jax.numpy as jnp
from jax import lax
from jax.experimental import pallas as pl
from jax.experimental.pallas import tpu as pltpu

D_MODEL = 1024
N_META = 16
EPS = 1e-6
ATT_HEADS = 8
ATT_KV_HEADS = 2
HEAD_DIM = 64
BLK = 128
ROPE_THETA = 10000.0
GLA_HEADS = 4
GLA_DK = 256
GLA_DV = 512
GLA_DK_HEAD = GLA_DK // GLA_HEADS
GLA_DV_HEAD = GLA_DV // GLA_HEADS
GLA_RANK = 16
GLA_TAU = 16.0
D_FF = 2816
ATT_Q = ATT_HEADS * HEAD_DIM
ATT_KV = ATT_KV_HEADS * HEAD_DIM

META_PAD = BLK - N_META
LANES = 128
HALO = 16
NEG = -1e30
VMEM_LIMIT = 56 * 1024 * 1024

C_QKV = ATT_Q + 2 * ATT_KV
C_GLA = 2 * GLA_DK + GLA_DV
C_GR = GLA_DV
C_GATE = 2 * D_MODEL
C_LOW = LANES
O_QKV = 0
O_GLA = O_QKV + C_QKV
O_GR = O_GLA + C_GLA
O_GATE = O_GR + C_GR
O_LOW = O_GATE + C_GATE
C_ALL = O_LOW + C_LOW

GLA_LEVELS = (64, 32, 16, 8, 4, 2, 1)
N_LEVELS = len(GLA_LEVELS)
GLA_FINE = tuple(h for h in GLA_LEVELS if h < 8)

BF16 = jnp.bfloat16
F32 = jnp.float32


def _sigmoid(x):
    return 1.0 / (1.0 + jnp.exp(-x))


def _dot(a, b):
    return jnp.dot(a, b, preferred_element_type=F32)


def _dot_nt(a, b):
    return lax.dot_general(a, b, (((1,), (1,)), ((), ())), preferred_element_type=F32)


def _split3(x):
    hi = x.astype(BF16)
    r1 = x - hi.astype(F32)
    mid = r1.astype(BF16)
    lo = (r1 - mid.astype(F32)).astype(BF16)
    return hi, mid, lo


def _params():
    return pltpu.CompilerParams(dimension_semantics=("arbitrary",), vmem_limit_bytes=VMEM_LIMIT)


def _inproj_block(x, g_ref, w_ref, b_ref, wal_hi_ref, wal_lo_ref, bal_ref):
    var = jnp.mean(x * x, axis=-1, keepdims=True)
    u = (x * lax.rsqrt(var + EPS) * g_ref[...]).astype(BF16)

    def proj(off, width):
        return _dot(u, w_ref[:, off:off + width]) + b_ref[:, off:off + width]

    qkv = proj(O_QKV, C_QKV).astype(BF16)
    gla = proj(O_GLA, C_GLA).astype(BF16)
    gr = proj(O_GR, C_GR)
    gr = (gr * _sigmoid(gr)).astype(BF16)
    gate = _sigmoid(proj(O_GATE, C_GATE)).astype(BF16)

    low = proj(O_LOW, C_LOW)
    low_hi = low.astype(BF16)
    low_lo = (low - low_hi.astype(F32)).astype(BF16)
    z = (_dot(low_hi, wal_hi_ref[...]) + _dot(low_lo, wal_hi_ref[...])
         + _dot(low_hi, wal_lo_ref[...]) + bal_ref[...])
    log_sig = jnp.minimum(z, 0.0) - jnp.log1p(jnp.exp(-jnp.abs(z)))
    return qkv, gla, gr, gate, log_sig * (1.0 / GLA_TAU)


def _attn_block(is_meta, first_x, sinks_ref, qkv_ref, cos_ref, sin_ref, kprev, vprev, kmeta, vmeta):
    cos = cos_ref[...]
    sin = sin_ref[...]
    lane = lax.broadcasted_iota(jnp.int32, (BLK, LANES), 1)
    first_half = (lane & (HEAD_DIM - 1)) < HEAD_DIM // 2

    def rope(xc):
        rot = jnp.where(first_half, pltpu.roll(xc, LANES - HEAD_DIM // 2, 1),
                        pltpu.roll(xc, HEAD_DIM // 2, 1))
        return xc * cos + rot * sin

    k_cur = rope(qkv_ref[:, ATT_Q:ATT_Q + ATT_KV].astype(F32)).astype(BF16)
    v_cur = qkv_ref[:, ATT_Q + ATT_KV:ATT_Q + 2 * ATT_KV]

    q_rows = []
    for hg in range(ATT_HEADS):
        chunk, off, grp = hg // 2, hg % 2, hg // (ATT_HEADS // ATT_KV_HEADS)
        qc = rope(qkv_ref[:, chunk * LANES:(chunk + 1) * LANES].astype(F32))
        if off != grp:
            qc = pltpu.roll(qc, HEAD_DIM, 1)
        on_group = (lane >= grp * HEAD_DIM) & (lane < (grp + 1) * HEAD_DIM)
        q_rows.append(jnp.where(on_group, qc * (HEAD_DIM ** -0.5), 0.0).astype(BF16))
    q_all = jnp.concatenate(q_rows, axis=0)

    k_prev = jnp.where(first_x, kmeta[...], kprev[...])
    v_prev = jnp.where(first_x, vmeta[...], vprev[...])
    kk = jnp.concatenate([k_prev, k_cur], axis=0)
    vv = jnp.concatenate([v_prev, v_cur], axis=0)
    sc = _dot_nt(q_all, kk).reshape(ATT_HEADS, BLK, 2 * BLK)

    rq = lax.broadcasted_iota(jnp.int32, (BLK, 2 * BLK), 0)
    ck = lax.broadcasted_iota(jnp.int32, (BLK, 2 * BLK), 1)
    in_prev = ck < BLK
    rk = jnp.where(in_prev, ck, ck - BLK)
    cur_min = jnp.where(is_meta, META_PAD, 0)
    prev_min = jnp.where(is_meta, BLK, jnp.where(first_x, META_PAD, 0))
    rk_lo = jnp.where(in_prev, jnp.maximum(rq + 1, prev_min), cur_min)
    rk_hi = jnp.where(in_prev, BLK - 1, rq)
    valid = (rk >= rk_lo) & (rk <= rk_hi)
    sc = jnp.where(valid[None], sc, NEG)

    hid = lax.broadcasted_iota(jnp.int32, (ATT_HEADS, 1, 1), 0)
    sink = jnp.zeros((ATT_HEADS, 1, 1), F32)
    for hg in range(ATT_HEADS):
        sink = jnp.where(hid == hg, sinks_ref[hg], sink)

    m = jnp.maximum(jnp.max(sc, axis=-1, keepdims=True), sink)
    p = jnp.exp(sc - m)
    denom = jnp.sum(p, axis=-1, keepdims=True) + jnp.exp(sink - m)
    o = _dot(p.astype(BF16).reshape(ATT_HEADS * BLK, 2 * BLK), vv)
    o = o.reshape(ATT_HEADS, BLK, LANES) * (1.0 / denom)

    chunks = []
    for chunk in range(ATT_HEADS // 2):
        grp = (2 * chunk) // (ATT_HEADS // ATT_KV_HEADS)
        o0, o1 = o[2 * chunk], o[2 * chunk + 1]
        if grp != 0:
            o0 = pltpu.roll(o0, HEAD_DIM, 1)
        if grp != 1:
            o1 = pltpu.roll(o1, HEAD_DIM, 1)
        chunks.append(jnp.where(lane < HEAD_DIM, o0, o1).astype(BF16))

    kprev[...] = k_cur
    vprev[...] = v_cur
    kmeta[...] = jnp.where(is_meta, k_cur, kmeta[...])
    vmeta[...] = jnp.where(is_meta, v_cur, vmeta[...])
    return jnp.concatenate(chunks, axis=1)


def _gla_tables():
    t = np.arange(BLK)
    tri = (t[:, None] >= t[None, :]).astype(np.float32)
    sel = np.zeros((len(GLA_FINE) * BLK, BLK), np.float32)
    for li, half in enumerate(GLA_FINE):
        boundary = (t & ~(2 * half - 1)) + half - 1
        sel[li * BLK + t, boundary] = 1.0
    x = t[:, None] ^ t[None, :]
    top = np.floor(np.log2(np.maximum(x, 1))).astype(np.int64)
    lvl = np.array([GLA_LEVELS.index(1 << int(v)) for v in top.ravel()]).reshape(BLK, BLK)
    lvl = np.where(t[:, None] == t[None, :], N_LEVELS, lvl)
    lvl = np.where(t[:, None] < t[None, :], N_LEVELS + 1, lvl)
    lvl = np.tile(lvl, (1, GLA_HEADS))
    return (jnp.asarray(tri, BF16), jnp.asarray(sel, BF16), jnp.asarray(lvl, jnp.int32))


def _gla_block(is_meta, first_x, gla_ref, loga_ref, gr_ref, tri_ref, sel_ref, lvl_ref, hn_ref,
               state, state_meta):
    row = lax.broadcasted_iota(jnp.int32, (BLK, GLA_DK), 0)
    lane = lax.broadcasted_iota(jnp.int32, (BLK, GLA_DK), 1)
    is_token = row >= jnp.where(is_meta, META_PAD, 0)

    g3 = jnp.concatenate(_split3(loga_ref[...]), axis=1)
    b3 = _dot(tri_ref[...], g3)
    b = b3[:, :GLA_DK] + b3[:, GLA_DK:2 * GLA_DK] + b3[:, 2 * GLA_DK:]
    b_last = b[BLK - 1:BLK, :]

    b_hi = b.astype(BF16)
    b_mid = (b - b_hi.astype(F32)).astype(BF16)
    fine = _dot(sel_ref[...], jnp.concatenate([b_hi, b_mid], axis=1))

    def boundary_rows(half):
        if half in GLA_FINE:
            li = GLA_FINE.index(half)
            return fine[li * BLK:(li + 1) * BLK, :GLA_DK] + fine[li * BLK:(li + 1) * BLK, GLA_DK:]
        groups = BLK // (2 * half)
        b_g = b.reshape(groups, 2 * half, GLA_DK)[:, half - 1:half, :]
        return jnp.broadcast_to(b_g, (groups, 2 * half, GLA_DK)).reshape(BLK, GLA_DK)

    qf = gla_ref[:, :GLA_DK].astype(F32) * (GLA_DK_HEAD ** -0.5)
    kf = jnp.where(is_token, gla_ref[:, GLA_DK:2 * GLA_DK].astype(F32), 0.0)

    head_mask = [jnp.where((lane >= h * GLA_DK_HEAD) & (lane < (h + 1) * GLA_DK_HEAD), 1.0, 0.0)
                 .astype(BF16) for h in range(GLA_HEADS)]

    def stack_heads(x):
        xb = x.astype(BF16)
        return jnp.concatenate([xb * hm for hm in head_mask], axis=0)

    state_in = jnp.where(first_x, state_meta[...], state[...])
    o_inter = _dot(stack_heads(qf * jnp.exp(b)), state_in.astype(BF16))

    lvl = lvl_ref[...]
    a = jnp.where(lvl == N_LEVELS, _dot_nt(qf.astype(BF16), stack_heads(kf)), 0.0)
    for li, half in enumerate(GLA_LEVELS):
        r = boundary_rows(half)
        upper = (row & half) != 0
        q_l = qf * jnp.exp(jnp.where(upper, b - r, NEG))
        k_l = kf * jnp.exp(jnp.where(upper, NEG, r - b))
        a = jnp.where(lvl == li, _dot_nt(q_l.astype(BF16), stack_heads(k_l)), a)
    a = a.astype(BF16)

    k_out_t = (kf * jnp.exp(b_last - b)).T.astype(BF16)
    decay_t = jnp.broadcast_to(jnp.exp(b_last), (BLK, GLA_DK)).T

    hn = hn_ref[...]
    new_state, heads = [], []
    for h in range(GLA_HEADS):
        v_h = gla_ref[:, 2 * GLA_DK + h * GLA_DV_HEAD:2 * GLA_DK + (h + 1) * GLA_DV_HEAD]
        o_h = o_inter[h * BLK:(h + 1) * BLK, :] + _dot(a[:, h * BLK:(h + 1) * BLK], v_h)
        var = jnp.mean(o_h * o_h, axis=-1, keepdims=True)
        o_h = o_h * lax.rsqrt(var + EPS) * hn
        gate = gr_ref[:, h * GLA_DV_HEAD:(h + 1) * GLA_DV_HEAD].astype(F32)
        heads.append((o_h * gate).astype(BF16))
        new_state.append(_dot(k_out_t[h * GLA_DK_HEAD:(h + 1) * GLA_DK_HEAD, :], v_h))
    state_new = state_in * decay_t + jnp.concatenate(new_state, axis=0)
    state[...] = state_new
    state_meta[...] = jnp.where(is_meta, state_new, state_meta[...])
    return jnp.concatenate(heads, axis=1)


def _mixer_kernel(nb, sinks_ref, xnext_ref, meta_ref, g_ref, w_ref, b_ref, wal_hi_ref, wal_lo_ref,
                  bal_ref, cos_ref, sin_ref, tri_ref, sel_ref, lvl_ref, hn_ref, wpa_ref, wpg_ref,
                  wo_ref, out_ref,
                  cur_h, cur_qkv, cur_gla, cur_gr, cur_gate, cur_loga,
                  nxt_qkv, nxt_gla, nxt_gr, nxt_gate, nxt_loga,
                  kprev, vprev, kmeta, vmeta, state, state_meta):
    s = pl.program_id(0)
    is_meta = s == 0
    first_x = jnp.logical_and(s >= 1, lax.rem(jnp.maximum(s - 1, 0), nb) == 0)
    cur = (cur_qkv, cur_gla, cur_gr, cur_gate, cur_loga)
    nxt = (nxt_qkv, nxt_gla, nxt_gr, nxt_gate, nxt_loga)
    weights = (g_ref, w_ref, b_ref, wal_hi_ref, wal_lo_ref, bal_ref)

    @pl.when(is_meta)
    def _():
        for ref in (kprev, vprev, kmeta, vmeta, state, state_meta):
            ref[...] = jnp.zeros_like(ref)
        cur_h[...] = meta_ref[...]
        for ref, val in zip(cur, _inproj_block(meta_ref[...], *weights)):
            ref[...] = val

    x_next = xnext_ref[...]
    for ref, val in zip(nxt, _inproj_block(x_next, *weights)):
        ref[...] = val

    y_att = _attn_block(is_meta, first_x, sinks_ref, cur_qkv, cos_ref, sin_ref,
                        kprev, vprev, kmeta, vmeta)
    y_gla = _gla_block(is_meta, first_x, cur_gla, cur_loga, cur_gr, tri_ref, sel_ref, lvl_ref,
                       hn_ref, state, state_meta)

    pa = _dot(y_att, wpa_ref[...])
    pg = _dot(y_gla, wpg_ref[...])
    mixed = (cur_gate[:, :D_MODEL].astype(F32) * pa
             + cur_gate[:, D_MODEL:].astype(F32) * pg).astype(BF16)
    out_ref[...] = cur_h[...] + _dot(mixed, wo_ref[...])

    cur_h[...] = x_next
    for c_ref, n_ref in zip(cur, nxt):
        c_ref[...] = n_ref[...]


def _mixer(x_rows, meta_block, mix_norm, w_cat, b_cat, wal_hi, wal_lo, b_alpha, sinks, cos2, sin2,
           head_norm, wpa, wpg, wo, nb):
    rows_x = x_rows.shape[0]
    nblocks_x = rows_x // BLK
    tri, sel, lvl = _gla_tables()
    const = lambda s: (0, 0)
    blk = lambda s: (jnp.where(s == 0, nblocks_x, s - 1), 0)
    xnext = lambda s: (jnp.minimum(s, nblocks_x - 1), 0)
    pos = lambda s: (jnp.where(s == 0, 0, lax.rem(jnp.maximum(s - 1, 0), nb) + 1), 0)
    proj_bufs = [pltpu.VMEM((BLK, C_QKV), BF16), pltpu.VMEM((BLK, C_GLA), BF16),
                 pltpu.VMEM((BLK, C_GR), BF16), pltpu.VMEM((BLK, C_GATE), BF16),
                 pltpu.VMEM((BLK, GLA_DK), F32)]
    return pl.pallas_call(
        functools.partial(_mixer_kernel, nb),
        grid=(nblocks_x + 1,),
        in_specs=[
            pl.BlockSpec(memory_space=pltpu.SMEM),
            pl.BlockSpec((BLK, D_MODEL), xnext),
            pl.BlockSpec((BLK, D_MODEL), const),
            pl.BlockSpec((1, D_MODEL), const),
            pl.BlockSpec((D_MODEL, C_ALL), const),
            pl.BlockSpec((1, C_ALL), const),
            pl.BlockSpec((LANES, GLA_DK), const),
            pl.BlockSpec((LANES, GLA_DK), const),
            pl.BlockSpec((1, GLA_DK), const),
            pl.BlockSpec((BLK, LANES), pos),
            pl.BlockSpec((BLK, LANES), pos),
            pl.BlockSpec((BLK, BLK), const),
            pl.BlockSpec((len(GLA_FINE) * BLK, BLK), const),
            pl.BlockSpec((BLK, GLA_HEADS * BLK), const),
            pl.BlockSpec((1, GLA_DV_HEAD), const),
            pl.BlockSpec((ATT_Q, D_MODEL), const),
            pl.BlockSpec((GLA_DV, D_MODEL), const),
            pl.BlockSpec((D_MODEL, D_MODEL), const),
        ],
        out_specs=pl.BlockSpec((BLK, D_MODEL), blk),
        out_shape=jax.ShapeDtypeStruct((rows_x + BLK, D_MODEL), F32),
        scratch_shapes=[pltpu.VMEM((BLK, D_MODEL), F32)] + proj_bufs + proj_bufs
        + [pltpu.VMEM((BLK, ATT_KV), BF16)] * 4
        + [pltpu.VMEM((GLA_DK, GLA_DV_HEAD), F32)] * 2,
        compiler_params=_params(),
        name="mixer",
    )(sinks, x_rows, meta_block, mix_norm, w_cat, b_cat, wal_hi, wal_lo, b_alpha, cos2, sin2,
      tri, sel, lvl, head_norm, wpa, wpg, wo)


def _ffn_kernel(tm, n_chunks, h_ref, halo_ref, g_ref, wup_ref, cw_ref, cb_ref, wdn_ref,
                gf_ref, out_ref, u_scr):
    def normed(x):
        var = jnp.mean(x * x, axis=-1, keepdims=True)
        return (x * lax.rsqrt(var + EPS) * g_ref[...]).astype(BF16)

    h = h_ref[...]
    u_scr[0:HALO, :] = normed(halo_ref[...])
    u_scr[HALO:, :] = normed(h)
    u = u_scr[...]

    fc = D_FF // n_chunks
    acc = None
    for c in range(n_chunks):
        a = _dot(u, wup_ref[:, c * fc:(c + 1) * fc])
        v = _dot(u[HALO:], wup_ref[:, D_FF + c * fc:D_FF + (c + 1) * fc])
        cw = cw_ref[:, c * fc:(c + 1) * fc]
        conv = (cw[0:1] * pltpu.roll(a, 2, 0) + cw[1:2] * pltpu.roll(a, 1, 0) + cw[2:3] * a
                + cb_ref[:, c * fc:(c + 1) * fc])[HALO:]
        inner = 0.7978845608028654 * (conv + 0.044715 * (conv * conv * conv))
        act = conv * (0.5 * (1.0 + jnp.tanh(inner)))
        part = _dot((act * v).astype(BF16), wdn_ref[c * fc:(c + 1) * fc, :])
        acc = part if acc is None else acc + part

    h2 = h + acc
    var = jnp.mean(h2 * h2, axis=-1, keepdims=True)
    out_ref[...] = h2 * lax.rsqrt(var + EPS) * gf_ref[...]


def _ffn(h_mid, ffn_norm, wup, conv_w, conv_b, wdn, final_norm, rows_x, seq, tm, n_chunks):
    assert seq % tm == 0 and rows_x % seq == 0
    tiles_per_seq = seq // tm
    meta_tail = (rows_x + BLK) // HALO - 1
    const = lambda i: (0, 0)
    row = lambda i: (i, 0)
    halo = lambda i: (jnp.where(lax.rem(i, tiles_per_seq) == 0, meta_tail,
                                i * (tm // HALO) - 1), 0)
    return pl.pallas_call(
        functools.partial(_ffn_kernel, tm, n_chunks),
        grid=(rows_x // tm,),
        in_specs=[
            pl.BlockSpec((tm, D_MODEL), row),
            pl.BlockSpec((HALO, D_MODEL), halo),
            pl.BlockSpec((1, D_MODEL), const),
            pl.BlockSpec((D_MODEL, 2 * D_FF), const, pipeline_mode=pl.Buffered(1)),
            pl.BlockSpec((8, D_FF), const),
            pl.BlockSpec((1, D_FF), const),
            pl.BlockSpec((D_FF, D_MODEL), const, pipeline_mode=pl.Buffered(1)),
            pl.BlockSpec((1, D_MODEL), const),
        ],
        out_specs=pl.BlockSpec((tm, D_MODEL), row),
        out_shape=jax.ShapeDtypeStruct((rows_x, D_MODEL), F32),
        scratch_shapes=[pltpu.VMEM((HALO + tm, D_MODEL), BF16)],
        compiler_params=_params(),
        name="conv_ffn",
    )(h_mid, h_mid, ffn_norm, wup, conv_w, conv_b, wdn, final_norm)


def _rope_tables(nb):
    half = HEAD_DIM // 2
    inv_freq = ROPE_THETA ** (-jnp.arange(half, dtype=F32) / half)
    pos = (jnp.arange((nb + 1) * BLK) - META_PAD).astype(F32)
    ang = pos[:, None] * inv_freq[None, :]
    cos2 = jnp.tile(jnp.cos(ang), (1, LANES // half))
    sin = jnp.sin(ang)
    sin2 = jnp.tile(jnp.concatenate([-sin, sin], axis=-1), (1, LANES // HEAD_DIM))
    return cos2, sin2


def kernel(x, meta_tokens, mix_norm, w_in, b_in, w_alpha, b_alpha, attn_sinks, gla_head_norm,
           w_proj_attn, w_proj_gla, w_out, ffn_norm, w_up, conv_w, conv_b, w_down, final_norm):
    batch, seq, _ = x.shape
    rows_x = batch * seq
    nb = seq // BLK
    l = 0

    x_rows = x.reshape(rows_x, D_MODEL)
    meta_block = jnp.concatenate(
        [jnp.zeros((META_PAD, D_MODEL), x.dtype), meta_tokens.astype(x.dtype)], axis=0)

    lo0 = ATT_Q + 2 * ATT_KV + 2 * GLA_DK + 2 * GLA_DV
    wi, bi = w_in[l], b_in[l]
    pad_w = jnp.zeros((D_MODEL, C_LOW - GLA_RANK), wi.dtype)
    w_cat = jnp.concatenate(
        [wi[:, :lo0], wi[:, lo0 + GLA_RANK:], wi[:, lo0:lo0 + GLA_RANK], pad_w], axis=1).astype(BF16)
    b_cat = jnp.concatenate(
        [bi[:lo0], bi[lo0 + GLA_RANK:], bi[lo0:lo0 + GLA_RANK],
         jnp.zeros((C_LOW - GLA_RANK,), bi.dtype)])[None, :]
    wal = jnp.concatenate(
        [w_alpha[l], jnp.zeros((C_LOW - GLA_RANK, GLA_DK), w_alpha.dtype)], axis=0)
    wal_hi = wal.astype(BF16)
    wal_lo = (wal - wal_hi.astype(F32)).astype(BF16)

    cos2, sin2 = _rope_tables(nb)
    h_mid = _mixer(x_rows, meta_block, mix_norm[l][None, :], w_cat, b_cat, wal_hi, wal_lo,
                   b_alpha[l][None, :], attn_sinks[l], cos2, sin2, gla_head_norm[l][None, :],
                   w_proj_attn[l].astype(BF16), w_proj_gla[l].astype(BF16), w_out[l].astype(BF16), nb)

    cw = jnp.concatenate([conv_w[l], jnp.zeros((8 - conv_w.shape[1], D_FF), conv_w.dtype)], axis=0)
    out = _ffn(h_mid, ffn_norm[l][None, :], w_up[l].astype(BF16), cw, conv_b[l][None, :],
               w_down[l].astype(BF16), final_norm[None, :], rows_x, seq, tm=1024, n_chunks=1)
    return out.reshape(batch, seq, D_MODEL)
```

```python
import functools

import numpy as np
import jax
import jax.numpy as jnp
from jax import lax
from jax.experimental import pallas as pl
from jax.experimental.pallas import tpu as pltpu

D_MODEL = 1024
N_META = 16
EPS = 1e-6
ATT_HEADS = 8
ATT_KV_HEADS = 2
HEAD_DIM = 64
BLK = 128
ROPE_THETA = 10000.0
GLA_HEADS = 4
GLA_DK = 256
GLA_DV = 512
GLA_DK_HEAD = GLA_DK // GLA_HEADS
GLA_DV_HEAD = GLA_DV // GLA_HEADS
GLA_RANK = 16
GLA_TAU = 16.0
D_FF = 2816
ATT_Q = ATT_HEADS * HEAD_DIM
ATT_KV = ATT_KV_HEADS * HEAD_DIM

META_PAD = BLK - N_META
LANES = 128
HALO = 16
NEG = -1e30
VMEM_LIMIT = 56 * 1024 * 1024

C_QKV = ATT_Q + 2 * ATT_KV
C_GLA = 2 * GLA_DK + GLA_DV
C_GR = GLA_DV
C_GATE = 2 * D_MODEL
C_LOW = LANES
O_QKV = 0
O_GLA = O_QKV + C_QKV
O_GR = O_GLA + C_GLA
O_GATE = O_GR + C_GR
O_LOW = O_GATE + C_GATE
C_ALL = O_LOW + C_LOW
PROJ_CHUNK = 256

GLA_LEVELS = (64, 32, 16, 8, 4, 2, 1)
N_LEVELS = len(GLA_LEVELS)
GLA_FINE = tuple(h for h in GLA_LEVELS if h < 8)

N_GROUPS = 2

BF16 = jnp.bfloat16
F32 = jnp.float32


def _sigmoid(x):
    return 1.0 / (1.0 + jnp.exp(-x))


def _dot(a, b):
    return jnp.dot(a, b, preferred_element_type=F32)


def _dot_nt(a, b):
    return lax.dot_general(a, b, (((1,), (1,)), ((), ())), preferred_element_type=F32)


def _split3(x):
    hi = x.astype(BF16)
    r1 = x - hi.astype(F32)
    mid = r1.astype(BF16)
    lo = (r1 - mid.astype(F32)).astype(BF16)
    return hi, mid, lo


def _params():
    return pltpu.CompilerParams(dimension_semantics=("arbitrary",), vmem_limit_bytes=VMEM_LIMIT)


def _inproj_stages(x, weights, store):
    g_ref, w_ref, b_ref, wal_hi_ref, wal_lo_ref, bal_ref = weights
    var = jnp.mean(x * x, axis=-1, keepdims=True)
    u = (x * lax.rsqrt(var + EPS) * g_ref[...]).astype(BF16)
    yield

    def proj(off, width):
        return _dot(u, w_ref[:, off:off + width]) + b_ref[:, off:off + width]

    low = proj(O_LOW, C_LOW)
    yield
    silu = lambda y: y * _sigmoid(y)
    plan = ((0, O_QKV, C_QKV, None), (1, O_GLA, C_GLA, None), (2, O_GR, C_GR, silu),
            (3, O_GATE, C_GATE, _sigmoid))
    low_pending = True
    for i, off, width, act in plan:
        for c in range(0, width, PROJ_CHUNK):
            y = proj(off + c, PROJ_CHUNK)
            store(i, c, c + PROJ_CHUNK, (y if act is None else act(y)).astype(BF16))
            yield
            if low_pending:
                low_pending = False
                low_hi = low.astype(BF16)
                low_lo = (low - low_hi.astype(F32)).astype(BF16)
                z = (_dot(low_hi, wal_hi_ref[...]) + _dot(low_lo, wal_hi_ref[...])
                     + _dot(low_hi, wal_lo_ref[...]) + bal_ref[...])
                log_sig = jnp.minimum(z, 0.0) - jnp.log1p(jnp.exp(-jnp.abs(z)))
                store(4, 0, GLA_DK, log_sig * (1.0 / GLA_TAU))
                yield


def _attn_stages(result, is_meta, first_x, sinks_ref, qkv_ref, cos_ref, sin_ref,
                 kprev, vprev, kmeta, vmeta):
    cos = cos_ref[...]
    sin = sin_ref[...]
    lane = lax.broadcasted_iota(jnp.int32, (BLK, LANES), 1)
    first_half = (lane & (HEAD_DIM - 1)) < HEAD_DIM // 2

    def rope(xc):
        rot = jnp.where(first_half, pltpu.roll(xc, LANES - HEAD_DIM // 2, 1),
                        pltpu.roll(xc, HEAD_DIM // 2, 1))
        return xc * cos + rot * sin

    k_cur = rope(qkv_ref[:, ATT_Q:ATT_Q + ATT_KV].astype(F32)).astype(BF16)
    v_cur = qkv_ref[:, ATT_Q + ATT_KV:ATT_Q + 2 * ATT_KV]

    q_rows = []
    for hg in range(ATT_HEADS):
        chunk, off, grp = hg // 2, hg % 2, hg // (ATT_HEADS // ATT_KV_HEADS)
        qc = rope(qkv_ref[:, chunk * LANES:(chunk + 1) * LANES].astype(F32))
        if off != grp:
            qc = pltpu.roll(qc, HEAD_DIM, 1)
        on_group = (lane >= grp * HEAD_DIM) & (lane < (grp + 1) * HEAD_DIM)
        q_rows.append(jnp.where(on_group, qc * (HEAD_DIM ** -0.5), 0.0).astype(BF16))
    q_all = jnp.concatenate(q_rows, axis=0)

    k_prev = jnp.where(first_x, kmeta[...], kprev[...])
    v_prev = jnp.where(first_x, vmeta[...], vprev[...])
    kk = jnp.concatenate([k_prev, k_cur], axis=0)
    vv = jnp.concatenate([v_prev, v_cur], axis=0)
    kprev[...] = k_cur
    vprev[...] = v_cur
    kmeta[...] = jnp.where(is_meta, k_cur, kmeta[...])
    vmeta[...] = jnp.where(is_meta, v_cur, vmeta[...])
    yield

    sc = _dot_nt(q_all, kk).reshape(ATT_HEADS, BLK, 2 * BLK)
    yield

    rq = lax.broadcasted_iota(jnp.int32, (BLK, BLK), 0)
    ck = lax.broadcasted_iota(jnp.int32, (BLK, BLK), 1)
    from_cur = ck <= rq
    cur_min = jnp.where(is_meta, META_PAD, 0)
    prev_min = jnp.where(is_meta, BLK, jnp.where(first_x, META_PAD, 0))
    valid = ck >= jnp.where(from_cur, cur_min, prev_min)
    sc = jnp.where(from_cur[None], sc[:, :, BLK:], sc[:, :, :BLK])
    sc = jnp.where(valid[None], sc, NEG)

    hid = lax.broadcasted_iota(jnp.int32, (ATT_HEADS, 1, 1), 0)
    sink = jnp.zeros((ATT_HEADS, 1, 1), F32)
    for hg in range(ATT_HEADS):
        sink = jnp.where(hid == hg, sinks_ref[hg], sink)

    m = jnp.maximum(jnp.max(sc, axis=-1, keepdims=True), sink)
    p = jnp.exp(sc - m)
    denom = jnp.sum(p, axis=-1, keepdims=True) + jnp.exp(sink - m)
    p = jnp.concatenate([jnp.where(from_cur[None], 0.0, p), jnp.where(from_cur[None], p, 0.0)],
                        axis=-1)
    p = p.astype(BF16).reshape(ATT_HEADS * BLK, 2 * BLK)
    yield

    o = _dot(p, vv)
    yield

    o = o.reshape(ATT_HEADS, BLK, LANES) * (1.0 / denom)
    chunks = []
    for chunk in range(ATT_HEADS // 2):
        grp = (2 * chunk) // (ATT_HEADS // ATT_KV_HEADS)
        o0, o1 = o[2 * chunk], o[2 * chunk + 1]
        if grp != 0:
            o0 = pltpu.roll(o0, HEAD_DIM, 1)
        if grp != 1:
            o1 = pltpu.roll(o1, HEAD_DIM, 1)
        chunks.append(jnp.where(lane < HEAD_DIM, o0, o1).astype(BF16))
    result.append(jnp.concatenate(chunks, axis=1))


def _gla_tables():
    t = np.arange(BLK)
    tri = (t[:, None] >= t[None, :]).astype(np.float32)
    sel = np.zeros((len(GLA_FINE) * BLK, BLK), np.float32)
    for li, half in enumerate(GLA_FINE):
        boundary = (t & ~(2 * half - 1)) + half - 1
        sel[li * BLK + t, boundary] = 1.0
    x = t[:, None] ^ t[None, :]
    top = np.floor(np.log2(np.maximum(x, 1))).astype(np.int64)
    lvl = np.array([GLA_LEVELS.index(1 << int(v)) for v in top.ravel()]).reshape(BLK, BLK)
    lvl = np.where(t[:, None] == t[None, :], N_LEVELS, lvl)
    lvl = np.where(t[:, None] < t[None, :], N_LEVELS + 1, lvl)
    lvl = np.tile(lvl, (1, GLA_HEADS))
    return (jnp.asarray(tri, BF16), jnp.asarray(sel, BF16), jnp.asarray(lvl, jnp.int32))


def _gla_stages(result, is_meta, first_x, gla_ref, loga_ref, gr_ref, tri_ref, sel_ref, lvl_ref,
                hn_ref, state, state_meta):
    row = lax.broadcasted_iota(jnp.int32, (BLK, GLA_DK), 0)
    lane = lax.broadcasted_iota(jnp.int32, (BLK, GLA_DK), 1)
    is_token = row >= jnp.where(is_meta, META_PAD, 0)

    g3 = jnp.concatenate(_split3(loga_ref[...]), axis=1)
    yield
    b3 = _dot(tri_ref[...], g3)
    yield
    b = b3[:, :GLA_DK] + b3[:, GLA_DK:2 * GLA_DK] + b3[:, 2 * GLA_DK:]
    b_last = b[BLK - 1:BLK, :]

    b_hi = b.astype(BF16)
    b_mid = (b - b_hi.astype(F32)).astype(BF16)
    fine = _dot(sel_ref[...], jnp.concatenate([b_hi, b_mid], axis=1))

    qf = gla_ref[:, :GLA_DK].astype(F32) * (GLA_DK_HEAD ** -0.5)
    kf = jnp.where(is_token, gla_ref[:, GLA_DK:2 * GLA_DK].astype(F32), 0.0)

    head_mask = [jnp.where((lane >= h * GLA_DK_HEAD) & (lane < (h + 1) * GLA_DK_HEAD), 1.0, 0.0)
                 .astype(BF16) for h in range(GLA_HEADS)]

    def stack_heads(x):
        xb = x.astype(BF16)
        return jnp.concatenate([xb * hm for hm in head_mask], axis=0)

    state_in = jnp.where(first_x, state_meta[...], state[...])
    q_in = stack_heads(qf * jnp.exp(b))
    q_diag, k_diag = qf.astype(BF16), stack_heads(kf)
    yield
    o_inter = _dot(q_in, state_in.astype(BF16))
    lvl = lvl_ref[...]
    a = jnp.where(lvl == N_LEVELS, _dot_nt(q_diag, k_diag), 0.0)
    zeros = lambda shape: jnp.zeros(shape, F32)
    for li, half in enumerate(GLA_LEVELS):
        if half in GLA_FINE:
            fi = GLA_FINE.index(half)
            r = fine[fi * BLK:(fi + 1) * BLK, :GLA_DK] + fine[fi * BLK:(fi + 1) * BLK, GLA_DK:]
            upper = (row & half) != 0
            t_l = jnp.where(upper, qf, kf) * jnp.exp(-jnp.abs(b - r))
            q_l = jnp.where(upper, t_l, 0.0)
            k_l = jnp.where(upper, 0.0, t_l)
        else:
            grp = (BLK // (2 * half), 2 * half, GLA_DK)
            b_g, q_g, k_g = b.reshape(grp), qf.reshape(grp), kf.reshape(grp)
            r = b_g[:, half - 1:half, :]
            q_up = q_g[:, half:, :] * jnp.exp(b_g[:, half:, :] - r)
            k_lo = k_g[:, :half, :] * jnp.exp(r - b_g[:, :half, :])
            q_l = jnp.concatenate([zeros(q_up.shape), q_up], axis=1).reshape(BLK, GLA_DK)
            k_l = jnp.concatenate([k_lo, zeros(k_lo.shape)], axis=1).reshape(BLK, GLA_DK)
        q_l, k_l = q_l.astype(BF16), stack_heads(k_l)
        yield
        a = jnp.where(lvl == li, _dot_nt(q_l, k_l), a)
    a = a.astype(BF16)

    k_out_t = (kf * jnp.exp(b_last - b)).T.astype(BF16)
    decay_t = jnp.broadcast_to(jnp.exp(b_last), (BLK, GLA_DK)).T
    yield

    hn = hn_ref[...]
    new_state, heads = [], []
    for h in range(GLA_HEADS):
        v_h = gla_ref[:, 2 * GLA_DK + h * GLA_DV_HEAD:2 * GLA_DK + (h + 1) * GLA_DV_HEAD]
        o_h = o_inter[h * BLK:(h + 1) * BLK, :] + _dot(a[:, h * BLK:(h + 1) * BLK], v_h)
        new_state.append(_dot(k_out_t[h * GLA_DK_HEAD:(h + 1) * GLA_DK_HEAD, :], v_h))
        heads.append(o_h)
    yield
    for h in range(GLA_HEADS):
        o_h = heads[h]
        var = jnp.mean(o_h * o_h, axis=-1, keepdims=True)
        o_h = o_h * lax.rsqrt(var + EPS) * hn
        gate = gr_ref[:, h * GLA_DV_HEAD:(h + 1) * GLA_DV_HEAD].astype(F32)
        heads[h] = (o_h * gate).astype(BF16)
    state_new = state_in * decay_t + jnp.concatenate(new_state, axis=0)
    state[...] = state_new
    state_meta[...] = jnp.where(is_meta, state_new, state_meta[...])
    result.append(jnp.concatenate(heads, axis=1))


def _merge_stages(result, y_att, y_gla, gate, h, wpa_ref, wpg_ref, wo_ref):
    pa = _dot(y_att, wpa_ref[...])
    pg = _dot(y_gla, wpg_ref[...])
    yield
    mixed = (gate[:, :D_MODEL].astype(F32) * pa + gate[:, D_MODEL:].astype(F32) * pg).astype(BF16)
    yield
    result.append(h + _dot(mixed, wo_ref[...]))


def _advance(gens):
    alive = []
    for gen in gens:
        try:
            next(gen)
            alive.append(gen)
        except StopIteration:
            pass
    return alive


def _mixer_kernel(nb, sinks_ref, xnext_ref, meta_ref, g_ref, w_ref, b_ref, wal_hi_ref, wal_lo_ref,
                  bal_ref, cos_ref, sin_ref, tri_ref, sel_ref, lvl_ref, hn_ref, wpa_ref, wpg_ref,
                  wo_ref, out_ref, outm_ref,
                  cur_h, cur_qkv, cur_gla, cur_gr, cur_gate, cur_loga,
                  nxt_qkv, nxt_gla, nxt_gr, nxt_gate, nxt_loga,
                  kprev, vprev, kmeta, vmeta, state, state_meta):
    s = pl.program_id(0)
    is_meta = s == 0
    first_x = jnp.logical_and(s >= 1, lax.rem(jnp.maximum(s - 1, 0), nb) == 0)
    cur = (cur_qkv, cur_gla, cur_gr, cur_gate, cur_loga)
    nxt = (nxt_qkv, nxt_gla, nxt_gr, nxt_gate, nxt_loga)
    weights = (g_ref, w_ref, b_ref, wal_hi_ref, wal_lo_ref, bal_ref)
    groups = range(N_GROUPS)

    @pl.when(is_meta)
    def _():
        for ref in (kprev, vprev, kmeta, vmeta, state, state_meta):
            ref[...] = jnp.zeros_like(ref)

        def store_all_groups(i, c0, c1, val):
            for g in groups:
                cur[i][g, :, c0:c1] = val

        for g in groups:
            cur_h[g] = meta_ref[...]
        meta_proj = [_inproj_stages(meta_ref[...], weights, store_all_groups)]
        while meta_proj:
            meta_proj = _advance(meta_proj)

    def store_next(i, c0, c1, val):
        nxt[i][:, :, c0:c1] = val.reshape(N_GROUPS, BLK, c1 - c0)

    x_next = xnext_ref[...]
    inproj = [_inproj_stages(x_next.reshape(N_GROUPS * BLK, D_MODEL), weights, store_next)]

    y_att, y_gla, h_mid = [], [], []
    mixers = []
    for g in groups:
        mixers.append(_attn_stages(y_att, is_meta, first_x, sinks_ref, cur_qkv.at[g], cos_ref,
                                   sin_ref, kprev.at[g], vprev.at[g], kmeta.at[g], vmeta.at[g]))
        mixers.append(_gla_stages(y_gla, is_meta, first_x, cur_gla.at[g], cur_loga.at[g],
                                  cur_gr.at[g], tri_ref, sel_ref, lvl_ref, hn_ref,
                                  state.at[g], state_meta.at[g]))
    while mixers:
        inproj = _advance(inproj)
        mixers = _advance(mixers)

    gate = cur_gate[...].reshape(N_GROUPS * BLK, C_GATE)
    h = cur_h[...].reshape(N_GROUPS * BLK, D_MODEL)
    merge = [_merge_stages(h_mid, jnp.concatenate(y_att, axis=0), jnp.concatenate(y_gla, axis=0),
                           gate, h, wpa_ref, wpg_ref, wo_ref)]
    while merge or inproj:
        merge = _advance(merge)
        inproj = _advance(inproj)

    h_mid = h_mid[0].reshape(N_GROUPS, BLK, D_MODEL)
    out_ref[...] = h_mid

    cur_h[...] = x_next
    for c_ref, n_ref in zip(cur, nxt):
        c_ref[...] = n_ref[...]

    @pl.when(is_meta)
    def _():
        outm_ref[...] = out_ref[0]


def _mixer(x_groups, meta_block, mix_norm, w_cat, b_cat, wal_hi, wal_lo, b_alpha, sinks, cos2, sin2,
           head_norm, wpa, wpg, wo, nb):
    rows_g = x_groups.shape[1]
    nblocks_g = rows_g // BLK
    tri, sel, lvl = _gla_tables()
    const = lambda s: (0, 0)
    xnext = lambda s: (0, jnp.minimum(s, nblocks_g - 1), 0)
    xout = lambda s: (0, jnp.maximum(s - 1, 0), 0)
    pos = lambda s: (jnp.where(s == 0, 0, lax.rem(jnp.maximum(s - 1, 0), nb) + 1), 0)
    grouped = lambda cols, dt: pltpu.VMEM((N_GROUPS, BLK, cols), dt)
    proj_bufs = [grouped(C_QKV, BF16), grouped(C_GLA, BF16), grouped(C_GR, BF16),
                 grouped(C_GATE, BF16), grouped(GLA_DK, F32)]
    return pl.pallas_call(
        functools.partial(_mixer_kernel, nb),
        grid=(nblocks_g + 1,),
        in_specs=[
            pl.BlockSpec(memory_space=pltpu.SMEM),
            pl.BlockSpec((N_GROUPS, BLK, D_MODEL), xnext),
            pl.BlockSpec((BLK, D_MODEL), const),
            pl.BlockSpec((1, D_MODEL), const),
            pl.BlockSpec((D_MODEL, C_ALL), const),
            pl.BlockSpec((1, C_ALL), const),
            pl.BlockSpec((LANES, GLA_DK), const),
            pl.BlockSpec((LANES, GLA_DK), const),
            pl.BlockSpec((1, GLA_DK), const),
            pl.BlockSpec((BLK, LANES), pos),
            pl.BlockSpec((BLK, LANES), pos),
            pl.BlockSpec((BLK, BLK), const),
            pl.BlockSpec((len(GLA_FINE) * BLK, BLK), const),
            pl.BlockSpec((BLK, GLA_HEADS * BLK), const),
            pl.BlockSpec((1, GLA_DV_HEAD), const),
            pl.BlockSpec((ATT_Q, D_MODEL), const),
            pl.BlockSpec((GLA_DV, D_MODEL), const),
            pl.BlockSpec((D_MODEL, D_MODEL), const),
        ],
        out_specs=[pl.BlockSpec((N_GROUPS, BLK, D_MODEL), xout),
                   pl.BlockSpec((BLK, D_MODEL), const)],
        out_shape=[jax.ShapeDtypeStruct((N_GROUPS, rows_g, D_MODEL), F32),
                   jax.ShapeDtypeStruct((BLK, D_MODEL), F32)],
        scratch_shapes=[grouped(D_MODEL, F32)] + proj_bufs + proj_bufs
        + [grouped(ATT_KV, BF16)] * 4
        + [pltpu.VMEM((N_GROUPS, GLA_DK, GLA_DV_HEAD), F32)] * 2,
        compiler_params=_params(),
        name="mixer",
    )(sinks, x_groups, meta_block, mix_norm, w_cat, b_cat, wal_hi, wal_lo, b_alpha, cos2, sin2,
      tri, sel, lvl, head_norm, wpa, wpg, wo)


def _ffn_kernel(tm, n_chunks, tiles_per_seq, h_ref, halo_ref, halo_meta_ref, g_ref, wup_ref, cw_ref,
                cb_ref, wdn_ref, gf_ref, out_ref, u_scr):
    def normed(x):
        var = jnp.mean(x * x, axis=-1, keepdims=True)
        return (x * lax.rsqrt(var + EPS) * g_ref[...]).astype(BF16)

    h = h_ref[...]
    seq_start = lax.rem(pl.program_id(0), tiles_per_seq) == 0
    u_scr[0:HALO, :] = normed(jnp.where(seq_start, halo_meta_ref[...], halo_ref[...]))
    u_scr[HALO:, :] = normed(h)
    u = u_scr[...]

    fc = D_FF // n_chunks
    acc = None
    for c in range(n_chunks):
        a = _dot(u, wup_ref[:, c * fc:(c + 1) * fc])
        v = _dot(u[HALO:], wup_ref[:, D_FF + c * fc:D_FF + (c + 1) * fc])
        cw = cw_ref[:, c * fc:(c + 1) * fc]
        conv = (cw[0:1] * pltpu.roll(a, 2, 0) + cw[1:2] * pltpu.roll(a, 1, 0) + cw[2:3] * a
                + cb_ref[:, c * fc:(c + 1) * fc])[HALO:]
        inner = 0.7978845608028654 * (conv + 0.044715 * (conv * conv * conv))
        act = conv * (0.5 * (1.0 + jnp.tanh(inner)))
        part = _dot((act * v).astype(BF16), wdn_ref[c * fc:(c + 1) * fc, :])
        acc = part if acc is None else acc + part

    h2 = h + acc
    var = jnp.mean(h2 * h2, axis=-1, keepdims=True)
    out_ref[...] = h2 * lax.rsqrt(var + EPS) * gf_ref[...]


def _ffn(h_mid, h_meta, ffn_norm, wup, conv_w, conv_b, wdn, final_norm, seq, tm, n_chunks):
    rows_x = h_mid.shape[0]
    assert seq % tm == 0 and rows_x % seq == 0
    tiles_per_seq = seq // tm
    const = lambda i: (0, 0)
    row = lambda i: (i, 0)
    return pl.pallas_call(
        functools.partial(_ffn_kernel, tm, n_chunks, tiles_per_seq),
        grid=(rows_x // tm,),
        in_specs=[
            pl.BlockSpec((tm, D_MODEL), row),
            pl.BlockSpec((HALO, D_MODEL), lambda i: (jnp.maximum(i * (tm // HALO) - 1, 0), 0)),
            pl.BlockSpec((HALO, D_MODEL), lambda i: (BLK // HALO - 1, 0)),
            pl.BlockSpec((1, D_MODEL), const),
            pl.BlockSpec((D_MODEL, 2 * D_FF), const, pipeline_mode=pl.Buffered(1)),
            pl.BlockSpec((8, D_FF), const),
            pl.BlockSpec((1, D_FF), const),
            pl.BlockSpec((D_FF, D_MODEL), const, pipeline_mode=pl.Buffered(1)),
            pl.BlockSpec((1, D_MODEL), const),
        ],
        out_specs=pl.BlockSpec((tm, D_MODEL), row),
        out_shape=jax.ShapeDtypeStruct((rows_x, D_MODEL), F32),
        scratch_shapes=[pltpu.VMEM((HALO + tm, D_MODEL), BF16)],
        compiler_params=_params(),
        name="conv_ffn",
    )(h_mid, h_mid, h_meta, ffn_norm, wup, conv_w, conv_b, wdn, final_norm)


def _rope_tables(nb):
    half = HEAD_DIM // 2
    inv_freq = ROPE_THETA ** (-jnp.arange(half, dtype=F32) / half)
    pos = (jnp.arange((nb + 1) * BLK) - META_PAD).astype(F32)
    ang = pos[:, None] * inv_freq[None, :]
    cos2 = jnp.tile(jnp.cos(ang), (1, LANES // half))
    sin = jnp.sin(ang)
    sin2 = jnp.tile(jnp.concatenate([-sin, sin], axis=-1), (1, LANES // HEAD_DIM))
    return cos2, sin2


def kernel(x, meta_tokens, mix_norm, w_in, b_in, w_alpha, b_alpha, attn_sinks, gla_head_norm,
           w_proj_attn, w_proj_gla, w_out, ffn_norm, w_up, conv_w, conv_b, w_down, final_norm):
    batch, seq, _ = x.shape
    assert batch % N_GROUPS == 0 and seq % BLK == 0
    rows_x = batch * seq
    nb = seq // BLK
    l = 0

    meta_block = jnp.concatenate(
        [jnp.zeros((META_PAD, D_MODEL), x.dtype), meta_tokens.astype(x.dtype)], axis=0)

    lo0 = ATT_Q + 2 * ATT_KV + 2 * GLA_DK + 2 * GLA_DV
    wi, bi = w_in[l], b_in[l]
    pad_w = jnp.zeros((D_MODEL, C_LOW - GLA_RANK), wi.dtype)
    w_cat = jnp.concatenate(
        [wi[:, :lo0], wi[:, lo0 + GLA_RANK:], wi[:, lo0:lo0 + GLA_RANK], pad_w], axis=1).astype(BF16)
    b_cat = jnp.concatenate(
        [bi[:lo0], bi[lo0 + GLA_RANK:], bi[lo0:lo0 + GLA_RANK],
         jnp.zeros((C_LOW - GLA_RANK,), bi.dtype)])[None, :]
    wal = jnp.concatenate(
        [w_alpha[l], jnp.zeros((C_LOW - GLA_RANK, GLA_DK), w_alpha.dtype)], axis=0)
    wal_hi = wal.astype(BF16)
    wal_lo = (wal - wal_hi.astype(F32)).astype(BF16)

    cos2, sin2 = _rope_tables(nb)
    x_groups = x.reshape(N_GROUPS, rows_x // N_GROUPS, D_MODEL)
    h_mid, h_meta = _mixer(x_groups, meta_block, mix_norm[l][None, :], w_cat, b_cat, wal_hi, wal_lo,
                           b_alpha[l][None, :], attn_sinks[l], cos2, sin2,
                           gla_head_norm[l][None, :], w_proj_attn[l].astype(BF16),
                           w_proj_gla[l].astype(BF16), w_out[l].astype(BF16), nb)

    cw = jnp.concatenate([conv_w[l], jnp.zeros((8 - conv_w.shape[1], D_FF), conv_w.dtype)], axis=0)
    out = _ffn(h_mid.reshape(rows_x, D_MODEL), h_meta, ffn_norm[l][None, :], w_up[l].astype(BF16),
               cw, conv_b[l][None, :], w_down[l].astype(BF16), final_norm[None, :], seq,
               tm=1024, n_chunks=1)
    return out.reshape(batch, seq, D_MODEL)
```

```python
import functools

import numpy as np
import jax
import jax.numpy as jnp
from jax import lax
from jax.experimental import pallas as pl
from jax.experimental.pallas import tpu as pltpu

D_MODEL = 1024
N_META = 16
EPS = 1e-6
ATT_HEADS = 8
ATT_KV_HEADS = 2
HEAD_DIM = 64
BLK = 128
ROPE_THETA = 10000.0
GLA_HEADS = 4
GLA_DK = 256
GLA_DV = 512
GLA_DK_HEAD = GLA_DK // GLA_HEADS
GLA_DV_HEAD = GLA_DV // GLA_HEADS
GLA_RANK = 16
GLA_TAU = 16.0
D_FF = 2816
ATT_Q = ATT_HEADS * HEAD_DIM
ATT_KV = ATT_KV_HEADS * HEAD_DIM

META_PAD = BLK - N_META
LANES = 128
HALO = 16
NEG = -1e30
VMEM_LIMIT = 56 * 1024 * 1024

C_QKV = ATT_Q + 2 * ATT_KV
C_GLA = 2 * GLA_DK + GLA_DV
C_GR = GLA_DV
C_GATE = 2 * D_MODEL
C_LOW = LANES
O_QKV = 0
O_GLA = O_QKV + C_QKV
O_GR = O_GLA + C_GLA
O_GATE = O_GR + C_GR
O_LOW = O_GATE + C_GATE
C_ALL = O_LOW + C_LOW
PROJ_CHUNK = 256

GLA_LEVELS = (64, 32, 16, 8, 4, 2, 1)
N_LEVELS = len(GLA_LEVELS)
GLA_FINE = tuple(h for h in GLA_LEVELS if h < 8)

N_GROUPS = 2

BF16 = jnp.bfloat16
F32 = jnp.float32


def _sigmoid(x):
    return 1.0 / (1.0 + jnp.exp(-x))


def _dot(a, b):
    return jnp.dot(a, b, preferred_element_type=F32)


def _dot_nt(a, b):
    return lax.dot_general(a, b, (((1,), (1,)), ((), ())), preferred_element_type=F32)


def _split3(x):
    hi = x.astype(BF16)
    r1 = x - hi.astype(F32)
    mid = r1.astype(BF16)
    lo = (r1 - mid.astype(F32)).astype(BF16)
    return hi, mid, lo


def _params():
    return pltpu.CompilerParams(dimension_semantics=("arbitrary",), vmem_limit_bytes=VMEM_LIMIT)


LOW0 = ATT_Q + 2 * ATT_KV + 2 * GLA_DK + 2 * GLA_DV
D_IN = LOW0 + GLA_RANK + C_GATE


def _prep_w_kernel(w_ref, out_ref):
    out_ref[:, :LOW0] = w_ref[:, :LOW0].astype(BF16)
    tail = w_ref[:, LOW0:]
    out_ref[:, O_GATE:O_GATE + C_GATE] = tail[:, GLA_RANK:].astype(BF16)
    out_ref[:, O_LOW:O_LOW + GLA_RANK] = tail[:, :GLA_RANK].astype(BF16)
    out_ref[:, O_LOW + GLA_RANK:] = jnp.zeros((w_ref.shape[0], C_LOW - GLA_RANK), BF16)


def _prep_w(w, tr):
    rows = w.shape[0]
    return pl.pallas_call(
        _prep_w_kernel,
        grid=(rows // tr,),
        in_specs=[pl.BlockSpec((tr, D_IN), lambda i: (i, 0))],
        out_specs=pl.BlockSpec((tr, C_ALL), lambda i: (i, 0)),
        out_shape=jax.ShapeDtypeStruct((rows, C_ALL), BF16),
        compiler_params=_params(),
        name="prep_w_in",
    )(w)


def _inproj_stages(x, weights, store):
    g_ref, w_ref, b_ref, wal_hi_ref, wal_lo_ref, bal_ref = weights
    var = jnp.mean(x * x, axis=-1, keepdims=True)
    u = (x * lax.rsqrt(var + EPS) * g_ref[...]).astype(BF16)
    yield

    def proj(off, width):
        return _dot(u, w_ref[:, off:off + width]) + b_ref[:, off:off + width]

    low = proj(O_LOW, C_LOW)
    yield
    silu = lambda y: y * _sigmoid(y)
    plan = ((0, O_QKV, C_QKV, None), (1, O_GLA, C_GLA, None), (2, O_GR, C_GR, silu),
            (3, O_GATE, C_GATE, _sigmoid))
    low_pending = True
    for i, off, width, act in plan:
        for c in range(0, width, PROJ_CHUNK):
            y = proj(off + c, PROJ_CHUNK)
            store(i, c, c + PROJ_CHUNK, (y if act is None else act(y)).astype(BF16))
            yield
            if low_pending:
                low_pending = False
                low_hi = low.astype(BF16)
                low_lo = (low - low_hi.astype(F32)).astype(BF16)
                z = (_dot(low_hi, wal_hi_ref[...]) + _dot(low_lo, wal_hi_ref[...])
                     + _dot(low_hi, wal_lo_ref[...]) + bal_ref[...])
                log_sig = jnp.minimum(z, 0.0) - jnp.log1p(jnp.exp(-jnp.abs(z)))
                store(4, 0, GLA_DK, log_sig * (1.0 / GLA_TAU))
                yield


def _attn_stages(result, is_meta, first_x, sinks_ref, qkv_ref, cos_ref, sin_ref,
                 kprev, vprev, kmeta, vmeta):
    cos = cos_ref[...]
    sin = sin_ref[...]
    lane = lax.broadcasted_iota(jnp.int32, (BLK, LANES), 1)
    first_half = (lane & (HEAD_DIM - 1)) < HEAD_DIM // 2

    def rope(xc):
        rot = jnp.where(first_half, pltpu.roll(xc, LANES - HEAD_DIM // 2, 1),
                        pltpu.roll(xc, HEAD_DIM // 2, 1))
        return xc * cos + rot * sin

    k_cur = rope(qkv_ref[:, ATT_Q:ATT_Q + ATT_KV].astype(F32)).astype(BF16)
    v_cur = qkv_ref[:, ATT_Q + ATT_KV:ATT_Q + 2 * ATT_KV]

    q_rows = []
    for hg in range(ATT_HEADS):
        chunk, off, grp = hg // 2, hg % 2, hg // (ATT_HEADS // ATT_KV_HEADS)
        qc = rope(qkv_ref[:, chunk * LANES:(chunk + 1) * LANES].astype(F32))
        if off != grp:
            qc = pltpu.roll(qc, HEAD_DIM, 1)
        on_group = (lane >= grp * HEAD_DIM) & (lane < (grp + 1) * HEAD_DIM)
        q_rows.append(jnp.where(on_group, qc * (HEAD_DIM ** -0.5), 0.0).astype(BF16))
    q_all = jnp.concatenate(q_rows, axis=0)

    k_prev = jnp.where(first_x, kmeta[...], kprev[...])
    v_prev = jnp.where(first_x, vmeta[...], vprev[...])
    kk = jnp.concatenate([k_prev, k_cur], axis=0)
    vv = jnp.concatenate([v_prev, v_cur], axis=0)
    kprev[...] = k_cur
    vprev[...] = v_cur
    kmeta[...] = jnp.where(is_meta, k_cur, kmeta[...])
    vmeta[...] = jnp.where(is_meta, v_cur, vmeta[...])
    yield

    sc = _dot_nt(q_all, kk).reshape(ATT_HEADS, BLK, 2 * BLK)
    yield

    rq = lax.broadcasted_iota(jnp.int32, (BLK, BLK), 0)
    ck = lax.broadcasted_iota(jnp.int32, (BLK, BLK), 1)
    from_cur = ck <= rq
    cur_min = jnp.where(is_meta, META_PAD, 0)
    prev_min = jnp.where(is_meta, BLK, jnp.where(first_x, META_PAD, 0))
    valid = ck >= jnp.where(from_cur, cur_min, prev_min)
    sc = jnp.where(from_cur[None], sc[:, :, BLK:], sc[:, :, :BLK])
    sc = jnp.where(valid[None], sc, NEG)

    hid = lax.broadcasted_iota(jnp.int32, (ATT_HEADS, 1, 1), 0)
    sink = jnp.zeros((ATT_HEADS, 1, 1), F32)
    for hg in range(ATT_HEADS):
        sink = jnp.where(hid == hg, sinks_ref[hg], sink)

    m = jnp.maximum(jnp.max(sc, axis=-1, keepdims=True), sink)
    p = jnp.exp(sc - m)
    denom = jnp.sum(p, axis=-1, keepdims=True) + jnp.exp(sink - m)
    p = jnp.concatenate([jnp.where(from_cur[None], 0.0, p), jnp.where(from_cur[None], p, 0.0)],
                        axis=-1)
    p = p.astype(BF16).reshape(ATT_HEADS * BLK, 2 * BLK)
    yield

    o = _dot(p, vv)
    yield

    o = o.reshape(ATT_HEADS, BLK, LANES) * (1.0 / denom)
    chunks = []
    for chunk in range(ATT_HEADS // 2):
        grp = (2 * chunk) // (ATT_HEADS // ATT_KV_HEADS)
        o0, o1 = o[2 * chunk], o[2 * chunk + 1]
        if grp != 0:
            o0 = pltpu.roll(o0, HEAD_DIM, 1)
        if grp != 1:
            o1 = pltpu.roll(o1, HEAD_DIM, 1)
        chunks.append(jnp.where(lane < HEAD_DIM, o0, o1).astype(BF16))
    result.append(jnp.concatenate(chunks, axis=1))


def _gla_tables():
    t = np.arange(BLK)
    tri = (t[:, None] >= t[None, :]).astype(np.float32)
    sel = np.zeros((len(GLA_FINE) * BLK, BLK), np.float32)
    for li, half in enumerate(GLA_FINE):
        boundary = (t & ~(2 * half - 1)) + half - 1
        sel[li * BLK + t, boundary] = 1.0
    x = t[:, None] ^ t[None, :]
    top = np.floor(np.log2(np.maximum(x, 1))).astype(np.int64)
    lvl = np.array([GLA_LEVELS.index(1 << int(v)) for v in top.ravel()]).reshape(BLK, BLK)
    lvl = np.where(t[:, None] == t[None, :], N_LEVELS, lvl)
    lvl = np.where(t[:, None] < t[None, :], N_LEVELS + 1, lvl)
    lvl = np.tile(lvl, (1, GLA_HEADS))
    return (jnp.asarray(tri, BF16), jnp.asarray(sel, BF16), jnp.asarray(lvl, jnp.int32))


def _gla_stages(result, is_meta, first_x, gla_ref, loga_ref, gr_ref, tri_ref, sel_ref, lvl_ref,
                hn_ref, state, state_meta):
    row = lax.broadcasted_iota(jnp.int32, (BLK, GLA_DK), 0)
    lane = lax.broadcasted_iota(jnp.int32, (BLK, GLA_DK), 1)
    is_token = row >= jnp.where(is_meta, META_PAD, 0)

    g3 = jnp.concatenate(_split3(loga_ref[...]), axis=1)
    yield
    b3 = _dot(tri_ref[...], g3)
    yield
    b = b3[:, :GLA_DK] + b3[:, GLA_DK:2 * GLA_DK] + b3[:, 2 * GLA_DK:]
    b_last = b[BLK - 1:BLK, :]

    b_hi = b.astype(BF16)
    b_mid = (b - b_hi.astype(F32)).astype(BF16)
    fine = _dot(sel_ref[...], jnp.concatenate([b_hi, b_mid], axis=1))

    qf = gla_ref[:, :GLA_DK].astype(F32) * (GLA_DK_HEAD ** -0.5)
    kf = jnp.where(is_token, gla_ref[:, GLA_DK:2 * GLA_DK].astype(F32), 0.0)

    head_mask = [jnp.where((lane >= h * GLA_DK_HEAD) & (lane < (h + 1) * GLA_DK_HEAD), 1.0, 0.0)
                 .astype(BF16) for h in range(GLA_HEADS)]

    def stack_heads(x):
        xb = x.astype(BF16)
        return jnp.concatenate([xb * hm for hm in head_mask], axis=0)

    state_in = jnp.where(first_x, state_meta[...], state[...])
    q_in = stack_heads(qf * jnp.exp(b))
    q_diag, k_diag = qf.astype(BF16), stack_heads(kf)
    yield
    o_inter = _dot(q_in, state_in.astype(BF16))
    lvl = lvl_ref[...]
    a = jnp.where(lvl == N_LEVELS, _dot_nt(q_diag, k_diag), 0.0)
    zeros = lambda shape: jnp.zeros(shape, F32)
    for li, half in enumerate(GLA_LEVELS):
        if half in GLA_FINE:
            fi = GLA_FINE.index(half)
            r = fine[fi * BLK:(fi + 1) * BLK, :GLA_DK] + fine[fi * BLK:(fi + 1) * BLK, GLA_DK:]
            upper = (row & half) != 0
            t_l = jnp.where(upper, qf, kf) * jnp.exp(-jnp.abs(b - r))
            q_l = jnp.where(upper, t_l, 0.0)
            k_l = jnp.where(upper, 0.0, t_l)
        else:
            grp = (BLK // (2 * half), 2 * half, GLA_DK)
            b_g, q_g, k_g = b.reshape(grp), qf.reshape(grp), kf.reshape(grp)
            r = b_g[:, half - 1:half, :]
            q_up = q_g[:, half:, :] * jnp.exp(b_g[:, half:, :] - r)
            k_lo = k_g[:, :half, :] * jnp.exp(r - b_g[:, :half, :])
            q_l = jnp.concatenate([zeros(q_up.shape), q_up], axis=1).reshape(BLK, GLA_DK)
            k_l = jnp.concatenate([k_lo, zeros(k_lo.shape)], axis=1).reshape(BLK, GLA_DK)
        q_l, k_l = q_l.astype(BF16), stack_heads(k_l)
        yield
        a = jnp.where(lvl == li, _dot_nt(q_l, k_l), a)
    a = a.astype(BF16)

    k_out_t = (kf * jnp.exp(b_last - b)).T.astype(BF16)
    decay_t = jnp.broadcast_to(jnp.exp(b_last), (BLK, GLA_DK)).T
    yield

    hn = hn_ref[...]
    new_state, heads = [], []
    for h in range(GLA_HEADS):
        v_h = gla_ref[:, 2 * GLA_DK + h * GLA_DV_HEAD:2 * GLA_DK + (h + 1) * GLA_DV_HEAD]
        o_h = o_inter[h * BLK:(h + 1) * BLK, :] + _dot(a[:, h * BLK:(h + 1) * BLK], v_h)
        new_state.append(_dot(k_out_t[h * GLA_DK_HEAD:(h + 1) * GLA_DK_HEAD, :], v_h))
        heads.append(o_h)
    yield
    for h in range(GLA_HEADS):
        o_h = heads[h]
        var = jnp.mean(o_h * o_h, axis=-1, keepdims=True)
        o_h = o_h * lax.rsqrt(var + EPS) * hn
        gate = gr_ref[:, h * GLA_DV_HEAD:(h + 1) * GLA_DV_HEAD].astype(F32)
        heads[h] = (o_h * gate).astype(BF16)
    state_new = state_in * decay_t + jnp.concatenate(new_state, axis=0)
    state[...] = state_new
    state_meta[...] = jnp.where(is_meta, state_new, state_meta[...])
    result.append(jnp.concatenate(heads, axis=1))


def _merge_stages(result, y_att, y_gla, gate, h, wpa_ref, wpg_ref, wo_ref):
    pa = _dot(y_att, wpa_ref[...])
    pg = _dot(y_gla, wpg_ref[...])
    yield
    mixed = (gate[:, :D_MODEL].astype(F32) * pa + gate[:, D_MODEL:].astype(F32) * pg).astype(BF16)
    yield
    result.append(h + _dot(mixed, wo_ref[...]))


def _advance(gens):
    alive = []
    for gen in gens:
        try:
            next(gen)
            alive.append(gen)
        except StopIteration:
            pass
    return alive


def _mixer_kernel(nb, sinks_ref, xnext_ref, meta_ref, g_ref, w_ref, b_ref, wal_hi_ref, wal_lo_ref,
                  bal_ref, cos_ref, sin_ref, tri_ref, sel_ref, lvl_ref, hn_ref, wpa_ref, wpg_ref,
                  wo_ref, wup_f32_ref, wdn_f32_ref, out_ref, outm_ref, wup_bf16_ref, wdn_bf16_ref,
                  cur_h, cur_qkv, cur_gla, cur_gr, cur_gate, cur_loga,
                  nxt_qkv, nxt_gla, nxt_gr, nxt_gate, nxt_loga,
                  kprev, vprev, kmeta, vmeta, state, state_meta):
    s = pl.program_id(0)
    is_meta = s == 0
    first_x = jnp.logical_and(s >= 1, lax.rem(jnp.maximum(s - 1, 0), nb) == 0)
    cur = (cur_qkv, cur_gla, cur_gr, cur_gate, cur_loga)
    nxt = (nxt_qkv, nxt_gla, nxt_gr, nxt_gate, nxt_loga)
    weights = (g_ref, w_ref, b_ref, wal_hi_ref, wal_lo_ref, bal_ref)
    groups = range(N_GROUPS)

    @pl.when(is_meta)
    def _():
        for ref in (kprev, vprev, kmeta, vmeta, state, state_meta):
            ref[...] = jnp.zeros_like(ref)

        def store_all_groups(i, c0, c1, val):
            for g in groups:
                cur[i][g, :, c0:c1] = val

        for g in groups:
            cur_h[g] = meta_ref[...]
        meta_proj = [_inproj_stages(meta_ref[...], weights, store_all_groups)]
        while meta_proj:
            meta_proj = _advance(meta_proj)

    def store_next(i, c0, c1, val):
        nxt[i][:, :, c0:c1] = val.reshape(N_GROUPS, BLK, c1 - c0)

    x_next = xnext_ref[...]
    inproj = [_inproj_stages(x_next.reshape(N_GROUPS * BLK, D_MODEL), weights, store_next)]

    y_att, y_gla, h_mid = [], [], []
    mixers = []
    for g in groups:
        mixers.append(_attn_stages(y_att, is_meta, first_x, sinks_ref, cur_qkv.at[g], cos_ref,
                                   sin_ref, kprev.at[g], vprev.at[g], kmeta.at[g], vmeta.at[g]))
        mixers.append(_gla_stages(y_gla, is_meta, first_x, cur_gla.at[g], cur_loga.at[g],
                                  cur_gr.at[g], tri_ref, sel_ref, lvl_ref, hn_ref,
                                  state.at[g], state_meta.at[g]))
    while mixers:
        inproj = _advance(inproj)
        mixers = _advance(mixers)

    gate = cur_gate[...].reshape(N_GROUPS * BLK, C_GATE)
    h = cur_h[...].reshape(N_GROUPS * BLK, D_MODEL)
    merge = [_merge_stages(h_mid, jnp.concatenate(y_att, axis=0), jnp.concatenate(y_gla, axis=0),
                           gate, h, wpa_ref, wpg_ref, wo_ref)]
    while merge or inproj:
        merge = _advance(merge)
        inproj = _advance(inproj)

    h_mid = h_mid[0].reshape(N_GROUPS, BLK, D_MODEL)
    out_ref[...] = h_mid

    wup_bf16_ref[...] = wup_f32_ref[...].astype(BF16)
    wdn_bf16_ref[...] = wdn_f32_ref[...].astype(BF16)

    cur_h[...] = x_next
    for c_ref, n_ref in zip(cur, nxt):
        c_ref[...] = n_ref[...]

    @pl.when(is_meta)
    def _():
        outm_ref[...] = out_ref[0]


def _mixer(x_groups, meta_block, mix_norm, w_cat, b_cat, wal_hi, wal_lo, b_alpha, sinks, cos2, sin2,
           head_norm, wpa, wpg, wo, wup_f32, wdn_f32, nb):
    rows_g = x_groups.shape[1]
    nblocks_g = rows_g // BLK
    assert wup_f32.shape[0] % nblocks_g == 0 and wdn_f32.shape[0] % nblocks_g == 0
    slab = lambda w: pl.BlockSpec((w.shape[0] // nblocks_g, w.shape[1]),
                                  lambda s: (jnp.maximum(s - 1, 0), 0))
    tri, sel, lvl = _gla_tables()
    const = lambda s: (0, 0)
    xnext = lambda s: (0, jnp.minimum(s, nblocks_g - 1), 0)
    xout = lambda s: (0, jnp.maximum(s - 1, 0), 0)
    pos = lambda s: (jnp.where(s == 0, 0, lax.rem(jnp.maximum(s - 1, 0), nb) + 1), 0)
    grouped = lambda cols, dt: pltpu.VMEM((N_GROUPS, BLK, cols), dt)
    proj_bufs = [grouped(C_QKV, BF16), grouped(C_GLA, BF16), grouped(C_GR, BF16),
                 grouped(C_GATE, BF16), grouped(GLA_DK, F32)]
    return pl.pallas_call(
        functools.partial(_mixer_kernel, nb),
        grid=(nblocks_g + 1,),
        in_specs=[
            pl.BlockSpec(memory_space=pltpu.SMEM),
            pl.BlockSpec((N_GROUPS, BLK, D_MODEL), xnext),
            pl.BlockSpec((BLK, D_MODEL), const),
            pl.BlockSpec((1, D_MODEL), const),
            pl.BlockSpec((D_MODEL, C_ALL), const),
            pl.BlockSpec((1, C_ALL), const),
            pl.BlockSpec((LANES, GLA_DK), const),
            pl.BlockSpec((LANES, GLA_DK), const),
            pl.BlockSpec((1, GLA_DK), const),
            pl.BlockSpec((BLK, LANES), pos),
            pl.BlockSpec((BLK, LANES), pos),
            pl.BlockSpec((BLK, BLK), const),
            pl.BlockSpec((len(GLA_FINE) * BLK, BLK), const),
            pl.BlockSpec((BLK, GLA_HEADS * BLK), const),
            pl.BlockSpec((1, GLA_DV_HEAD), const),
            pl.BlockSpec((ATT_Q, D_MODEL), const),
            pl.BlockSpec((GLA_DV, D_MODEL), const),
            pl.BlockSpec((D_MODEL, D_MODEL), const),
            slab(wup_f32),
            slab(wdn_f32),
        ],
        out_specs=[pl.BlockSpec((N_GROUPS, BLK, D_MODEL), xout),
                   pl.BlockSpec((BLK, D_MODEL), const),
                   slab(wup_f32), slab(wdn_f32)],
        out_shape=[jax.ShapeDtypeStruct((N_GROUPS, rows_g, D_MODEL), F32),
                   jax.ShapeDtypeStruct((BLK, D_MODEL), F32),
                   jax.ShapeDtypeStruct(wup_f32.shape, BF16),
                   jax.ShapeDtypeStruct(wdn_f32.shape, BF16)],
        scratch_shapes=[grouped(D_MODEL, F32)] + proj_bufs + proj_bufs
        + [grouped(ATT_KV, BF16)] * 4
        + [pltpu.VMEM((N_GROUPS, GLA_DK, GLA_DV_HEAD), F32)] * 2,
        compiler_params=_params(),
        name="mixer",
    )(sinks, x_groups, meta_block, mix_norm, w_cat, b_cat, wal_hi, wal_lo, b_alpha, cos2, sin2,
      tri, sel, lvl, head_norm, wpa, wpg, wo, wup_f32, wdn_f32)


def _ffn_kernel(tm, n_chunks, tiles_per_seq, h_ref, halo_ref, halo_meta_ref, g_ref, wup_ref, cw_ref,
                cb_ref, wdn_ref, gf_ref, out_ref, u_scr):
    def normed(x):
        var = jnp.mean(x * x, axis=-1, keepdims=True)
        return (x * lax.rsqrt(var + EPS) * g_ref[...]).astype(BF16)

    h = h_ref[...]
    seq_start = lax.rem(pl.program_id(0), tiles_per_seq) == 0
    u_scr[0:HALO, :] = normed(jnp.where(seq_start, halo_meta_ref[...], halo_ref[...]))
    u_scr[HALO:, :] = normed(h)
    u = u_scr[...]

    fc = D_FF // n_chunks
    acc = None
    for c in range(n_chunks):
        a = _dot(u, wup_ref[:, c * fc:(c + 1) * fc])
        v = _dot(u[HALO:], wup_ref[:, D_FF + c * fc:D_FF + (c + 1) * fc])
        cw = cw_ref[:, c * fc:(c + 1) * fc]
        conv = (cw[0:1] * pltpu.roll(a, 2, 0) + cw[1:2] * pltpu.roll(a, 1, 0) + cw[2:3] * a
                + cb_ref[:, c * fc:(c + 1) * fc])[HALO:]
        inner = 0.7978845608028654 * (conv + 0.044715 * (conv * conv * conv))
        act = conv * (0.5 * (1.0 + jnp.tanh(inner)))
        part = _dot((act * v).astype(BF16), wdn_ref[c * fc:(c + 1) * fc, :])
        acc = part if acc is None else acc + part

    h2 = h + acc
    var = jnp.mean(h2 * h2, axis=-1, keepdims=True)
    out_ref[...] = h2 * lax.rsqrt(var + EPS) * gf_ref[...]


def _ffn(h_mid, h_meta, ffn_norm, wup, conv_w, conv_b, wdn, final_norm, seq, tm, n_chunks):
    rows_x = h_mid.shape[0]
    assert seq % tm == 0 and rows_x % seq == 0
    tiles_per_seq = seq // tm
    const = lambda i: (0, 0)
    row = lambda i: (i, 0)
    return pl.pallas_call(
        functools.partial(_ffn_kernel, tm, n_chunks, tiles_per_seq),
        grid=(rows_x // tm,),
        in_specs=[
            pl.BlockSpec((tm, D_MODEL), row),
            pl.BlockSpec((HALO, D_MODEL), lambda i: (jnp.maximum(i * (tm // HALO) - 1, 0), 0)),
            pl.BlockSpec((HALO, D_MODEL), lambda i: (BLK // HALO - 1, 0)),
            pl.BlockSpec((1, D_MODEL), const),
            pl.BlockSpec((D_MODEL, 2 * D_FF), const, pipeline_mode=pl.Buffered(1)),
            pl.BlockSpec((8, D_FF), const),
            pl.BlockSpec((1, D_FF), const),
            pl.BlockSpec((D_FF, D_MODEL), const, pipeline_mode=pl.Buffered(1)),
            pl.BlockSpec((1, D_MODEL), const),
        ],
        out_specs=pl.BlockSpec((tm, D_MODEL), row),
        out_shape=jax.ShapeDtypeStruct((rows_x, D_MODEL), F32),
        scratch_shapes=[pltpu.VMEM((HALO + tm, D_MODEL), BF16)],
        compiler_params=_params(),
        name="conv_ffn",
    )(h_mid, h_mid, h_meta, ffn_norm, wup, conv_w, conv_b, wdn, final_norm)


def _rope_tables(nb):
    half = HEAD_DIM // 2
    inv_freq = ROPE_THETA ** (-jnp.arange(half, dtype=F32) / half)
    pos = (jnp.arange((nb + 1) * BLK) - META_PAD).astype(F32)
    ang = pos[:, None] * inv_freq[None, :]
    cos2 = jnp.tile(jnp.cos(ang), (1, LANES // half))
    sin = jnp.sin(ang)
    sin2 = jnp.tile(jnp.concatenate([-sin, sin], axis=-1), (1, LANES // HEAD_DIM))
    return cos2, sin2


def kernel(x, meta_tokens, mix_norm, w_in, b_in, w_alpha, b_alpha, attn_sinks, gla_head_norm,
           w_proj_attn, w_proj_gla, w_out, ffn_norm, w_up, conv_w, conv_b, w_down, final_norm):
    batch, seq, _ = x.shape
    assert batch % N_GROUPS == 0 and seq % BLK == 0
    rows_x = batch * seq
    nb = seq // BLK
    l = 0

    meta_block = jnp.concatenate(
        [jnp.zeros((META_PAD, D_MODEL), x.dtype), meta_tokens.astype(x.dtype)], axis=0)

    lo0 = LOW0
    wi, bi = w_in[l], b_in[l]
    w_cat = _prep_w(wi, tr=128)
    b_cat = jnp.concatenate(
        [bi[:lo0], bi[lo0 + GLA_RANK:], bi[lo0:lo0 + GLA_RANK],
         jnp.zeros((C_LOW - GLA_RANK,), bi.dtype)])[None, :]
    wal = jnp.concatenate(
        [w_alpha[l], jnp.zeros((C_LOW - GLA_RANK, GLA_DK), w_alpha.dtype)], axis=0)
    wal_hi = wal.astype(BF16)
    wal_lo = (wal - wal_hi.astype(F32)).astype(BF16)

    cos2, sin2 = _rope_tables(nb)
    x_groups = x.reshape(N_GROUPS, rows_x // N_GROUPS, D_MODEL)
    wup_view = w_up[l].reshape(D_MODEL, 2 * D_FF)
    wdn_view = w_down[l].reshape(D_MODEL, D_FF)
    h_mid, h_meta, wup, wdn = _mixer(
        x_groups, meta_block, mix_norm[l][None, :], w_cat, b_cat, wal_hi, wal_lo,
        b_alpha[l][None, :], attn_sinks[l], cos2, sin2, gla_head_norm[l][None, :],
        w_proj_attn[l].astype(BF16), w_proj_gla[l].astype(BF16), w_out[l].astype(BF16),
        wup_view, wdn_view, nb)

    cw = jnp.concatenate([conv_w[l], jnp.zeros((8 - conv_w.shape[1], D_FF), conv_w.dtype)], axis=0)
    out = _ffn(h_mid.reshape(rows_x, D_MODEL), h_meta, ffn_norm[l][None, :], wup,
               cw, conv_b[l][None, :], wdn.reshape(D_FF, D_MODEL), final_norm[None, :], seq,
               tm=1024, n_chunks=1)
    return out.reshape(batch, seq, D_MODEL)
```

```python
import functools

import numpy as np
import jax
import jax.numpy as jnp
from jax import lax
from jax.experimental import pallas as pl
from jax.experimental.pallas import tpu as pltpu

D_MODEL = 1024
N_META = 16
EPS = 1e-6
ATT_HEADS = 8
ATT_KV_HEADS = 2
HEAD_DIM = 64
BLK = 128
ROPE_THETA = 10000.0
GLA_HEADS = 4
GLA_DK = 256
GLA_DV = 512
GLA_DK_HEAD = GLA_DK // GLA_HEADS
GLA_DV_HEAD = GLA_DV // GLA_HEADS
GLA_RANK = 16
GLA_TAU = 16.0
D_FF = 2816
ATT_Q = ATT_HEADS * HEAD_DIM
ATT_KV = ATT_KV_HEADS * HEAD_DIM

META_PAD = BLK - N_META
LANES = 128
HALO = 16
NEG = -1e30
VMEM_LIMIT = 56 * 1024 * 1024

C_QKV = ATT_Q + 2 * ATT_KV
C_GLA = 2 * GLA_DK + GLA_DV
C_GR = GLA_DV
C_GATE = 2 * D_MODEL
C_LOW = LANES
O_QKV = 0
O_GLA = O_QKV + C_QKV
O_GR = O_GLA + C_GLA
O_GATE = O_GR + C_GR
O_LOW = O_GATE + C_GATE
C_ALL = O_LOW + C_LOW
PROJ_CHUNK = 256

GLA_LEVELS = (64, 32, 16, 8, 4, 2, 1)
N_LEVELS = len(GLA_LEVELS)
GLA_FINE = tuple(h for h in GLA_LEVELS if h < 8)

N_GROUPS = 2

BF16 = jnp.bfloat16
F32 = jnp.float32


def _sigmoid(x):
    return 1.0 / (1.0 + jnp.exp(-x))


def _dot(a, b):
    return jnp.dot(a, b, preferred_element_type=F32)


def _dot_nt(a, b):
    return lax.dot_general(a, b, (((1,), (1,)), ((), ())), preferred_element_type=F32)


def _split3(x):
    hi = x.astype(BF16)
    r1 = x - hi.astype(F32)
    mid = r1.astype(BF16)
    lo = (r1 - mid.astype(F32)).astype(BF16)
    return hi, mid, lo


def _params():
    return pltpu.CompilerParams(dimension_semantics=("arbitrary",), vmem_limit_bytes=VMEM_LIMIT)


LOW0 = ATT_Q + 2 * ATT_KV + 2 * GLA_DK + 2 * GLA_DV
D_IN = LOW0 + GLA_RANK + C_GATE


def _prep_w_kernel(wt_ref, out_ref):
    for t in range(C_ALL // LANES):
        col = t * LANES
        if col < LOW0:
            rows = wt_ref[col:col + LANES, :]
        elif col < O_LOW:
            src = LOW0 + GLA_RANK + (col - O_GATE)
            rows = wt_ref[src:src + LANES, :]
        else:
            rows = jnp.concatenate([wt_ref[LOW0:LOW0 + GLA_RANK, :],
                                    jnp.zeros((C_LOW - GLA_RANK, D_MODEL), F32)], axis=0)
        out_ref[:, col:col + LANES] = rows.T.astype(BF16)


def _prep_w(w, layer):
    wt = jnp.swapaxes(w, 1, 2)
    return pl.pallas_call(
        _prep_w_kernel,
        grid=(1,),
        in_specs=[pl.BlockSpec((None, D_IN, D_MODEL), lambda i: (layer, 0, 0),
                               pipeline_mode=pl.Buffered(1))],
        out_specs=pl.BlockSpec((D_MODEL, C_ALL), lambda i: (0, 0)),
        out_shape=jax.ShapeDtypeStruct((D_MODEL, C_ALL), BF16),
        compiler_params=_params(),
        name="prep_w_in",
    )(wt)


def _inproj_stages(x, weights, store):
    g_ref, w_ref, b_ref, wal_hi_ref, wal_lo_ref, bal_ref = weights
    var = jnp.mean(x * x, axis=-1, keepdims=True)
    u = (x * lax.rsqrt(var + EPS) * g_ref[...]).astype(BF16)
    yield

    def proj(off, width):
        return _dot(u, w_ref[:, off:off + width]) + b_ref[:, off:off + width]

    low = proj(O_LOW, C_LOW)
    yield
    silu = lambda y: y * _sigmoid(y)
    plan = ((0, O_QKV, C_QKV, None), (1, O_GLA, C_GLA, None), (2, O_GR, C_GR, silu),
            (3, O_GATE, C_GATE, _sigmoid))
    low_pending = True
    for i, off, width, act in plan:
        for c in range(0, width, PROJ_CHUNK):
            y = proj(off + c, PROJ_CHUNK)
            store(i, c, c + PROJ_CHUNK, (y if act is None else act(y)).astype(BF16))
            yield
            if low_pending:
                low_pending = False
                low_hi = low.astype(BF16)
                low_lo = (low - low_hi.astype(F32)).astype(BF16)
                z = (_dot(low_hi, wal_hi_ref[...]) + _dot(low_lo, wal_hi_ref[...])
                     + _dot(low_hi, wal_lo_ref[...]) + bal_ref[...])
                log_sig = jnp.minimum(z, 0.0) - jnp.log1p(jnp.exp(-jnp.abs(z)))
                store(4, 0, GLA_DK, log_sig * (1.0 / GLA_TAU))
                yield


def _attn_stages(result, is_meta, first_x, sinks_ref, qkv_ref, cos_ref, sin_ref,
                 kprev, vprev, kmeta, vmeta):
    cos = cos_ref[...]
    sin = sin_ref[...]
    lane = lax.broadcasted_iota(jnp.int32, (BLK, LANES), 1)
    first_half = (lane & (HEAD_DIM - 1)) < HEAD_DIM // 2

    def rope(xc):
        rot = jnp.where(first_half, pltpu.roll(xc, LANES - HEAD_DIM // 2, 1),
                        pltpu.roll(xc, HEAD_DIM // 2, 1))
        return xc * cos + rot * sin

    k_cur = rope(qkv_ref[:, ATT_Q:ATT_Q + ATT_KV].astype(F32)).astype(BF16)
    v_cur = qkv_ref[:, ATT_Q + ATT_KV:ATT_Q + 2 * ATT_KV]

    q_rows = []
    for hg in range(ATT_HEADS):
        chunk, off, grp = hg // 2, hg % 2, hg // (ATT_HEADS // ATT_KV_HEADS)
        qc = rope(qkv_ref[:, chunk * LANES:(chunk + 1) * LANES].astype(F32))
        if off != grp:
            qc = pltpu.roll(qc, HEAD_DIM, 1)
        on_group = (lane >= grp * HEAD_DIM) & (lane < (grp + 1) * HEAD_DIM)
        q_rows.append(jnp.where(on_group, qc * (HEAD_DIM ** -0.5), 0.0).astype(BF16))
    q_all = jnp.concatenate(q_rows, axis=0)

    k_prev = jnp.where(first_x, kmeta[...], kprev[...])
    v_prev = jnp.where(first_x, vmeta[...], vprev[...])
    kk = jnp.concatenate([k_prev, k_cur], axis=0)
    vv = jnp.concatenate([v_prev, v_cur], axis=0)
    kprev[...] = k_cur
    vprev[...] = v_cur
    kmeta[...] = jnp.where(is_meta, k_cur, kmeta[...])
    vmeta[...] = jnp.where(is_meta, v_cur, vmeta[...])
    yield

    sc = _dot_nt(q_all, kk).reshape(ATT_HEADS, BLK, 2 * BLK)
    yield

    rq = lax.broadcasted_iota(jnp.int32, (BLK, BLK), 0)
    ck = lax.broadcasted_iota(jnp.int32, (BLK, BLK), 1)
    from_cur = ck <= rq
    cur_min = jnp.where(is_meta, META_PAD, 0)
    prev_min = jnp.where(is_meta, BLK, jnp.where(first_x, META_PAD, 0))
    valid = ck >= jnp.where(from_cur, cur_min, prev_min)
    sc = jnp.where(from_cur[None], sc[:, :, BLK:], sc[:, :, :BLK])
    sc = jnp.where(valid[None], sc, NEG)

    hid = lax.broadcasted_iota(jnp.int32, (ATT_HEADS, 1, 1), 0)
    sink = jnp.zeros((ATT_HEADS, 1, 1), F32)
    for hg in range(ATT_HEADS):
        sink = jnp.where(hid == hg, sinks_ref[hg], sink)

    m = jnp.maximum(jnp.max(sc, axis=-1, keepdims=True), sink)
    p = jnp.exp(sc - m)
    denom = jnp.sum(p, axis=-1, keepdims=True) + jnp.exp(sink - m)
    p = jnp.concatenate([jnp.where(from_cur[None], 0.0, p), jnp.where(from_cur[None], p, 0.0)],
                        axis=-1)
    p = p.astype(BF16).reshape(ATT_HEADS * BLK, 2 * BLK)
    yield

    o = _dot(p, vv)
    yield

    o = o.reshape(ATT_HEADS, BLK, LANES) * (1.0 / denom)
    chunks = []
    for chunk in range(ATT_HEADS // 2):
        grp = (2 * chunk) // (ATT_HEADS // ATT_KV_HEADS)
        o0, o1 = o[2 * chunk], o[2 * chunk + 1]
        if grp != 0:
            o0 = pltpu.roll(o0, HEAD_DIM, 1)
        if grp != 1:
            o1 = pltpu.roll(o1, HEAD_DIM, 1)
        chunks.append(jnp.where(lane < HEAD_DIM, o0, o1).astype(BF16))
    result.append(jnp.concatenate(chunks, axis=1))


def _gla_tables():
    t = np.arange(BLK)
    tri = (t[:, None] >= t[None, :]).astype(np.float32)
    sel = np.zeros((len(GLA_FINE) * BLK, BLK), np.float32)
    for li, half in enumerate(GLA_FINE):
        boundary = (t & ~(2 * half - 1)) + half - 1
        sel[li * BLK + t, boundary] = 1.0
    x = t[:, None] ^ t[None, :]
    top = np.floor(np.log2(np.maximum(x, 1))).astype(np.int64)
    lvl = np.array([GLA_LEVELS.index(1 << int(v)) for v in top.ravel()]).reshape(BLK, BLK)
    lvl = np.where(t[:, None] == t[None, :], N_LEVELS, lvl)
    lvl = np.where(t[:, None] < t[None, :], N_LEVELS + 1, lvl)
    lvl = np.tile(lvl, (1, GLA_HEADS))
    return (jnp.asarray(tri, BF16), jnp.asarray(sel, BF16), jnp.asarray(lvl, jnp.int32))


def _gla_stages(result, is_meta, first_x, gla_ref, loga_ref, gr_ref, tri_ref, sel_ref, lvl_ref,
                hn_ref, state, state_meta):
    row = lax.broadcasted_iota(jnp.int32, (BLK, GLA_DK), 0)
    lane = lax.broadcasted_iota(jnp.int32, (BLK, GLA_DK), 1)
    is_token = row >= jnp.where(is_meta, META_PAD, 0)

    g3 = jnp.concatenate(_split3(loga_ref[...]), axis=1)
    yield
    b3 = _dot(tri_ref[...], g3)
    yield
    b = b3[:, :GLA_DK] + b3[:, GLA_DK:2 * GLA_DK] + b3[:, 2 * GLA_DK:]
    b_last = b[BLK - 1:BLK, :]

    b_hi = b.astype(BF16)
    b_mid = (b - b_hi.astype(F32)).astype(BF16)
    fine = _dot(sel_ref[...], jnp.concatenate([b_hi, b_mid], axis=1))

    qf = gla_ref[:, :GLA_DK].astype(F32) * (GLA_DK_HEAD ** -0.5)
    kf = jnp.where(is_token, gla_ref[:, GLA_DK:2 * GLA_DK].astype(F32), 0.0)

    head_mask = [jnp.where((lane >= h * GLA_DK_HEAD) & (lane < (h + 1) * GLA_DK_HEAD), 1.0, 0.0)
                 .astype(BF16) for h in range(GLA_HEADS)]

    def stack_heads(x):
        xb = x.astype(BF16)
        return jnp.concatenate([xb * hm for hm in head_mask], axis=0)

    state_in = jnp.where(first_x, state_meta[...], state[...])
    q_in = stack_heads(qf * jnp.exp(b))
    q_diag, k_diag = qf.astype(BF16), stack_heads(kf)
    yield
    o_inter = _dot(q_in, state_in.astype(BF16))
    lvl = lvl_ref[...]
    a = jnp.where(lvl == N_LEVELS, _dot_nt(q_diag, k_diag), 0.0)
    zeros = lambda shape: jnp.zeros(shape, F32)
    for li, half in enumerate(GLA_LEVELS):
        if half in GLA_FINE:
            fi = GLA_FINE.index(half)
            r = fine[fi * BLK:(fi + 1) * BLK, :GLA_DK] + fine[fi * BLK:(fi + 1) * BLK, GLA_DK:]
            upper = (row & half) != 0
            t_l = jnp.where(upper, qf, kf) * jnp.exp(-jnp.abs(b - r))
            q_l = jnp.where(upper, t_l, 0.0)
            k_l = jnp.where(upper, 0.0, t_l)
        else:
            grp = (BLK // (2 * half), 2 * half, GLA_DK)
            b_g, q_g, k_g = b.reshape(grp), qf.reshape(grp), kf.reshape(grp)
            r = b_g[:, half - 1:half, :]
            q_up = q_g[:, half:, :] * jnp.exp(b_g[:, half:, :] - r)
            k_lo = k_g[:, :half, :] * jnp.exp(r - b_g[:, :half, :])
            q_l = jnp.concatenate([zeros(q_up.shape), q_up], axis=1).reshape(BLK, GLA_DK)
            k_l = jnp.concatenate([k_lo, zeros(k_lo.shape)], axis=1).reshape(BLK, GLA_DK)
        q_l, k_l = q_l.astype(BF16), stack_heads(k_l)
        yield
        a = jnp.where(lvl == li, _dot_nt(q_l, k_l), a)
    a = a.astype(BF16)

    k_out_t = (kf * jnp.exp(b_last - b)).T.astype(BF16)
    decay_t = jnp.broadcast_to(jnp.exp(b_last), (BLK, GLA_DK)).T
    yield

    hn = hn_ref[...]
    new_state, heads = [], []
    for h in range(GLA_HEADS):
        v_h = gla_ref[:, 2 * GLA_DK + h * GLA_DV_HEAD:2 * GLA_DK + (h + 1) * GLA_DV_HEAD]
        o_h = o_inter[h * BLK:(h + 1) * BLK, :] + _dot(a[:, h * BLK:(h + 1) * BLK], v_h)
        new_state.append(_dot(k_out_t[h * GLA_DK_HEAD:(h + 1) * GLA_DK_HEAD, :], v_h))
        heads.append(o_h)
    yield
    for h in range(GLA_HEADS):
        o_h = heads[h]
        var = jnp.mean(o_h * o_h, axis=-1, keepdims=True)
        o_h = o_h * lax.rsqrt(var + EPS) * hn
        gate = gr_ref[:, h * GLA_DV_HEAD:(h + 1) * GLA_DV_HEAD].astype(F32)
        heads[h] = (o_h * gate).astype(BF16)
    state_new = state_in * decay_t + jnp.concatenate(new_state, axis=0)
    state[...] = state_new
    state_meta[...] = jnp.where(is_meta, state_new, state_meta[...])
    result.append(jnp.concatenate(heads, axis=1))


def _merge_stages(result, y_att, y_gla, gate, h, wpa_ref, wpg_ref, wo_ref):
    pa = _dot(y_att, wpa_ref[...])
    pg = _dot(y_gla, wpg_ref[...])
    yield
    mixed = (gate[:, :D_MODEL].astype(F32) * pa + gate[:, D_MODEL:].astype(F32) * pg).astype(BF16)
    yield
    result.append(h + _dot(mixed, wo_ref[...]))


def _advance(gens):
    alive = []
    for gen in gens:
        try:
            next(gen)
            alive.append(gen)
        except StopIteration:
            pass
    return alive


def _mixer_kernel(nb, sinks_ref, xnext_ref, meta_ref, g_ref, w_ref, b_ref, wal_hi_ref, wal_lo_ref,
                  bal_ref, cos_ref, sin_ref, tri_ref, sel_ref, lvl_ref, hn_ref, wpa_ref, wpg_ref,
                  wo_ref, wup_f32_ref, wdn_f32_ref, out_ref, outm_ref, wup_bf16_ref, wdn_bf16_ref,
                  cur_h, cur_qkv, cur_gla, cur_gr, cur_gate, cur_loga,
                  nxt_qkv, nxt_gla, nxt_gr, nxt_gate, nxt_loga,
                  kprev, vprev, kmeta, vmeta, state, state_meta):
    s = pl.program_id(0)
    is_meta = s == 0
    first_x = jnp.logical_and(s >= 1, lax.rem(jnp.maximum(s - 1, 0), nb) == 0)
    cur = (cur_qkv, cur_gla, cur_gr, cur_gate, cur_loga)
    nxt = (nxt_qkv, nxt_gla, nxt_gr, nxt_gate, nxt_loga)
    weights = (g_ref, w_ref, b_ref, wal_hi_ref, wal_lo_ref, bal_ref)
    groups = range(N_GROUPS)

    @pl.when(is_meta)
    def _():
        for ref in (kprev, vprev, kmeta, vmeta, state, state_meta):
            ref[...] = jnp.zeros_like(ref)

        def store_all_groups(i, c0, c1, val):
            for g in groups:
                cur[i][g, :, c0:c1] = val

        for g in groups:
            cur_h[g] = meta_ref[...]
        meta_proj = [_inproj_stages(meta_ref[...], weights, store_all_groups)]
        while meta_proj:
            meta_proj = _advance(meta_proj)

    def store_next(i, c0, c1, val):
        nxt[i][:, :, c0:c1] = val.reshape(N_GROUPS, BLK, c1 - c0)

    x_next = xnext_ref[...]
    inproj = [_inproj_stages(x_next.reshape(N_GROUPS * BLK, D_MODEL), weights, store_next)]

    y_att, y_gla, h_mid = [], [], []
    mixers = []
    for g in groups:
        mixers.append(_attn_stages(y_att, is_meta, first_x, sinks_ref, cur_qkv.at[g], cos_ref,
                                   sin_ref, kprev.at[g], vprev.at[g], kmeta.at[g], vmeta.at[g]))
        mixers.append(_gla_stages(y_gla, is_meta, first_x, cur_gla.at[g], cur_loga.at[g],
                                  cur_gr.at[g], tri_ref, sel_ref, lvl_ref, hn_ref,
                                  state.at[g], state_meta.at[g]))
    while mixers:
        inproj = _advance(inproj)
        mixers = _advance(mixers)

    gate = cur_gate[...].reshape(N_GROUPS * BLK, C_GATE)
    h = cur_h[...].reshape(N_GROUPS * BLK, D_MODEL)
    merge = [_merge_stages(h_mid, jnp.concatenate(y_att, axis=0), jnp.concatenate(y_gla, axis=0),
                           gate, h, wpa_ref, wpg_ref, wo_ref)]
    while merge or inproj:
        merge = _advance(merge)
        inproj = _advance(inproj)

    h_mid = h_mid[0].reshape(N_GROUPS, BLK, D_MODEL)
    out_ref[...] = h_mid

    wup_bf16_ref[...] = wup_f32_ref[...].astype(BF16)
    wdn_bf16_ref[...] = wdn_f32_ref[...].astype(BF16)

    cur_h[...] = x_next
    for c_ref, n_ref in zip(cur, nxt):
        c_ref[...] = n_ref[...]

    @pl.when(is_meta)
    def _():
        outm_ref[...] = out_ref[0]


def _mixer(x_groups, meta_block, mix_norm, w_cat, b_cat, wal_hi, wal_lo, b_alpha, sinks, cos2, sin2,
           head_norm, wpa, wpg, wo, wup_f32, wdn_f32, layer, nb):
    rows_g = x_groups.shape[1]
    nblocks_g = rows_g // BLK

    def slab(w, rows, out):
        n = w.shape[1] // rows
        assert w.shape[1] % rows == 0 and n <= nblocks_g
        idx = lambda s: jnp.minimum(jnp.maximum(s - 1, 0), n - 1)
        if out:
            return pl.BlockSpec((rows, w.shape[2]), lambda s: (idx(s), 0))
        return pl.BlockSpec((None, rows, w.shape[2]), lambda s: (layer, idx(s), 0))

    wup_rows, wdn_rows = 16, 64
    tri, sel, lvl = _gla_tables()
    const = lambda s: (0, 0)
    xnext = lambda s: (0, jnp.minimum(s, nblocks_g - 1), 0)
    xout = lambda s: (0, jnp.maximum(s - 1, 0), 0)
    pos = lambda s: (jnp.where(s == 0, 0, lax.rem(jnp.maximum(s - 1, 0), nb) + 1), 0)
    grouped = lambda cols, dt: pltpu.VMEM((N_GROUPS, BLK, cols), dt)
    proj_bufs = [grouped(C_QKV, BF16), grouped(C_GLA, BF16), grouped(C_GR, BF16),
                 grouped(C_GATE, BF16), grouped(GLA_DK, F32)]
    return pl.pallas_call(
        functools.partial(_mixer_kernel, nb),
        grid=(nblocks_g + 1,),
        in_specs=[
            pl.BlockSpec(memory_space=pltpu.SMEM),
            pl.BlockSpec((N_GROUPS, BLK, D_MODEL), xnext),
            pl.BlockSpec((BLK, D_MODEL), const),
            pl.BlockSpec((1, D_MODEL), const),
            pl.BlockSpec((D_MODEL, C_ALL), const),
            pl.BlockSpec((1, C_ALL), const),
            pl.BlockSpec((LANES, GLA_DK), const),
            pl.BlockSpec((LANES, GLA_DK), const),
            pl.BlockSpec((1, GLA_DK), const),
            pl.BlockSpec((BLK, LANES), pos),
            pl.BlockSpec((BLK, LANES), pos),
            pl.BlockSpec((BLK, BLK), const),
            pl.BlockSpec((len(GLA_FINE) * BLK, BLK), const),
            pl.BlockSpec((BLK, GLA_HEADS * BLK), const),
            pl.BlockSpec((1, GLA_DV_HEAD), const),
            pl.BlockSpec((ATT_Q, D_MODEL), const),
            pl.BlockSpec((GLA_DV, D_MODEL), const),
            pl.BlockSpec((D_MODEL, D_MODEL), const),
            slab(wup_f32, wup_rows, False),
            slab(wdn_f32, wdn_rows, False),
        ],
        out_specs=[pl.BlockSpec((N_GROUPS, BLK, D_MODEL), xout),
                   pl.BlockSpec((BLK, D_MODEL), const),
                   slab(wup_f32, wup_rows, True), slab(wdn_f32, wdn_rows, True)],
        out_shape=[jax.ShapeDtypeStruct((N_GROUPS, rows_g, D_MODEL), F32),
                   jax.ShapeDtypeStruct((BLK, D_MODEL), F32),
                   jax.ShapeDtypeStruct(wup_f32.shape[1:], BF16),
                   jax.ShapeDtypeStruct(wdn_f32.shape[1:], BF16)],
        scratch_shapes=[grouped(D_MODEL, F32)] + proj_bufs + proj_bufs
        + [grouped(ATT_KV, BF16)] * 4
        + [pltpu.VMEM((N_GROUPS, GLA_DK, GLA_DV_HEAD), F32)] * 2,
        compiler_params=_params(),
        name="mixer",
    )(sinks, x_groups, meta_block, mix_norm, w_cat, b_cat, wal_hi, wal_lo, b_alpha, cos2, sin2,
      tri, sel, lvl, head_norm, wpa, wpg, wo, wup_f32, wdn_f32)


def _ffn_kernel(tm, n_chunks, tiles_per_seq, h_ref, halo_ref, halo_meta_ref, g_ref, wup_ref, cw_ref,
                cb_ref, wdn_ref, gf_ref, out_ref, u_scr):
    def normed(x):
        var = jnp.mean(x * x, axis=-1, keepdims=True)
        return (x * lax.rsqrt(var + EPS) * g_ref[...]).astype(BF16)

    h = h_ref[...]
    seq_start = lax.rem(pl.program_id(0), tiles_per_seq) == 0
    u_scr[0:HALO, :] = normed(jnp.where(seq_start, halo_meta_ref[...], halo_ref[...]))
    u_scr[HALO:, :] = normed(h)
    u = u_scr[...]

    fc = D_FF // n_chunks
    acc = None
    for c in range(n_chunks):
        a = _dot(u, wup_ref[:, c * fc:(c + 1) * fc])
        v = _dot(u[HALO:], wup_ref[:, D_FF + c * fc:D_FF + (c + 1) * fc])
        cw = cw_ref[:, c * fc:(c + 1) * fc]
        conv = (cw[0:1] * pltpu.roll(a, 2, 0) + cw[1:2] * pltpu.roll(a, 1, 0) + cw[2:3] * a
                + cb_ref[:, c * fc:(c + 1) * fc])[HALO:]
        inner = 0.7978845608028654 * (conv + 0.044715 * (conv * conv * conv))
        act = conv * (0.5 * (1.0 + jnp.tanh(inner)))
        part = _dot((act * v).astype(BF16), wdn_ref[c * fc:(c + 1) * fc, :])
        acc = part if acc is None else acc + part

    h2 = h + acc
    var = jnp.mean(h2 * h2, axis=-1, keepdims=True)
    out_ref[...] = h2 * lax.rsqrt(var + EPS) * gf_ref[...]


def _ffn(h_mid, h_meta, ffn_norm, wup, conv_w, conv_b, wdn, final_norm, seq, tm, n_chunks):
    rows_x = h_mid.shape[0]
    assert seq % tm == 0 and rows_x % seq == 0
    tiles_per_seq = seq // tm
    const = lambda i: (0, 0)
    row = lambda i: (i, 0)
    return pl.pallas_call(
        functools.partial(_ffn_kernel, tm, n_chunks, tiles_per_seq),
        grid=(rows_x // tm,),
        in_specs=[
            pl.BlockSpec((tm, D_MODEL), row),
            pl.BlockSpec((HALO, D_MODEL), lambda i: (jnp.maximum(i * (tm // HALO) - 1, 0), 0)),
            pl.BlockSpec((HALO, D_MODEL), lambda i: (BLK // HALO - 1, 0)),
            pl.BlockSpec((1, D_MODEL), const),
            pl.BlockSpec((D_MODEL, 2 * D_FF), const, pipeline_mode=pl.Buffered(1)),
            pl.BlockSpec((8, D_FF), const),
            pl.BlockSpec((1, D_FF), const),
            pl.BlockSpec((D_FF, D_MODEL), const, pipeline_mode=pl.Buffered(1)),
            pl.BlockSpec((1, D_MODEL), const),
        ],
        out_specs=pl.BlockSpec((tm, D_MODEL), row),
        out_shape=jax.ShapeDtypeStruct((rows_x, D_MODEL), F32),
        scratch_shapes=[pltpu.VMEM((HALO + tm, D_MODEL), BF16)],
        compiler_params=_params(),
        name="conv_ffn",
    )(h_mid, h_mid, h_meta, ffn_norm, wup, conv_w, conv_b, wdn, final_norm)


def _rope_tables(nb):
    half = HEAD_DIM // 2
    inv_freq = ROPE_THETA ** (-jnp.arange(half, dtype=F32) / half)
    pos = (jnp.arange((nb + 1) * BLK) - META_PAD).astype(F32)
    ang = pos[:, None] * inv_freq[None, :]
    cos2 = jnp.tile(jnp.cos(ang), (1, LANES // half))
    sin = jnp.sin(ang)
    sin2 = jnp.tile(jnp.concatenate([-sin, sin], axis=-1), (1, LANES // HEAD_DIM))
    return cos2, sin2


def kernel(x, meta_tokens, mix_norm, w_in, b_in, w_alpha, b_alpha, attn_sinks, gla_head_norm,
           w_proj_attn, w_proj_gla, w_out, ffn_norm, w_up, conv_w, conv_b, w_down, final_norm):
    batch, seq, _ = x.shape
    assert batch % N_GROUPS == 0 and seq % BLK == 0
    rows_x = batch * seq
    nb = seq // BLK
    l = 0

    meta_block = jnp.concatenate(
        [jnp.zeros((META_PAD, D_MODEL), x.dtype), meta_tokens.astype(x.dtype)], axis=0)

    lo0 = LOW0
    bi = b_in[l]
    w_cat = _prep_w(w_in, l)
    b_cat = jnp.concatenate(
        [bi[:lo0], bi[lo0 + GLA_RANK:], bi[lo0:lo0 + GLA_RANK],
         jnp.zeros((C_LOW - GLA_RANK,), bi.dtype)])[None, :]
    wal = jnp.concatenate(
        [w_alpha[l], jnp.zeros((C_LOW - GLA_RANK, GLA_DK), w_alpha.dtype)], axis=0)
    wal_hi = wal.astype(BF16)
    wal_lo = (wal - wal_hi.astype(F32)).astype(BF16)

    cos2, sin2 = _rope_tables(nb)
    x_groups = x.reshape(N_GROUPS, rows_x // N_GROUPS, D_MODEL)
    h_mid, h_meta, wup, wdn = _mixer(
        x_groups, meta_block, mix_norm[l][None, :], w_cat, b_cat, wal_hi, wal_lo,
        b_alpha[l][None, :], attn_sinks[l], cos2, sin2, gla_head_norm[l][None, :],
        w_proj_attn[l].astype(BF16), w_proj_gla[l].astype(BF16), w_out[l].astype(BF16),
        w_up, w_down, l, nb)

    cw = jnp.concatenate([conv_w[l], jnp.zeros((8 - conv_w.shape[1], D_FF), conv_w.dtype)], axis=0)
    out = _ffn(h_mid.reshape(rows_x, D_MODEL), h_meta, ffn_norm[l][None, :], wup,
               cw, conv_b[l][None, :], wdn, final_norm[None, :], seq,
               tm=1024, n_chunks=1)
    return out.reshape(batch, seq, D_MODEL)
```

```python
import functools

import numpy as np
import jax
import jax.numpy as jnp
from jax import lax
from jax.experimental import pallas as pl
from jax.experimental.pallas import tpu as pltpu

D_MODEL = 1024
N_META = 16
EPS = 1e-6
ATT_HEADS = 8
ATT_KV_HEADS = 2
HEAD_DIM = 64
BLK = 128
ROPE_THETA = 10000.0
GLA_HEADS = 4
GLA_DK = 256
GLA_DV = 512
GLA_DK_HEAD = GLA_DK // GLA_HEADS
GLA_DV_HEAD = GLA_DV // GLA_HEADS
GLA_RANK = 16
GLA_TAU = 16.0
D_FF = 2816
ATT_Q = ATT_HEADS * HEAD_DIM
ATT_KV = ATT_KV_HEADS * HEAD_DIM

META_PAD = BLK - N_META
LANES = 128
HALO = 16
NEG = -1e30
VMEM_LIMIT = 56 * 1024 * 1024

C_QKV = ATT_Q + 2 * ATT_KV
C_GLA = 2 * GLA_DK + GLA_DV
C_GR = GLA_DV
C_GATE = 2 * D_MODEL
C_LOW = LANES
O_QKV = 0
O_GLA = O_QKV + C_QKV
O_GR = O_GLA + C_GLA
O_GATE = O_GR + C_GR
O_LOW = O_GATE + C_GATE
C_ALL = O_LOW + C_LOW
PROJ_CHUNK = 256

GLA_LEVELS = (64, 32, 16, 8, 4, 2, 1)
N_LEVELS = len(GLA_LEVELS)
GLA_FINE = tuple(h for h in GLA_LEVELS if h < 8)

N_GROUPS = 2
MIXER_STARTS = (4, 0, 12, 8)

BF16 = jnp.bfloat16
F32 = jnp.float32


def _sigmoid(x):
    return 1.0 / (1.0 + jnp.exp(-x))


def _dot(a, b):
    return jnp.dot(a, b, preferred_element_type=F32)


def _dot_nt(a, b):
    return lax.dot_general(a, b, (((1,), (1,)), ((), ())), preferred_element_type=F32)


def _split3(x):
    hi = x.astype(BF16)
    r1 = x - hi.astype(F32)
    mid = r1.astype(BF16)
    lo = (r1 - mid.astype(F32)).astype(BF16)
    return hi, mid, lo


def _params():
    return pltpu.CompilerParams(dimension_semantics=("arbitrary",), vmem_limit_bytes=VMEM_LIMIT)


LOW0 = ATT_Q + 2 * ATT_KV + 2 * GLA_DK + 2 * GLA_DV
D_IN = LOW0 + GLA_RANK + C_GATE


def _prep_w_kernel(wt_ref, out_ref):
    for t in range(C_ALL // LANES):
        col = t * LANES
        if col < LOW0:
            rows = wt_ref[col:col + LANES, :]
        elif col < O_LOW:
            src = LOW0 + GLA_RANK + (col - O_GATE)
            rows = wt_ref[src:src + LANES, :]
        else:
            rows = jnp.concatenate([wt_ref[LOW0:LOW0 + GLA_RANK, :],
                                    jnp.zeros((C_LOW - GLA_RANK, D_MODEL), F32)], axis=0)
        out_ref[:, col:col + LANES] = rows.T.astype(BF16)


def _prep_w(w, layer):
    wt = jnp.swapaxes(w, 1, 2)
    return pl.pallas_call(
        _prep_w_kernel,
        grid=(1,),
        in_specs=[pl.BlockSpec((None, D_IN, D_MODEL), lambda i: (layer, 0, 0),
                               pipeline_mode=pl.Buffered(1))],
        out_specs=pl.BlockSpec((D_MODEL, C_ALL), lambda i: (0, 0)),
        out_shape=jax.ShapeDtypeStruct((D_MODEL, C_ALL), BF16),
        compiler_params=_params(),
        name="prep_w_in",
    )(wt)


def _inproj_stages(x, weights, store):
    g_ref, w_ref, b_ref, wal_hi_ref, wal_lo_ref, bal_ref = weights
    var = jnp.mean(x * x, axis=-1, keepdims=True)
    u = (x * lax.rsqrt(var + EPS) * g_ref[...]).astype(BF16)
    yield

    def proj(off, width):
        return _dot(u, w_ref[:, off:off + width]) + b_ref[:, off:off + width]

    low = proj(O_LOW, C_LOW)
    yield
    silu = lambda y: y * _sigmoid(y)
    plan = ((0, O_QKV, C_QKV, None), (1, O_GLA, C_GLA, None), (2, O_GR, C_GR, silu),
            (3, O_GATE, C_GATE, _sigmoid))
    low_pending = True
    for i, off, width, act in plan:
        for c in range(0, width, PROJ_CHUNK):
            y = proj(off + c, PROJ_CHUNK)
            store(i, c, c + PROJ_CHUNK, (y if act is None else act(y)).astype(BF16))
            yield
            if low_pending:
                low_pending = False
                low_hi = low.astype(BF16)
                low_lo = (low - low_hi.astype(F32)).astype(BF16)
                z = (_dot(low_hi, wal_hi_ref[...]) + _dot(low_lo, wal_hi_ref[...])
                     + _dot(low_hi, wal_lo_ref[...]) + bal_ref[...])
                log_sig = jnp.minimum(z, 0.0) - jnp.log1p(jnp.exp(-jnp.abs(z)))
                store(4, 0, GLA_DK, log_sig * (1.0 / GLA_TAU))
                yield


def _attn_stages(result, is_meta, first_x, sinks_ref, qkv_ref, cos_ref, sin_ref,
                 kprev, vprev, kmeta, vmeta):
    cos = cos_ref[...]
    sin = sin_ref[...]
    lane = lax.broadcasted_iota(jnp.int32, (BLK, LANES), 1)
    first_half = (lane & (HEAD_DIM - 1)) < HEAD_DIM // 2

    def rope(xc):
        rot = jnp.where(first_half, pltpu.roll(xc, LANES - HEAD_DIM // 2, 1),
                        pltpu.roll(xc, HEAD_DIM // 2, 1))
        return xc * cos + rot * sin

    k_cur = rope(qkv_ref[:, ATT_Q:ATT_Q + ATT_KV].astype(F32)).astype(BF16)
    v_cur = qkv_ref[:, ATT_Q + ATT_KV:ATT_Q + 2 * ATT_KV]

    q_rows = []
    for hg in range(ATT_HEADS):
        chunk, off, grp = hg // 2, hg % 2, hg // (ATT_HEADS // ATT_KV_HEADS)
        qc = rope(qkv_ref[:, chunk * LANES:(chunk + 1) * LANES].astype(F32))
        if off != grp:
            qc = pltpu.roll(qc, HEAD_DIM, 1)
        on_group = (lane >= grp * HEAD_DIM) & (lane < (grp + 1) * HEAD_DIM)
        q_rows.append(jnp.where(on_group, qc * (HEAD_DIM ** -0.5), 0.0).astype(BF16))
    q_all = jnp.concatenate(q_rows, axis=0)

    k_prev = jnp.where(first_x, kmeta[...], kprev[...])
    v_prev = jnp.where(first_x, vmeta[...], vprev[...])
    kk = jnp.concatenate([k_prev, k_cur], axis=0)
    vv = jnp.concatenate([v_prev, v_cur], axis=0)
    kprev[...] = k_cur
    vprev[...] = v_cur
    kmeta[...] = jnp.where(is_meta, k_cur, kmeta[...])
    vmeta[...] = jnp.where(is_meta, v_cur, vmeta[...])
    yield

    sc = _dot_nt(q_all, kk).reshape(ATT_HEADS, BLK, 2 * BLK)
    yield

    rq = lax.broadcasted_iota(jnp.int32, (BLK, BLK), 0)
    ck = lax.broadcasted_iota(jnp.int32, (BLK, BLK), 1)
    from_cur = ck <= rq
    cur_min = jnp.where(is_meta, META_PAD, 0)
    prev_min = jnp.where(is_meta, BLK, jnp.where(first_x, META_PAD, 0))
    valid = ck >= jnp.where(from_cur, cur_min, prev_min)
    sc = jnp.where(from_cur[None], sc[:, :, BLK:], sc[:, :, :BLK])
    sc = jnp.where(valid[None], sc, NEG)

    hid = lax.broadcasted_iota(jnp.int32, (ATT_HEADS, 1, 1), 0)
    sink = jnp.zeros((ATT_HEADS, 1, 1), F32)
    for hg in range(ATT_HEADS):
        sink = jnp.where(hid == hg, sinks_ref[hg], sink)

    m = jnp.maximum(jnp.max(sc, axis=-1, keepdims=True), sink)
    p = jnp.exp(sc - m)
    denom = jnp.sum(p, axis=-1, keepdims=True) + jnp.exp(sink - m)
    p = jnp.concatenate([jnp.where(from_cur[None], 0.0, p), jnp.where(from_cur[None], p, 0.0)],
                        axis=-1)
    p = p.astype(BF16).reshape(ATT_HEADS * BLK, 2 * BLK)
    yield

    o = _dot(p, vv)
    yield

    o = o.reshape(ATT_HEADS, BLK, LANES) * (1.0 / denom)
    chunks = []
    for chunk in range(ATT_HEADS // 2):
        grp = (2 * chunk) // (ATT_HEADS // ATT_KV_HEADS)
        o0, o1 = o[2 * chunk], o[2 * chunk + 1]
        if grp != 0:
            o0 = pltpu.roll(o0, HEAD_DIM, 1)
        if grp != 1:
            o1 = pltpu.roll(o1, HEAD_DIM, 1)
        chunks.append(jnp.where(lane < HEAD_DIM, o0, o1).astype(BF16))
    result.append(jnp.concatenate(chunks, axis=1))


def _gla_tables():
    t = np.arange(BLK)
    tri = (t[:, None] >= t[None, :]).astype(np.float32)
    sel = np.zeros((len(GLA_FINE) * BLK, BLK), np.float32)
    for li, half in enumerate(GLA_FINE):
        boundary = (t & ~(2 * half - 1)) + half - 1
        sel[li * BLK + t, boundary] = 1.0
    x = t[:, None] ^ t[None, :]
    top = np.floor(np.log2(np.maximum(x, 1))).astype(np.int64)
    lvl = np.array([GLA_LEVELS.index(1 << int(v)) for v in top.ravel()]).reshape(BLK, BLK)
    lvl = np.where(t[:, None] == t[None, :], N_LEVELS, lvl)
    lvl = np.where(t[:, None] < t[None, :], N_LEVELS + 1, lvl)
    lvl = np.tile(lvl, (1, GLA_HEADS))
    return (jnp.asarray(tri, BF16), jnp.asarray(sel, BF16), jnp.asarray(lvl, jnp.int32))


def _gla_stages(result, is_meta, first_x, gla_ref, loga_ref, gr_ref, tri_ref, sel_ref, lvl_ref,
                hn_ref, state, state_meta):
    row = lax.broadcasted_iota(jnp.int32, (BLK, GLA_DK), 0)
    lane = lax.broadcasted_iota(jnp.int32, (BLK, GLA_DK), 1)
    is_token = row >= jnp.where(is_meta, META_PAD, 0)

    g3 = jnp.concatenate(_split3(loga_ref[...]), axis=1)
    yield
    b3 = _dot(tri_ref[...], g3)
    yield
    b = b3[:, :GLA_DK] + b3[:, GLA_DK:2 * GLA_DK] + b3[:, 2 * GLA_DK:]
    b_last = b[BLK - 1:BLK, :]

    b_hi = b.astype(BF16)
    b_mid = (b - b_hi.astype(F32)).astype(BF16)
    fine = _dot(sel_ref[...], jnp.concatenate([b_hi, b_mid], axis=1))

    qf = gla_ref[:, :GLA_DK].astype(F32) * (GLA_DK_HEAD ** -0.5)
    kf = jnp.where(is_token, gla_ref[:, GLA_DK:2 * GLA_DK].astype(F32), 0.0)

    head_mask = [jnp.where((lane >= h * GLA_DK_HEAD) & (lane < (h + 1) * GLA_DK_HEAD), 1.0, 0.0)
                 .astype(BF16) for h in range(GLA_HEADS)]

    def stack_heads(x):
        xb = x.astype(BF16)
        return jnp.concatenate([xb * hm for hm in head_mask], axis=0)

    state_in = jnp.where(first_x, state_meta[...], state[...])
    q_in = stack_heads(qf * jnp.exp(b))
    q_diag, k_diag = qf.astype(BF16), stack_heads(kf)
    yield
    o_inter = _dot(q_in, state_in.astype(BF16))
    lvl = lvl_ref[...]
    a = jnp.where(lvl == N_LEVELS, _dot_nt(q_diag, k_diag), 0.0)
    zeros = lambda shape: jnp.zeros(shape, F32)
    for li, half in enumerate(GLA_LEVELS):
        if half in GLA_FINE:
            fi = GLA_FINE.index(half)
            r = fine[fi * BLK:(fi + 1) * BLK, :GLA_DK] + fine[fi * BLK:(fi + 1) * BLK, GLA_DK:]
            upper = (row & half) != 0
            t_l = jnp.where(upper, qf, kf) * jnp.exp(-jnp.abs(b - r))
            q_l = jnp.where(upper, t_l, 0.0)
            k_l = jnp.where(upper, 0.0, t_l)
        else:
            grp = (BLK // (2 * half), 2 * half, GLA_DK)
            b_g, q_g, k_g = b.reshape(grp), qf.reshape(grp), kf.reshape(grp)
            r = b_g[:, half - 1:half, :]
            q_up = q_g[:, half:, :] * jnp.exp(b_g[:, half:, :] - r)
            k_lo = k_g[:, :half, :] * jnp.exp(r - b_g[:, :half, :])
            q_l = jnp.concatenate([zeros(q_up.shape), q_up], axis=1).reshape(BLK, GLA_DK)
            k_l = jnp.concatenate([k_lo, zeros(k_lo.shape)], axis=1).reshape(BLK, GLA_DK)
        q_l, k_l = q_l.astype(BF16), stack_heads(k_l)
        yield
        a = jnp.where(lvl == li, _dot_nt(q_l, k_l), a)
    a = a.astype(BF16)

    k_out_t = (kf * jnp.exp(b_last - b)).T.astype(BF16)
    decay_t = jnp.broadcast_to(jnp.exp(b_last), (BLK, GLA_DK)).T
    yield

    hn = hn_ref[...]
    new_state, heads = [], []
    for h in range(GLA_HEADS):
        v_h = gla_ref[:, 2 * GLA_DK + h * GLA_DV_HEAD:2 * GLA_DK + (h + 1) * GLA_DV_HEAD]
        o_h = o_inter[h * BLK:(h + 1) * BLK, :] + _dot(a[:, h * BLK:(h + 1) * BLK], v_h)
        new_state.append(_dot(k_out_t[h * GLA_DK_HEAD:(h + 1) * GLA_DK_HEAD, :], v_h))
        heads.append(o_h)
    yield
    for h in range(GLA_HEADS):
        o_h = heads[h]
        var = jnp.mean(o_h * o_h, axis=-1, keepdims=True)
        o_h = o_h * lax.rsqrt(var + EPS) * hn
        gate = gr_ref[:, h * GLA_DV_HEAD:(h + 1) * GLA_DV_HEAD].astype(F32)
        heads[h] = (o_h * gate).astype(BF16)
    state_new = state_in * decay_t + jnp.concatenate(new_state, axis=0)
    state[...] = state_new
    state_meta[...] = jnp.where(is_meta, state_new, state_meta[...])
    result.append(jnp.concatenate(heads, axis=1))


def _merge_stages(result, y_att, y_gla, gate, h, wpa_ref, wpg_ref, wo_ref):
    pa = _dot(y_att, wpa_ref[...])
    pg = _dot(y_gla, wpg_ref[...])
    yield
    mixed = (gate[:, :D_MODEL].astype(F32) * pa + gate[:, D_MODEL:].astype(F32) * pg).astype(BF16)
    yield
    result.append(h + _dot(mixed, wo_ref[...]))


def _advance(gens):
    alive = []
    for gen in gens:
        try:
            next(gen)
            alive.append(gen)
        except StopIteration:
            pass
    return alive


def _mixer_kernel(nb, sinks_ref, xnext_ref, meta_ref, g_ref, w_ref, b_ref, wal_hi_ref, wal_lo_ref,
                  bal_ref, cos_ref, sin_ref, tri_ref, sel_ref, lvl_ref, hn_ref, wpa_ref, wpg_ref,
                  wo_ref, wup_f32_ref, wdn_f32_ref, out_ref, outm_ref, wup_bf16_ref, wdn_bf16_ref,
                  cur_h, cur_qkv, cur_gla, cur_gr, cur_gate, cur_loga,
                  nxt_qkv, nxt_gla, nxt_gr, nxt_gate, nxt_loga,
                  kprev, vprev, kmeta, vmeta, state, state_meta):
    s = pl.program_id(0)
    is_meta = s == 0
    first_x = jnp.logical_and(s >= 1, lax.rem(jnp.maximum(s - 1, 0), nb) == 0)
    cur = (cur_qkv, cur_gla, cur_gr, cur_gate, cur_loga)
    nxt = (nxt_qkv, nxt_gla, nxt_gr, nxt_gate, nxt_loga)
    weights = (g_ref, w_ref, b_ref, wal_hi_ref, wal_lo_ref, bal_ref)
    groups = range(N_GROUPS)

    @pl.when(is_meta)
    def _():
        for ref in (kprev, vprev, kmeta, vmeta, state, state_meta):
            ref[...] = jnp.zeros_like(ref)

        def store_all_groups(i, c0, c1, val):
            for g in groups:
                cur[i][g, :, c0:c1] = val

        for g in groups:
            cur_h[g] = meta_ref[...]
        meta_proj = [_inproj_stages(meta_ref[...], weights, store_all_groups)]
        while meta_proj:
            meta_proj = _advance(meta_proj)

    def store_next(i, c0, c1, val):
        nxt[i][:, :, c0:c1] = val.reshape(N_GROUPS, BLK, c1 - c0)

    x_next = xnext_ref[...]
    inproj = [_inproj_stages(x_next.reshape(N_GROUPS * BLK, D_MODEL), weights, store_next)]

    y_att, y_gla, h_mid = [], [], []
    mixers = []
    for g in groups:
        mixers.append(_attn_stages(y_att, is_meta, first_x, sinks_ref, cur_qkv.at[g], cos_ref,
                                   sin_ref, kprev.at[g], vprev.at[g], kmeta.at[g], vmeta.at[g]))
        mixers.append(_gla_stages(y_gla, is_meta, first_x, cur_gla.at[g], cur_loga.at[g],
                                  cur_gr.at[g], tri_ref, sel_ref, lvl_ref, hn_ref,
                                  state.at[g], state_meta.at[g]))
    starts = sorted(zip(MIXER_STARTS, range(len(mixers))))
    active, rounds = [], 0
    while active or starts:
        while starts and starts[0][0] <= rounds:
            active.append(mixers[starts.pop(0)[1]])
        inproj = _advance(inproj)
        active = _advance(active)
        rounds += 1

    gate = cur_gate[...].reshape(N_GROUPS * BLK, C_GATE)
    h = cur_h[...].reshape(N_GROUPS * BLK, D_MODEL)
    merge = [_merge_stages(h_mid, jnp.concatenate(y_att, axis=0), jnp.concatenate(y_gla, axis=0),
                           gate, h, wpa_ref, wpg_ref, wo_ref)]
    while merge or inproj:
        merge = _advance(merge)
        inproj = _advance(inproj)

    h_mid = h_mid[0].reshape(N_GROUPS, BLK, D_MODEL)
    out_ref[...] = h_mid

    wup_bf16_ref[...] = wup_f32_ref[...].astype(BF16)
    wdn_bf16_ref[...] = wdn_f32_ref[...].astype(BF16)

    cur_h[...] = x_next
    for c_ref, n_ref in zip(cur, nxt):
        c_ref[...] = n_ref[...]

    @pl.when(is_meta)
    def _():
        outm_ref[...] = out_ref[0]


def _mixer(x_groups, meta_block, mix_norm, w_cat, b_cat, wal_hi, wal_lo, b_alpha, sinks, cos2, sin2,
           head_norm, wpa, wpg, wo, wup_f32, wdn_f32, layer, nb):
    rows_g = x_groups.shape[1]
    nblocks_g = rows_g // BLK

    def slab(w, rows, out):
        n = w.shape[1] // rows
        assert w.shape[1] % rows == 0 and n <= nblocks_g
        idx = lambda s: jnp.minimum(jnp.maximum(s - 1, 0), n - 1)
        if out:
            return pl.BlockSpec((rows, w.shape[2]), lambda s: (idx(s), 0))
        return pl.BlockSpec((None, rows, w.shape[2]), lambda s: (layer, idx(s), 0))

    wup_rows, wdn_rows = 16, 64
    tri, sel, lvl = _gla_tables()
    const = lambda s: (0, 0)
    xnext = lambda s: (0, jnp.minimum(s, nblocks_g - 1), 0)
    xout = lambda s: (0, jnp.maximum(s - 1, 0), 0)
    pos = lambda s: (jnp.where(s == 0, 0, lax.rem(jnp.maximum(s - 1, 0), nb) + 1), 0)
    grouped = lambda cols, dt: pltpu.VMEM((N_GROUPS, BLK, cols), dt)
    proj_bufs = [grouped(C_QKV, BF16), grouped(C_GLA, BF16), grouped(C_GR, BF16),
                 grouped(C_GATE, BF16), grouped(GLA_DK, F32)]
    return pl.pallas_call(
        functools.partial(_mixer_kernel, nb),
        grid=(nblocks_g + 1,),
        in_specs=[
            pl.BlockSpec(memory_space=pltpu.SMEM),
            pl.BlockSpec((N_GROUPS, BLK, D_MODEL), xnext),
            pl.BlockSpec((BLK, D_MODEL), const),
            pl.BlockSpec((1, D_MODEL), const),
            pl.BlockSpec((D_MODEL, C_ALL), const),
            pl.BlockSpec((1, C_ALL), const),
            pl.BlockSpec((LANES, GLA_DK), const),
            pl.BlockSpec((LANES, GLA_DK), const),
            pl.BlockSpec((1, GLA_DK), const),
            pl.BlockSpec((BLK, LANES), pos),
            pl.BlockSpec((BLK, LANES), pos),
            pl.BlockSpec((BLK, BLK), const),
            pl.BlockSpec((len(GLA_FINE) * BLK, BLK), const),
            pl.BlockSpec((BLK, GLA_HEADS * BLK), const),
            pl.BlockSpec((1, GLA_DV_HEAD), const),
            pl.BlockSpec((ATT_Q, D_MODEL), const),
            pl.BlockSpec((GLA_DV, D_MODEL), const),
            pl.BlockSpec((D_MODEL, D_MODEL), const),
            slab(wup_f32, wup_rows, False),
            slab(wdn_f32, wdn_rows, False),
        ],
        out_specs=[pl.BlockSpec((N_GROUPS, BLK, D_MODEL), xout),
                   pl.BlockSpec((BLK, D_MODEL), const),
                   slab(wup_f32, wup_rows, True), slab(wdn_f32, wdn_rows, True)],
        out_shape=[jax.ShapeDtypeStruct((N_GROUPS, rows_g, D_MODEL), F32),
                   jax.ShapeDtypeStruct((BLK, D_MODEL), F32),
                   jax.ShapeDtypeStruct(wup_f32.shape[1:], BF16),
                   jax.ShapeDtypeStruct(wdn_f32.shape[1:], BF16)],
        scratch_shapes=[grouped(D_MODEL, F32)] + proj_bufs + proj_bufs
        + [grouped(ATT_KV, BF16)] * 4
        + [pltpu.VMEM((N_GROUPS, GLA_DK, GLA_DV_HEAD), F32)] * 2,
        compiler_params=_params(),
        name="mixer",
    )(sinks, x_groups, meta_block, mix_norm, w_cat, b_cat, wal_hi, wal_lo, b_alpha, cos2, sin2,
      tri, sel, lvl, head_norm, wpa, wpg, wo, wup_f32, wdn_f32)


def _ffn_kernel(tm, n_chunks, tiles_per_seq, h_ref, halo_ref, halo_meta_ref, g_ref, wup_ref, cw_ref,
                cb_ref, wdn_ref, gf_ref, out_ref, u_scr):
    def normed(x):
        var = jnp.mean(x * x, axis=-1, keepdims=True)
        return (x * lax.rsqrt(var + EPS) * g_ref[...]).astype(BF16)

    h = h_ref[...]
    seq_start = lax.rem(pl.program_id(0), tiles_per_seq) == 0
    u_scr[0:HALO, :] = normed(jnp.where(seq_start, halo_meta_ref[...], halo_ref[...]))
    u_scr[HALO:, :] = normed(h)
    u = u_scr[...]

    fc = D_FF // n_chunks
    acc = None
    for c in range(n_chunks):
        a = _dot(u, wup_ref[:, c * fc:(c + 1) * fc])
        v = _dot(u[HALO:], wup_ref[:, D_FF + c * fc:D_FF + (c + 1) * fc])
        cw = cw_ref[:, c * fc:(c + 1) * fc]
        conv = (cw[0:1] * pltpu.roll(a, 2, 0) + cw[1:2] * pltpu.roll(a, 1, 0) + cw[2:3] * a
                + cb_ref[:, c * fc:(c + 1) * fc])[HALO:]
        inner = 0.7978845608028654 * (conv + 0.044715 * (conv * conv * conv))
        act = conv * (0.5 * (1.0 + jnp.tanh(inner)))
        part = _dot((act * v).astype(BF16), wdn_ref[c * fc:(c + 1) * fc, :])
        acc = part if acc is None else acc + part

    h2 = h + acc
    var = jnp.mean(h2 * h2, axis=-1, keepdims=True)
    out_ref[...] = h2 * lax.rsqrt(var + EPS) * gf_ref[...]


def _ffn(h_mid, h_meta, ffn_norm, wup, conv_w, conv_b, wdn, final_norm, seq, tm, n_chunks):
    rows_x = h_mid.shape[0]
    assert seq % tm == 0 and rows_x % seq == 0
    tiles_per_seq = seq // tm
    const = lambda i: (0, 0)
    row = lambda i: (i, 0)
    return pl.pallas_call(
        functools.partial(_ffn_kernel, tm, n_chunks, tiles_per_seq),
        grid=(rows_x // tm,),
        in_specs=[
            pl.BlockSpec((tm, D_MODEL), row),
            pl.BlockSpec((HALO, D_MODEL), lambda i: (jnp.maximum(i * (tm // HALO) - 1, 0), 0)),
            pl.BlockSpec((HALO, D_MODEL), lambda i: (BLK // HALO - 1, 0)),
            pl.BlockSpec((1, D_MODEL), const),
            pl.BlockSpec((D_MODEL, 2 * D_FF), const, pipeline_mode=pl.Buffered(1)),
            pl.BlockSpec((8, D_FF), const),
            pl.BlockSpec((1, D_FF), const),
            pl.BlockSpec((D_FF, D_MODEL), const, pipeline_mode=pl.Buffered(1)),
            pl.BlockSpec((1, D_MODEL), const),
        ],
        out_specs=pl.BlockSpec((tm, D_MODEL), row),
        out_shape=jax.ShapeDtypeStruct((rows_x, D_MODEL), F32),
        scratch_shapes=[pltpu.VMEM((HALO + tm, D_MODEL), BF16)],
        compiler_params=_params(),
        name="conv_ffn",
    )(h_mid, h_mid, h_meta, ffn_norm, wup, conv_w, conv_b, wdn, final_norm)


def _rope_tables(nb):
    half = HEAD_DIM // 2
    inv_freq = ROPE_THETA ** (-jnp.arange(half, dtype=F32) / half)
    pos = (jnp.arange((nb + 1) * BLK) - META_PAD).astype(F32)
    ang = pos[:, None] * inv_freq[None, :]
    cos2 = jnp.tile(jnp.cos(ang), (1, LANES // half))
    sin = jnp.sin(ang)
    sin2 = jnp.tile(jnp.concatenate([-sin, sin], axis=-1), (1, LANES // HEAD_DIM))
    return cos2, sin2


def kernel(x, meta_tokens, mix_norm, w_in, b_in, w_alpha, b_alpha, attn_sinks, gla_head_norm,
           w_proj_attn, w_proj_gla, w_out, ffn_norm, w_up, conv_w, conv_b, w_down, final_norm):
    batch, seq, _ = x.shape
    assert batch % N_GROUPS == 0 and seq % BLK == 0
    rows_x = batch * seq
    nb = seq // BLK
    l = 0

    meta_block = jnp.concatenate(
        [jnp.zeros((META_PAD, D_MODEL), x.dtype), meta_tokens.astype(x.dtype)], axis=0)

    lo0 = LOW0
    bi = b_in[l]
    w_cat = _prep_w(w_in, l)
    b_cat = jnp.concatenate(
        [bi[:lo0], bi[lo0 + GLA_RANK:], bi[lo0:lo0 + GLA_RANK],
         jnp.zeros((C_LOW - GLA_RANK,), bi.dtype)])[None, :]
    wal = jnp.concatenate(
        [w_alpha[l], jnp.zeros((C_LOW - GLA_RANK, GLA_DK), w_alpha.dtype)], axis=0)
    wal_hi = wal.astype(BF16)
    wal_lo = (wal - wal_hi.astype(F32)).astype(BF16)

    cos2, sin2 = _rope_tables(nb)
    x_groups = x.reshape(N_GROUPS, rows_x // N_GROUPS, D_MODEL)
    h_mid, h_meta, wup, wdn = _mixer(
        x_groups, meta_block, mix_norm[l][None, :], w_cat, b_cat, wal_hi, wal_lo,
        b_alpha[l][None, :], attn_sinks[l], cos2, sin2, gla_head_norm[l][None, :],
        w_proj_attn[l].astype(BF16), w_proj_gla[l].astype(BF16), w_out[l].astype(BF16),
        w_up, w_down, l, nb)

    cw = jnp.concatenate([conv_w[l], jnp.zeros((8 - conv_w.shape[1], D_FF), conv_w.dtype)], axis=0)
    out = _ffn(h_mid.reshape(rows_x, D_MODEL), h_meta, ffn_norm[l][None, :], wup,
               cw, conv_b[l][None, :], wdn, final_norm[None, :], seq,
               tm=1024, n_chunks=1)
    return out.reshape(batch, seq, D_MODEL)
```

```python
import functools

import numpy as np
import jax
import jax.numpy as jnp
from jax import lax
from jax.experimental import pallas as pl
from jax.experimental.pallas import tpu as pltpu

D_MODEL = 1024
N_META = 16
EPS = 1e-6
ATT_HEADS = 8
ATT_KV_HEADS = 2
HEAD_DIM = 64
BLK = 128
ROPE_THETA = 10000.0
GLA_HEADS = 4
GLA_DK = 256
GLA_DV = 512
GLA_DK_HEAD = GLA_DK // GLA_HEADS
GLA_DV_HEAD = GLA_DV // GLA_HEADS
GLA_RANK = 16
GLA_TAU = 16.0
D_FF = 2816
ATT_Q = ATT_HEADS * HEAD_DIM
ATT_KV = ATT_KV_HEADS * HEAD_DIM

META_PAD = BLK - N_META
LANES = 128
HALO = 16
NEG = -1e30
VMEM_LIMIT = 56 * 1024 * 1024

C_QKV = ATT_Q + 2 * ATT_KV
C_GLA = 2 * GLA_DK + GLA_DV
C_GR = GLA_DV
C_GATE = 2 * D_MODEL
C_LOW = LANES
O_QKV = 0
O_GLA = O_QKV + C_QKV
O_GR = O_GLA + C_GLA
O_GATE = O_GR + C_GR
O_LOW = O_GATE + C_GATE
C_ALL = O_LOW + C_LOW
PROJ_CHUNK = 256

GLA_LEVELS = (64, 32, 16, 8, 4, 2, 1)
N_LEVELS = len(GLA_LEVELS)
GLA_FINE = tuple(h for h in GLA_LEVELS if h < 8)

N_GROUPS = 2
MIXER_STARTS = (4, 0, 12, 8)

BF16 = jnp.bfloat16
F32 = jnp.float32


def _sigmoid(x):
    return 1.0 / (1.0 + jnp.exp(-x))


def _dot(a, b):
    return jnp.dot(a, b, preferred_element_type=F32)


def _dot_nt(a, b):
    return lax.dot_general(a, b, (((1,), (1,)), ((), ())), preferred_element_type=F32)


def _split3(x):
    hi = x.astype(BF16)
    r1 = x - hi.astype(F32)
    mid = r1.astype(BF16)
    lo = (r1 - mid.astype(F32)).astype(BF16)
    return hi, mid, lo


def _params():
    return pltpu.CompilerParams(dimension_semantics=("arbitrary",), vmem_limit_bytes=VMEM_LIMIT)


LOW0 = ATT_Q + 2 * ATT_KV + 2 * GLA_DK + 2 * GLA_DV
D_IN = LOW0 + GLA_RANK + C_GATE


def _regroup_w_in(wt_ref, out_ref):
    for t in range(C_ALL // LANES):
        col = t * LANES
        if col < LOW0:
            rows = wt_ref[col:col + LANES, :]
        elif col < O_LOW:
            src = LOW0 + GLA_RANK + (col - O_GATE)
            rows = wt_ref[src:src + LANES, :]
        else:
            rows = jnp.concatenate([wt_ref[LOW0:LOW0 + GLA_RANK, :],
                                    jnp.zeros((C_LOW - GLA_RANK, D_MODEL), F32)], axis=0)
        out_ref[:, col:col + LANES] = rows.T.astype(BF16)


def _inproj_stages(x, weights, store):
    g_ref, w_ref, b_ref, wal_hi_ref, wal_lo_ref, bal_ref = weights
    var = jnp.mean(x * x, axis=-1, keepdims=True)
    u = (x * lax.rsqrt(var + EPS) * g_ref[...]).astype(BF16)
    yield

    def proj(off, width):
        return _dot(u, w_ref[:, off:off + width]) + b_ref[:, off:off + width]

    low = proj(O_LOW, C_LOW)
    yield
    silu = lambda y: y * _sigmoid(y)
    plan = ((0, O_QKV, C_QKV, None), (1, O_GLA, C_GLA, None), (2, O_GR, C_GR, silu),
            (3, O_GATE, C_GATE, _sigmoid))
    low_pending = True
    for i, off, width, act in plan:
        for c in range(0, width, PROJ_CHUNK):
            y = proj(off + c, PROJ_CHUNK)
            store(i, c, c + PROJ_CHUNK, (y if act is None else act(y)).astype(BF16))
            yield
            if low_pending:
                low_pending = False
                low_hi = low.astype(BF16)
                low_lo = (low - low_hi.astype(F32)).astype(BF16)
                z = (_dot(low_hi, wal_hi_ref[...]) + _dot(low_lo, wal_hi_ref[...])
                     + _dot(low_hi, wal_lo_ref[...]) + bal_ref[...])
                log_sig = jnp.minimum(z, 0.0) - jnp.log1p(jnp.exp(-jnp.abs(z)))
                store(4, 0, GLA_DK, log_sig * (1.0 / GLA_TAU))
                yield


def _attn_stages(result, is_meta, first_x, sinks_ref, qkv_ref, rope_ref,
                 kprev, vprev, kmeta, vmeta):
    cos = rope_ref[:, :LANES]
    sin = rope_ref[:, LANES:]
    lane = lax.broadcasted_iota(jnp.int32, (BLK, LANES), 1)
    first_half = (lane & (HEAD_DIM - 1)) < HEAD_DIM // 2

    def rope(xc):
        rot = jnp.where(first_half, pltpu.roll(xc, LANES - HEAD_DIM // 2, 1),
                        pltpu.roll(xc, HEAD_DIM // 2, 1))
        return xc * cos + rot * sin

    k_cur = rope(qkv_ref[:, ATT_Q:ATT_Q + ATT_KV].astype(F32)).astype(BF16)
    v_cur = qkv_ref[:, ATT_Q + ATT_KV:ATT_Q + 2 * ATT_KV]

    q_rows = []
    for hg in range(ATT_HEADS):
        chunk, off, grp = hg // 2, hg % 2, hg // (ATT_HEADS // ATT_KV_HEADS)
        qc = rope(qkv_ref[:, chunk * LANES:(chunk + 1) * LANES].astype(F32))
        if off != grp:
            qc = pltpu.roll(qc, HEAD_DIM, 1)
        on_group = (lane >= grp * HEAD_DIM) & (lane < (grp + 1) * HEAD_DIM)
        q_rows.append(jnp.where(on_group, qc * (HEAD_DIM ** -0.5), 0.0).astype(BF16))
    q_all = jnp.concatenate(q_rows, axis=0)

    k_prev = jnp.where(first_x, kmeta[...], kprev[...])
    v_prev = jnp.where(first_x, vmeta[...], vprev[...])
    kk = jnp.concatenate([k_prev, k_cur], axis=0)
    vv = jnp.concatenate([v_prev, v_cur], axis=0)
    kprev[...] = k_cur
    vprev[...] = v_cur
    kmeta[...] = jnp.where(is_meta, k_cur, kmeta[...])
    vmeta[...] = jnp.where(is_meta, v_cur, vmeta[...])
    yield

    sc = _dot_nt(q_all, kk).reshape(ATT_HEADS, BLK, 2 * BLK)
    yield

    rq = lax.broadcasted_iota(jnp.int32, (BLK, BLK), 0)
    ck = lax.broadcasted_iota(jnp.int32, (BLK, BLK), 1)
    from_cur = ck <= rq
    cur_min = jnp.where(is_meta, META_PAD, 0)
    prev_min = jnp.where(is_meta, BLK, jnp.where(first_x, META_PAD, 0))
    valid = ck >= jnp.where(from_cur, cur_min, prev_min)
    sc = jnp.where(from_cur[None], sc[:, :, BLK:], sc[:, :, :BLK])
    sc = jnp.where(valid[None], sc, NEG)

    hid = lax.broadcasted_iota(jnp.int32, (ATT_HEADS, 1, 1), 0)
    sink = jnp.zeros((ATT_HEADS, 1, 1), F32)
    for hg in range(ATT_HEADS):
        sink = jnp.where(hid == hg, sinks_ref[hg], sink)

    m = jnp.maximum(jnp.max(sc, axis=-1, keepdims=True), sink)
    p = jnp.exp(sc - m)
    denom = jnp.sum(p, axis=-1, keepdims=True) + jnp.exp(sink - m)
    p = jnp.concatenate([jnp.where(from_cur[None], 0.0, p), jnp.where(from_cur[None], p, 0.0)],
                        axis=-1)
    p = p.astype(BF16).reshape(ATT_HEADS * BLK, 2 * BLK)
    yield

    o = _dot(p, vv)
    yield

    o = o.reshape(ATT_HEADS, BLK, LANES) * (1.0 / denom)
    chunks = []
    for chunk in range(ATT_HEADS // 2):
        grp = (2 * chunk) // (ATT_HEADS // ATT_KV_HEADS)
        o0, o1 = o[2 * chunk], o[2 * chunk + 1]
        if grp != 0:
            o0 = pltpu.roll(o0, HEAD_DIM, 1)
        if grp != 1:
            o1 = pltpu.roll(o1, HEAD_DIM, 1)
        chunks.append(jnp.where(lane < HEAD_DIM, o0, o1).astype(BF16))
    result.append(jnp.concatenate(chunks, axis=1))


def _gla_tables():
    t = np.arange(BLK)
    tri = (t[:, None] >= t[None, :]).astype(np.float32)
    sel = np.zeros((len(GLA_FINE) * BLK, BLK), np.float32)
    for li, half in enumerate(GLA_FINE):
        boundary = (t & ~(2 * half - 1)) + half - 1
        sel[li * BLK + t, boundary] = 1.0
    x = t[:, None] ^ t[None, :]
    top = np.floor(np.log2(np.maximum(x, 1))).astype(np.int64)
    lvl = np.array([GLA_LEVELS.index(1 << int(v)) for v in top.ravel()]).reshape(BLK, BLK)
    lvl = np.where(t[:, None] == t[None, :], N_LEVELS, lvl)
    lvl = np.where(t[:, None] < t[None, :], N_LEVELS + 1, lvl)
    lvl = np.tile(lvl, (1, GLA_HEADS))
    return (jnp.asarray(tri, BF16), jnp.asarray(sel, BF16), jnp.asarray(lvl, jnp.int32))


def _gla_stages(result, is_meta, first_x, gla_ref, loga_ref, gr_ref, tri_ref, sel_ref, lvl_ref,
                hn_ref, state, state_meta):
    row = lax.broadcasted_iota(jnp.int32, (BLK, GLA_DK), 0)
    lane = lax.broadcasted_iota(jnp.int32, (BLK, GLA_DK), 1)
    is_token = row >= jnp.where(is_meta, META_PAD, 0)

    g3 = jnp.concatenate(_split3(loga_ref[...]), axis=1)
    yield
    b3 = _dot(tri_ref[...], g3)
    yield
    b = b3[:, :GLA_DK] + b3[:, GLA_DK:2 * GLA_DK] + b3[:, 2 * GLA_DK:]
    b_last = b[BLK - 1:BLK, :]

    b_hi = b.astype(BF16)
    b_mid = (b - b_hi.astype(F32)).astype(BF16)
    fine = _dot(sel_ref[...], jnp.concatenate([b_hi, b_mid], axis=1))

    qf = gla_ref[:, :GLA_DK].astype(F32) * (GLA_DK_HEAD ** -0.5)
    kf = jnp.where(is_token, gla_ref[:, GLA_DK:2 * GLA_DK].astype(F32), 0.0)

    head_mask = [jnp.where((lane >= h * GLA_DK_HEAD) & (lane < (h + 1) * GLA_DK_HEAD), 1.0, 0.0)
                 .astype(BF16) for h in range(GLA_HEADS)]

    def stack_heads(x):
        xb = x.astype(BF16)
        return jnp.concatenate([xb * hm for hm in head_mask], axis=0)

    state_in = jnp.where(first_x, state_meta[...], state[...])
    q_in = stack_heads(qf * jnp.exp(b))
    q_diag, k_diag = qf.astype(BF16), stack_heads(kf)
    yield
    o_inter = _dot(q_in, state_in.astype(BF16))
    lvl = lvl_ref[...]
    a = jnp.where(lvl == N_LEVELS, _dot_nt(q_diag, k_diag), 0.0)
    zeros = lambda shape: jnp.zeros(shape, F32)
    for li, half in enumerate(GLA_LEVELS):
        if half in GLA_FINE:
            fi = GLA_FINE.index(half)
            r = fine[fi * BLK:(fi + 1) * BLK, :GLA_DK] + fine[fi * BLK:(fi + 1) * BLK, GLA_DK:]
            upper = (row & half) != 0
            t_l = jnp.where(upper, qf, kf) * jnp.exp(-jnp.abs(b - r))
            q_l = jnp.where(upper, t_l, 0.0)
            k_l = jnp.where(upper, 0.0, t_l)
        else:
            grp = (BLK // (2 * half), 2 * half, GLA_DK)
            b_g, q_g, k_g = b.reshape(grp), qf.reshape(grp), kf.reshape(grp)
            r = b_g[:, half - 1:half, :]
            q_up = q_g[:, half:, :] * jnp.exp(b_g[:, half:, :] - r)
            k_lo = k_g[:, :half, :] * jnp.exp(r - b_g[:, :half, :])
            q_l = jnp.concatenate([zeros(q_up.shape), q_up], axis=1).reshape(BLK, GLA_DK)
            k_l = jnp.concatenate([k_lo, zeros(k_lo.shape)], axis=1).reshape(BLK, GLA_DK)
        q_l, k_l = q_l.astype(BF16), stack_heads(k_l)
        yield
        a = jnp.where(lvl == li, _dot_nt(q_l, k_l), a)
    a = a.astype(BF16)

    k_out_t = (kf * jnp.exp(b_last - b)).T.astype(BF16)
    decay_t = jnp.broadcast_to(jnp.exp(b_last), (BLK, GLA_DK)).T
    yield

    hn = hn_ref[...]
    new_state, heads = [], []
    for h in range(GLA_HEADS):
        v_h = gla_ref[:, 2 * GLA_DK + h * GLA_DV_HEAD:2 * GLA_DK + (h + 1) * GLA_DV_HEAD]
        o_h = o_inter[h * BLK:(h + 1) * BLK, :] + _dot(a[:, h * BLK:(h + 1) * BLK], v_h)
        new_state.append(_dot(k_out_t[h * GLA_DK_HEAD:(h + 1) * GLA_DK_HEAD, :], v_h))
        heads.append(o_h)
    yield
    for h in range(GLA_HEADS):
        o_h = heads[h]
        var = jnp.mean(o_h * o_h, axis=-1, keepdims=True)
        o_h = o_h * lax.rsqrt(var + EPS) * hn
        gate = gr_ref[:, h * GLA_DV_HEAD:(h + 1) * GLA_DV_HEAD].astype(F32)
        heads[h] = (o_h * gate).astype(BF16)
    state_new = state_in * decay_t + jnp.concatenate(new_state, axis=0)
    state[...] = state_new
    state_meta[...] = jnp.where(is_meta, state_new, state_meta[...])
    result.append(jnp.concatenate(heads, axis=1))


def _merge_stages(result, y_att, y_gla, gate, h, wpa_ref, wpg_ref, wo_ref):
    pa = _dot(y_att, wpa_ref[...])
    pg = _dot(y_gla, wpg_ref[...])
    yield
    mixed = (gate[:, :D_MODEL].astype(F32) * pa + gate[:, D_MODEL:].astype(F32) * pg).astype(BF16)
    yield
    result.append(h + _dot(mixed, wo_ref[...]))


def _advance(gens):
    alive = []
    for gen in gens:
        try:
            next(gen)
            alive.append(gen)
        except StopIteration:
            pass
    return alive


def _mixer_kernel(nb, sinks_ref, xnext_ref, meta_ref, g_ref, w_in_t_ref, b_ref, wal_hi_ref, wal_lo_ref,
                  bal_ref, rope_ref, tri_ref, sel_ref, lvl_ref, hn_ref, wpa_ref, wpg_ref,
                  wo_ref, wup_f32_ref, wdn_f32_ref, out_ref, outm_ref, wup_bf16_ref, wdn_bf16_ref,
                  w_cat, cur_h, cur_qkv, cur_gla, cur_gr, cur_gate, cur_loga,
                  nxt_qkv, nxt_gla, nxt_gr, nxt_gate, nxt_loga,
                  kprev, vprev, kmeta, vmeta, state, state_meta):
    s = pl.program_id(0)
    is_meta = s == 0
    first_x = jnp.logical_and(s >= 1, lax.rem(jnp.maximum(s - 1, 0), nb) == 0)
    cur = (cur_qkv, cur_gla, cur_gr, cur_gate, cur_loga)
    nxt = (nxt_qkv, nxt_gla, nxt_gr, nxt_gate, nxt_loga)
    weights = (g_ref, w_cat, b_ref, wal_hi_ref, wal_lo_ref, bal_ref)
    groups = range(N_GROUPS)

    @pl.when(is_meta)
    def _():
        for ref in (kprev, vprev, kmeta, vmeta, state, state_meta):
            ref[...] = jnp.zeros_like(ref)
        _regroup_w_in(w_in_t_ref, w_cat)

        def store_all_groups(i, c0, c1, val):
            for g in groups:
                cur[i][g, :, c0:c1] = val

        for g in groups:
            cur_h[g] = meta_ref[...]
        meta_proj = [_inproj_stages(meta_ref[...], weights, store_all_groups)]
        while meta_proj:
            meta_proj = _advance(meta_proj)

    def store_next(i, c0, c1, val):
        nxt[i][:, :, c0:c1] = val.reshape(N_GROUPS, BLK, c1 - c0)

    x_next = xnext_ref[...]
    inproj = [_inproj_stages(x_next.reshape(N_GROUPS * BLK, D_MODEL), weights, store_next)]

    y_att, y_gla, h_mid = [], [], []
    mixers = []
    for g in groups:
        mixers.append(_attn_stages(y_att, is_meta, first_x, sinks_ref, cur_qkv.at[g], rope_ref,
                                   kprev.at[g], vprev.at[g], kmeta.at[g], vmeta.at[g]))
        mixers.append(_gla_stages(y_gla, is_meta, first_x, cur_gla.at[g], cur_loga.at[g],
                                  cur_gr.at[g], tri_ref, sel_ref, lvl_ref, hn_ref,
                                  state.at[g], state_meta.at[g]))
    starts = sorted(zip(MIXER_STARTS, range(len(mixers))))
    active, rounds = [], 0
    while active or starts:
        while starts and starts[0][0] <= rounds:
            active.append(mixers[starts.pop(0)[1]])
        inproj = _advance(inproj)
        active = _advance(active)
        rounds += 1

    gate = cur_gate[...].reshape(N_GROUPS * BLK, C_GATE)
    h = cur_h[...].reshape(N_GROUPS * BLK, D_MODEL)
    merge = [_merge_stages(h_mid, jnp.concatenate(y_att, axis=0), jnp.concatenate(y_gla, axis=0),
                           gate, h, wpa_ref, wpg_ref, wo_ref)]
    while merge or inproj:
        merge = _advance(merge)
        inproj = _advance(inproj)

    h_mid = h_mid[0].reshape(N_GROUPS, BLK, D_MODEL)
    out_ref[...] = h_mid

    wup_bf16_ref[...] = wup_f32_ref[...].astype(BF16)
    wdn_bf16_ref[...] = wdn_f32_ref[...].astype(BF16)

    cur_h[...] = x_next
    for c_ref, n_ref in zip(cur, nxt):
        c_ref[...] = n_ref[...]

    @pl.when(is_meta)
    def _():
        outm_ref[...] = out_ref[0]


def _mixer(x_groups, meta_block, mix_norm, w_in_t, b_cat, wal_hi, wal_lo, b_alpha, sinks, rope,
           head_norm, wpa, wpg, wo, wup_f32, wdn_f32, layer, nb):
    rows_g = x_groups.shape[1]
    nblocks_g = rows_g // BLK

    def slab(w, rows, out):
        n = w.shape[1] // rows
        assert w.shape[1] % rows == 0 and n <= nblocks_g
        idx = lambda s: jnp.minimum(jnp.maximum(s - 1, 0), n - 1)
        if out:
            return pl.BlockSpec((rows, w.shape[2]), lambda s: (idx(s), 0))
        return pl.BlockSpec((None, rows, w.shape[2]), lambda s: (layer, idx(s), 0))

    wup_rows, wdn_rows = 16, 64
    tri, sel, lvl = _gla_tables()
    const = lambda s: (0, 0)
    xnext = lambda s: (0, jnp.minimum(s, nblocks_g - 1), 0)
    xout = lambda s: (0, jnp.maximum(s - 1, 0), 0)
    pos = lambda s: (jnp.where(s == 0, 0, lax.rem(jnp.maximum(s - 1, 0), nb) + 1), 0)
    grouped = lambda cols, dt: pltpu.VMEM((N_GROUPS, BLK, cols), dt)
    proj_bufs = [grouped(C_QKV, BF16), grouped(C_GLA, BF16), grouped(C_GR, BF16),
                 grouped(C_GATE, BF16), grouped(GLA_DK, F32)]
    return pl.pallas_call(
        functools.partial(_mixer_kernel, nb),
        grid=(nblocks_g + 1,),
        in_specs=[
            pl.BlockSpec(memory_space=pltpu.SMEM),
            pl.BlockSpec((N_GROUPS, BLK, D_MODEL), xnext),
            pl.BlockSpec((BLK, D_MODEL), const),
            pl.BlockSpec((1, D_MODEL), const),
            pl.BlockSpec((None, D_IN, D_MODEL), lambda s: (layer, 0, 0),
                         pipeline_mode=pl.Buffered(1)),
            pl.BlockSpec((1, C_ALL), const),
            pl.BlockSpec((LANES, GLA_DK), const),
            pl.BlockSpec((LANES, GLA_DK), const),
            pl.BlockSpec((1, GLA_DK), const),
            pl.BlockSpec((BLK, 2 * LANES), pos),
            pl.BlockSpec((BLK, BLK), const),
            pl.BlockSpec((len(GLA_FINE) * BLK, BLK), const),
            pl.BlockSpec((BLK, GLA_HEADS * BLK), const),
            pl.BlockSpec((1, GLA_DV_HEAD), const),
            pl.BlockSpec((ATT_Q, D_MODEL), const),
            pl.BlockSpec((GLA_DV, D_MODEL), const),
            pl.BlockSpec((D_MODEL, D_MODEL), const),
            slab(wup_f32, wup_rows, False),
            slab(wdn_f32, wdn_rows, False),
        ],
        out_specs=[pl.BlockSpec((N_GROUPS, BLK, D_MODEL), xout),
                   pl.BlockSpec((BLK, D_MODEL), const),
                   slab(wup_f32, wup_rows, True), slab(wdn_f32, wdn_rows, True)],
        out_shape=[jax.ShapeDtypeStruct((N_GROUPS, rows_g, D_MODEL), F32),
                   jax.ShapeDtypeStruct((BLK, D_MODEL), F32),
                   jax.ShapeDtypeStruct(wup_f32.shape[1:], BF16),
                   jax.ShapeDtypeStruct(wdn_f32.shape[1:], BF16)],
        scratch_shapes=[pltpu.VMEM((D_MODEL, C_ALL), BF16), grouped(D_MODEL, F32)]
        + proj_bufs + proj_bufs
        + [grouped(ATT_KV, BF16)] * 4
        + [pltpu.VMEM((N_GROUPS, GLA_DK, GLA_DV_HEAD), F32)] * 2,
        compiler_params=_params(),
        name="mixer",
    )(sinks, x_groups, meta_block, mix_norm, w_in_t, b_cat, wal_hi, wal_lo, b_alpha, rope,
      tri, sel, lvl, head_norm, wpa, wpg, wo, wup_f32, wdn_f32)


def _ffn_kernel(tm, n_chunks, tiles_per_seq, h_ref, halo_ref, halo_meta_ref, g_ref, wup_ref, cw_ref,
                cb_ref, wdn_ref, gf_ref, out_ref, u_scr):
    def normed(x):
        var = jnp.mean(x * x, axis=-1, keepdims=True)
        return (x * lax.rsqrt(var + EPS) * g_ref[...]).astype(BF16)

    h = h_ref[...]
    seq_start = lax.rem(pl.program_id(0), tiles_per_seq) == 0
    u_scr[0:HALO, :] = normed(jnp.where(seq_start, halo_meta_ref[...], halo_ref[...]))
    u_scr[HALO:, :] = normed(h)
    u = u_scr[...]

    fc = D_FF // n_chunks
    acc = None
    for c in range(n_chunks):
        a = _dot(u, wup_ref[:, c * fc:(c + 1) * fc])
        v = _dot(u[HALO:], wup_ref[:, D_FF + c * fc:D_FF + (c + 1) * fc])
        cw = cw_ref[:, c * fc:(c + 1) * fc]
        conv = (cw[0:1] * pltpu.roll(a, 2, 0) + cw[1:2] * pltpu.roll(a, 1, 0) + cw[2:3] * a
                + cb_ref[:, c * fc:(c + 1) * fc])[HALO:]
        inner = 0.7978845608028654 * (conv + 0.044715 * (conv * conv * conv))
        act = conv * (0.5 * (1.0 + jnp.tanh(inner)))
        part = _dot((act * v).astype(BF16), wdn_ref[c * fc:(c + 1) * fc, :])
        acc = part if acc is None else acc + part

    h2 = h + acc
    var = jnp.mean(h2 * h2, axis=-1, keepdims=True)
    out_ref[...] = h2 * lax.rsqrt(var + EPS) * gf_ref[...]


def _ffn(h_mid, h_meta, ffn_norm, wup, conv_w, conv_b, wdn, final_norm, seq, tm, n_chunks):
    rows_x = h_mid.shape[0]
    assert seq % tm == 0 and rows_x % seq == 0
    tiles_per_seq = seq // tm
    const = lambda i: (0, 0)
    row = lambda i: (i, 0)
    return pl.pallas_call(
        functools.partial(_ffn_kernel, tm, n_chunks, tiles_per_seq),
        grid=(rows_x // tm,),
        in_specs=[
            pl.BlockSpec((tm, D_MODEL), row),
            pl.BlockSpec((HALO, D_MODEL), lambda i: (jnp.maximum(i * (tm // HALO) - 1, 0), 0)),
            pl.BlockSpec((HALO, D_MODEL), lambda i: (BLK // HALO - 1, 0)),
            pl.BlockSpec((1, D_MODEL), const),
            pl.BlockSpec((D_MODEL, 2 * D_FF), const, pipeline_mode=pl.Buffered(1)),
            pl.BlockSpec((8, D_FF), const),
            pl.BlockSpec((1, D_FF), const),
            pl.BlockSpec((D_FF, D_MODEL), const, pipeline_mode=pl.Buffered(1)),
            pl.BlockSpec((1, D_MODEL), const),
        ],
        out_specs=pl.BlockSpec((tm, D_MODEL), row),
        out_shape=jax.ShapeDtypeStruct((rows_x, D_MODEL), F32),
        scratch_shapes=[pltpu.VMEM((HALO + tm, D_MODEL), BF16)],
        compiler_params=_params(),
        name="conv_ffn",
    )(h_mid, h_mid, h_meta, ffn_norm, wup, conv_w, conv_b, wdn, final_norm)


def _rope_tables(nb):
    half = HEAD_DIM // 2
    inv_freq = ROPE_THETA ** (-jnp.arange(half, dtype=F32) / half)
    pos = (jnp.arange((nb + 1) * BLK) - META_PAD).astype(F32)
    ang = pos[:, None] * inv_freq[None, :]
    cos2 = jnp.tile(jnp.cos(ang), (1, LANES // half))
    sin = jnp.sin(ang)
    sin2 = jnp.tile(jnp.concatenate([-sin, sin], axis=-1), (1, LANES // HEAD_DIM))
    return jnp.concatenate([cos2, sin2], axis=1)


def kernel(x, meta_tokens, mix_norm, w_in, b_in, w_alpha, b_alpha, attn_sinks, gla_head_norm,
           w_proj_attn, w_proj_gla, w_out, ffn_norm, w_up, conv_w, conv_b, w_down, final_norm):
    batch, seq, _ = x.shape
    assert batch % N_GROUPS == 0 and seq % BLK == 0
    rows_x = batch * seq
    nb = seq // BLK
    l = 0

    meta_block = jnp.concatenate(
        [jnp.zeros((META_PAD, D_MODEL), x.dtype), meta_tokens.astype(x.dtype)], axis=0)

    lo0 = LOW0
    bi = b_in[l]
    w_in_t = jnp.swapaxes(w_in, 1, 2)
    b_cat = jnp.concatenate(
        [bi[:lo0], bi[lo0 + GLA_RANK:], bi[lo0:lo0 + GLA_RANK],
         jnp.zeros((C_LOW - GLA_RANK,), bi.dtype)])[None, :]
    wal = jnp.concatenate(
        [w_alpha[l], jnp.zeros((C_LOW - GLA_RANK, GLA_DK), w_alpha.dtype)], axis=0)
    wal_hi = wal.astype(BF16)
    wal_lo = (wal - wal_hi.astype(F32)).astype(BF16)

    rope = _rope_tables(nb)
    x_groups = x.reshape(N_GROUPS, rows_x // N_GROUPS, D_MODEL)
    h_mid, h_meta, wup, wdn = _mixer(
        x_groups, meta_block, mix_norm[l][None, :], w_in_t, b_cat, wal_hi, wal_lo,
        b_alpha[l][None, :], attn_sinks[l], rope, gla_head_norm[l][None, :],
        w_proj_attn[l].astype(BF16), w_proj_gla[l].astype(BF16), w_out[l].astype(BF16),
        w_up, w_down, l, nb)

    cw = jnp.concatenate([conv_w[l], jnp.zeros((8 - conv_w.shape[1], D_FF), conv_w.dtype)], axis=0)
    out = _ffn(h_mid.reshape(rows_x, D_MODEL), h_meta, ffn_norm[l][None, :], wup,
               cw, conv_b[l][None, :], wdn, final_norm[None, :], seq,
               tm=1024, n_chunks=1)
    return out.reshape(batch, seq, D_MODEL)
```

```python
import functools

import numpy as np
import jax
import jax.numpy as jnp
from jax import lax
from jax.experimental import pallas as pl
from jax.experimental.pallas import tpu as pltpu

D_MODEL = 1024
N_META = 16
EPS = 1e-6
ATT_HEADS = 8
ATT_KV_HEADS = 2
HEAD_DIM = 64
BLK = 128
ROPE_THETA = 10000.0
GLA_HEADS = 4
GLA_DK = 256
GLA_DV = 512
GLA_DK_HEAD = GLA_DK // GLA_HEADS
GLA_DV_HEAD = GLA_DV // GLA_HEADS
GLA_RANK = 16
GLA_TAU = 16.0
D_FF = 2816
ATT_Q = ATT_HEADS * HEAD_DIM
ATT_KV = ATT_KV_HEADS * HEAD_DIM

META_PAD = BLK - N_META
LANES = 128
HALO = 16
NEG = -1e30
VMEM_LIMIT = 56 * 1024 * 1024

C_QKV = ATT_Q + 2 * ATT_KV
C_GLA = 2 * GLA_DK + GLA_DV
C_GR = GLA_DV
C_GATE = 2 * D_MODEL
C_LOW = LANES
O_QKV = 0
O_GLA = O_QKV + C_QKV
O_GR = O_GLA + C_GLA
O_GATE = O_GR + C_GR
O_LOW = O_GATE + C_GATE
C_ALL = O_LOW + C_LOW
PROJ_CHUNK = 256

GLA_LEVELS = (64, 32, 16, 8, 4, 2, 1)
N_LEVELS = len(GLA_LEVELS)
GLA_FINE = tuple(h for h in GLA_LEVELS if h < 8)

N_GROUPS = 2
MIXER_STARTS = (4, 0, 12, 8)
FFN_PARTS = 2
FFN_LAG = 1

BF16 = jnp.bfloat16
F32 = jnp.float32


def _sigmoid(x):
    return 1.0 / (1.0 + jnp.exp(-x))


def _dot(a, b):
    return jnp.dot(a, b, preferred_element_type=F32)


def _dot_nt(a, b):
    return lax.dot_general(a, b, (((1,), (1,)), ((), ())), preferred_element_type=F32)


def _split3(x):
    hi = x.astype(BF16)
    r1 = x - hi.astype(F32)
    mid = r1.astype(BF16)
    lo = (r1 - mid.astype(F32)).astype(BF16)
    return hi, mid, lo


def _params():
    return pltpu.CompilerParams(dimension_semantics=("arbitrary",), vmem_limit_bytes=VMEM_LIMIT)


LOW0 = ATT_Q + 2 * ATT_KV + 2 * GLA_DK + 2 * GLA_DV
D_IN = LOW0 + GLA_RANK + C_GATE


def _regroup_w_in(wt_ref, out_ref):
    for t in range(C_ALL // LANES):
        col = t * LANES
        if col < LOW0:
            rows = wt_ref[col:col + LANES, :]
        elif col < O_LOW:
            src = LOW0 + GLA_RANK + (col - O_GATE)
            rows = wt_ref[src:src + LANES, :]
        else:
            rows = jnp.concatenate([wt_ref[LOW0:LOW0 + GLA_RANK, :],
                                    jnp.zeros((C_LOW - GLA_RANK, D_MODEL), F32)], axis=0)
        out_ref[:, col:col + LANES] = rows.T.astype(BF16)


def _inproj_stages(x, weights, store):
    g_ref, w_ref, b_ref, wal_hi_ref, wal_lo_ref, bal_ref = weights
    var = jnp.mean(x * x, axis=-1, keepdims=True)
    u = (x * lax.rsqrt(var + EPS) * g_ref[...]).astype(BF16)
    yield

    def proj(off, width):
        return _dot(u, w_ref[:, off:off + width]) + b_ref[:, off:off + width]

    low = proj(O_LOW, C_LOW)
    yield
    silu = lambda y: y * _sigmoid(y)
    plan = ((0, O_QKV, C_QKV, None), (1, O_GLA, C_GLA, None), (2, O_GR, C_GR, silu),
            (3, O_GATE, C_GATE, _sigmoid))
    low_pending = True
    for i, off, width, act in plan:
        for c in range(0, width, PROJ_CHUNK):
            y = proj(off + c, PROJ_CHUNK)
            store(i, c, c + PROJ_CHUNK, (y if act is None else act(y)).astype(BF16))
            yield
            if low_pending:
                low_pending = False
                low_hi = low.astype(BF16)
                low_lo = (low - low_hi.astype(F32)).astype(BF16)
                z = (_dot(low_hi, wal_hi_ref[...]) + _dot(low_lo, wal_hi_ref[...])
                     + _dot(low_hi, wal_lo_ref[...]) + bal_ref[...])
                log_sig = jnp.minimum(z, 0.0) - jnp.log1p(jnp.exp(-jnp.abs(z)))
                store(4, 0, GLA_DK, log_sig * (1.0 / GLA_TAU))
                yield


def _attn_stages(result, is_meta, first_x, sinks_ref, qkv_ref, rope_ref,
                 kprev, vprev, kmeta, vmeta):
    cos = rope_ref[:, :LANES]
    sin = rope_ref[:, LANES:]
    lane = lax.broadcasted_iota(jnp.int32, (BLK, LANES), 1)
    first_half = (lane & (HEAD_DIM - 1)) < HEAD_DIM // 2

    def rope(xc):
        rot = jnp.where(first_half, pltpu.roll(xc, LANES - HEAD_DIM // 2, 1),
                        pltpu.roll(xc, HEAD_DIM // 2, 1))
        return xc * cos + rot * sin

    k_cur = rope(qkv_ref[:, ATT_Q:ATT_Q + ATT_KV].astype(F32)).astype(BF16)
    v_cur = qkv_ref[:, ATT_Q + ATT_KV:ATT_Q + 2 * ATT_KV]

    q_rows = []
    for hg in range(ATT_HEADS):
        chunk, off, grp = hg // 2, hg % 2, hg // (ATT_HEADS // ATT_KV_HEADS)
        qc = rope(qkv_ref[:, chunk * LANES:(chunk + 1) * LANES].astype(F32))
        if off != grp:
            qc = pltpu.roll(qc, HEAD_DIM, 1)
        on_group = (lane >= grp * HEAD_DIM) & (lane < (grp + 1) * HEAD_DIM)
        q_rows.append(jnp.where(on_group, qc * (HEAD_DIM ** -0.5), 0.0).astype(BF16))
    q_all = jnp.concatenate(q_rows, axis=0)

    k_prev = jnp.where(first_x, kmeta[...], kprev[...])
    v_prev = jnp.where(first_x, vmeta[...], vprev[...])
    kk = jnp.concatenate([k_prev, k_cur], axis=0)
    vv = jnp.concatenate([v_prev, v_cur], axis=0)
    kprev[...] = k_cur
    vprev[...] = v_cur
    kmeta[...] = jnp.where(is_meta, k_cur, kmeta[...])
    vmeta[...] = jnp.where(is_meta, v_cur, vmeta[...])
    yield

    sc = _dot_nt(q_all, kk).reshape(ATT_HEADS, BLK, 2 * BLK)
    yield

    rq = lax.broadcasted_iota(jnp.int32, (BLK, BLK), 0)
    ck = lax.broadcasted_iota(jnp.int32, (BLK, BLK), 1)
    from_cur = ck <= rq
    cur_min = jnp.where(is_meta, META_PAD, 0)
    prev_min = jnp.where(is_meta, BLK, jnp.where(first_x, META_PAD, 0))
    valid = ck >= jnp.where(from_cur, cur_min, prev_min)
    sc = jnp.where(from_cur[None], sc[:, :, BLK:], sc[:, :, :BLK])
    sc = jnp.where(valid[None], sc, NEG)

    hid = lax.broadcasted_iota(jnp.int32, (ATT_HEADS, 1, 1), 0)
    sink = jnp.zeros((ATT_HEADS, 1, 1), F32)
    for hg in range(ATT_HEADS):
        sink = jnp.where(hid == hg, sinks_ref[hg], sink)

    m = jnp.maximum(jnp.max(sc, axis=-1, keepdims=True), sink)
    p = jnp.exp(sc - m)
    denom = jnp.sum(p, axis=-1, keepdims=True) + jnp.exp(sink - m)
    p = jnp.concatenate([jnp.where(from_cur[None], 0.0, p), jnp.where(from_cur[None], p, 0.0)],
                        axis=-1)
    p = p.astype(BF16).reshape(ATT_HEADS * BLK, 2 * BLK)
    yield

    o = _dot(p, vv)
    yield

    o = o.reshape(ATT_HEADS, BLK, LANES) * (1.0 / denom)
    chunks = []
    for chunk in range(ATT_HEADS // 2):
        grp = (2 * chunk) // (ATT_HEADS // ATT_KV_HEADS)
        o0, o1 = o[2 * chunk], o[2 * chunk + 1]
        if grp != 0:
            o0 = pltpu.roll(o0, HEAD_DIM, 1)
        if grp != 1:
            o1 = pltpu.roll(o1, HEAD_DIM, 1)
        chunks.append(jnp.where(lane < HEAD_DIM, o0, o1).astype(BF16))
    result.append(jnp.concatenate(chunks, axis=1))


def _gla_tables():
    t = np.arange(BLK)
    tri = (t[:, None] >= t[None, :]).astype(np.float32)
    sel = np.zeros((len(GLA_FINE) * BLK, BLK), np.float32)
    for li, half in enumerate(GLA_FINE):
        boundary = (t & ~(2 * half - 1)) + half - 1
        sel[li * BLK + t, boundary] = 1.0
    x = t[:, None] ^ t[None, :]
    top = np.floor(np.log2(np.maximum(x, 1))).astype(np.int64)
    lvl = np.array([GLA_LEVELS.index(1 << int(v)) for v in top.ravel()]).reshape(BLK, BLK)
    lvl = np.where(t[:, None] == t[None, :], N_LEVELS, lvl)
    lvl = np.where(t[:, None] < t[None, :], N_LEVELS + 1, lvl)
    lvl = np.tile(lvl, (1, GLA_HEADS))
    return (jnp.asarray(tri, BF16), jnp.asarray(sel, BF16), jnp.asarray(lvl, jnp.int32))


def _gla_stages(result, is_meta, first_x, gla_ref, loga_ref, gr_ref, tri_ref, sel_ref, lvl_ref,
                hn_ref, state, state_meta):
    row = lax.broadcasted_iota(jnp.int32, (BLK, GLA_DK), 0)
    lane = lax.broadcasted_iota(jnp.int32, (BLK, GLA_DK), 1)
    is_token = row >= jnp.where(is_meta, META_PAD, 0)

    g3 = jnp.concatenate(_split3(loga_ref[...]), axis=1)
    yield
    b3 = _dot(tri_ref[...], g3)
    yield
    b = b3[:, :GLA_DK] + b3[:, GLA_DK:2 * GLA_DK] + b3[:, 2 * GLA_DK:]
    b_last = b[BLK - 1:BLK, :]

    b_hi = b.astype(BF16)
    b_mid = (b - b_hi.astype(F32)).astype(BF16)
    fine = _dot(sel_ref[...], jnp.concatenate([b_hi, b_mid], axis=1))

    qf = gla_ref[:, :GLA_DK].astype(F32) * (GLA_DK_HEAD ** -0.5)
    kf = jnp.where(is_token, gla_ref[:, GLA_DK:2 * GLA_DK].astype(F32), 0.0)

    head_mask = [jnp.where((lane >= h * GLA_DK_HEAD) & (lane < (h + 1) * GLA_DK_HEAD), 1.0, 0.0)
                 .astype(BF16) for h in range(GLA_HEADS)]

    def stack_heads(x):
        xb = x.astype(BF16)
        return jnp.concatenate([xb * hm for hm in head_mask], axis=0)

    state_in = jnp.where(first_x, state_meta[...], state[...])
    q_in = stack_heads(qf * jnp.exp(b))
    q_diag, k_diag = qf.astype(BF16), stack_heads(kf)
    yield
    o_inter = _dot(q_in, state_in.astype(BF16))
    lvl = lvl_ref[...]
    a = jnp.where(lvl == N_LEVELS, _dot_nt(q_diag, k_diag), 0.0)
    zeros = lambda shape: jnp.zeros(shape, F32)
    for li, half in enumerate(GLA_LEVELS):
        if half in GLA_FINE:
            fi = GLA_FINE.index(half)
            r = fine[fi * BLK:(fi + 1) * BLK, :GLA_DK] + fine[fi * BLK:(fi + 1) * BLK, GLA_DK:]
            upper = (row & half) != 0
            t_l = jnp.where(upper, qf, kf) * jnp.exp(-jnp.abs(b - r))
            q_l = jnp.where(upper, t_l, 0.0)
            k_l = jnp.where(upper, 0.0, t_l)
        else:
            grp = (BLK // (2 * half), 2 * half, GLA_DK)
            b_g, q_g, k_g = b.reshape(grp), qf.reshape(grp), kf.reshape(grp)
            r = b_g[:, half - 1:half, :]
            q_up = q_g[:, half:, :] * jnp.exp(b_g[:, half:, :] - r)
            k_lo = k_g[:, :half, :] * jnp.exp(r - b_g[:, :half, :])
            q_l = jnp.concatenate([zeros(q_up.shape), q_up], axis=1).reshape(BLK, GLA_DK)
            k_l = jnp.concatenate([k_lo, zeros(k_lo.shape)], axis=1).reshape(BLK, GLA_DK)
        q_l, k_l = q_l.astype(BF16), stack_heads(k_l)
        yield
        a = jnp.where(lvl == li, _dot_nt(q_l, k_l), a)
    a = a.astype(BF16)

    k_out_t = (kf * jnp.exp(b_last - b)).T.astype(BF16)
    decay_t = jnp.broadcast_to(jnp.exp(b_last), (BLK, GLA_DK)).T
    yield

    hn = hn_ref[...]
    new_state, heads = [], []
    for h in range(GLA_HEADS):
        v_h = gla_ref[:, 2 * GLA_DK + h * GLA_DV_HEAD:2 * GLA_DK + (h + 1) * GLA_DV_HEAD]
        o_h = o_inter[h * BLK:(h + 1) * BLK, :] + _dot(a[:, h * BLK:(h + 1) * BLK], v_h)
        new_state.append(_dot(k_out_t[h * GLA_DK_HEAD:(h + 1) * GLA_DK_HEAD, :], v_h))
        heads.append(o_h)
    yield
    for h in range(GLA_HEADS):
        o_h = heads[h]
        var = jnp.mean(o_h * o_h, axis=-1, keepdims=True)
        o_h = o_h * lax.rsqrt(var + EPS) * hn
        gate = gr_ref[:, h * GLA_DV_HEAD:(h + 1) * GLA_DV_HEAD].astype(F32)
        heads[h] = (o_h * gate).astype(BF16)
    state_new = state_in * decay_t + jnp.concatenate(new_state, axis=0)
    state[...] = state_new
    state_meta[...] = jnp.where(is_meta, state_new, state_meta[...])
    result.append(jnp.concatenate(heads, axis=1))


def _merge_stages(result, y_att, y_gla, gate, h, wpa_ref, wpg_ref, wo_ref):
    pa = _dot(y_att, wpa_ref[...])
    pg = _dot(y_gla, wpg_ref[...])
    yield
    mixed = (gate[:, :D_MODEL].astype(F32) * pa + gate[:, D_MODEL:].astype(F32) * pg).astype(BF16)
    yield
    result.append(h + _dot(mixed, wo_ref[...]))


def _advance(gens):
    alive = []
    for gen in gens:
        try:
            next(gen)
            alive.append(gen)
        except StopIteration:
            pass
    return alive


def _mixer_kernel(nb, sinks_ref, xnext_ref, meta_ref, g_ref, w_in_t_ref, b_ref, wal_hi_ref, wal_lo_ref,
                  bal_ref, rope_ref, tri_ref, sel_ref, lvl_ref, hn_ref, wpa_ref, wpg_ref,
                  wo_ref, wup_f32_ref, wdn_f32_ref, out_ref, outm_ref, wup_bf16_ref, wdn_bf16_ref,
                  w_cat, cur_h, cur_qkv, cur_gla, cur_gr, cur_gate, cur_loga,
                  nxt_qkv, nxt_gla, nxt_gr, nxt_gate, nxt_loga,
                  kprev, vprev, kmeta, vmeta, state, state_meta):
    s = pl.program_id(0)
    is_meta = s == 0
    first_x = jnp.logical_and(s >= 1, lax.rem(jnp.maximum(s - 1, 0), nb) == 0)
    cur = (cur_qkv, cur_gla, cur_gr, cur_gate, cur_loga)
    nxt = (nxt_qkv, nxt_gla, nxt_gr, nxt_gate, nxt_loga)
    weights = (g_ref, w_cat, b_ref, wal_hi_ref, wal_lo_ref, bal_ref)
    groups = range(N_GROUPS)

    @pl.when(is_meta)
    def _():
        for ref in (kprev, vprev, kmeta, vmeta, state, state_meta):
            ref[...] = jnp.zeros_like(ref)
        _regroup_w_in(w_in_t_ref, w_cat)

        def store_all_groups(i, c0, c1, val):
            for g in groups:
                cur[i][g, :, c0:c1] = val

        for g in groups:
            cur_h[g] = meta_ref[...]
        meta_proj = [_inproj_stages(meta_ref[...], weights, store_all_groups)]
        while meta_proj:
            meta_proj = _advance(meta_proj)

    def store_next(i, c0, c1, val):
        nxt[i][:, :, c0:c1] = val.reshape(N_GROUPS, BLK, c1 - c0)

    x_next = xnext_ref[...]
    inproj = [_inproj_stages(x_next.reshape(N_GROUPS * BLK, D_MODEL), weights, store_next)]

    y_att, y_gla, h_mid = [], [], []
    mixers = []
    for g in groups:
        mixers.append(_attn_stages(y_att, is_meta, first_x, sinks_ref, cur_qkv.at[g], rope_ref,
                                   kprev.at[g], vprev.at[g], kmeta.at[g], vmeta.at[g]))
        mixers.append(_gla_stages(y_gla, is_meta, first_x, cur_gla.at[g], cur_loga.at[g],
                                  cur_gr.at[g], tri_ref, sel_ref, lvl_ref, hn_ref,
                                  state.at[g], state_meta.at[g]))
    starts = sorted(zip(MIXER_STARTS, range(len(mixers))))
    active, rounds = [], 0
    while active or starts:
        while starts and starts[0][0] <= rounds:
            active.append(mixers[starts.pop(0)[1]])
        inproj = _advance(inproj)
        active = _advance(active)
        rounds += 1

    gate = cur_gate[...].reshape(N_GROUPS * BLK, C_GATE)
    h = cur_h[...].reshape(N_GROUPS * BLK, D_MODEL)
    merge = [_merge_stages(h_mid, jnp.concatenate(y_att, axis=0), jnp.concatenate(y_gla, axis=0),
                           gate, h, wpa_ref, wpg_ref, wo_ref)]
    while merge or inproj:
        merge = _advance(merge)
        inproj = _advance(inproj)

    h_mid = h_mid[0].reshape(N_GROUPS, BLK, D_MODEL)
    out_ref[...] = h_mid

    wup_bf16_ref[...] = wup_f32_ref[...].astype(BF16)
    wdn_bf16_ref[...] = wdn_f32_ref[...].astype(BF16)

    cur_h[...] = x_next
    for c_ref, n_ref in zip(cur, nxt):
        c_ref[...] = n_ref[...]

    @pl.when(is_meta)
    def _():
        outm_ref[...] = out_ref[0]


def _mixer(x_groups, meta_block, mix_norm, w_in_t, b_cat, wal_hi, wal_lo, b_alpha, sinks, rope,
           head_norm, wpa, wpg, wo, wup_f32, wdn_f32, layer, nb):
    rows_g = x_groups.shape[1]
    nblocks_g = rows_g // BLK

    def slab(w, rows, out):
        n = w.shape[1] // rows
        assert w.shape[1] % rows == 0 and n <= nblocks_g
        idx = lambda s: jnp.minimum(jnp.maximum(s - 1, 0), n - 1)
        if out:
            return pl.BlockSpec((rows, w.shape[2]), lambda s: (idx(s), 0))
        return pl.BlockSpec((None, rows, w.shape[2]), lambda s: (layer, idx(s), 0))

    wup_rows, wdn_rows = 16, 64
    tri, sel, lvl = _gla_tables()
    const = lambda s: (0, 0)
    xnext = lambda s: (0, jnp.minimum(s, nblocks_g - 1), 0)
    xout = lambda s: (0, jnp.maximum(s - 1, 0), 0)
    pos = lambda s: (jnp.where(s == 0, 0, lax.rem(jnp.maximum(s - 1, 0), nb) + 1), 0)
    grouped = lambda cols, dt: pltpu.VMEM((N_GROUPS, BLK, cols), dt)
    proj_bufs = [grouped(C_QKV, BF16), grouped(C_GLA, BF16), grouped(C_GR, BF16),
                 grouped(C_GATE, BF16), grouped(GLA_DK, F32)]
    return pl.pallas_call(
        functools.partial(_mixer_kernel, nb),
        grid=(nblocks_g + 1,),
        in_specs=[
            pl.BlockSpec(memory_space=pltpu.SMEM),
            pl.BlockSpec((N_GROUPS, BLK, D_MODEL), xnext),
            pl.BlockSpec((BLK, D_MODEL), const),
            pl.BlockSpec((1, D_MODEL), const),
            pl.BlockSpec((None, D_IN, D_MODEL), lambda s: (layer, 0, 0),
                         pipeline_mode=pl.Buffered(1)),
            pl.BlockSpec((1, C_ALL), const),
            pl.BlockSpec((LANES, GLA_DK), const),
            pl.BlockSpec((LANES, GLA_DK), const),
            pl.BlockSpec((1, GLA_DK), const),
            pl.BlockSpec((BLK, 2 * LANES), pos),
            pl.BlockSpec((BLK, BLK), const),
            pl.BlockSpec((len(GLA_FINE) * BLK, BLK), const),
            pl.BlockSpec((BLK, GLA_HEADS * BLK), const),
            pl.BlockSpec((1, GLA_DV_HEAD), const),
            pl.BlockSpec((ATT_Q, D_MODEL), const),
            pl.BlockSpec((GLA_DV, D_MODEL), const),
            pl.BlockSpec((D_MODEL, D_MODEL), const),
            slab(wup_f32, wup_rows, False),
            slab(wdn_f32, wdn_rows, False),
        ],
        out_specs=[pl.BlockSpec((N_GROUPS, BLK, D_MODEL), xout),
                   pl.BlockSpec((BLK, D_MODEL), const),
                   slab(wup_f32, wup_rows, True), slab(wdn_f32, wdn_rows, True)],
        out_shape=[jax.ShapeDtypeStruct((N_GROUPS, rows_g, D_MODEL), F32),
                   jax.ShapeDtypeStruct((BLK, D_MODEL), F32),
                   jax.ShapeDtypeStruct(wup_f32.shape[1:], BF16),
                   jax.ShapeDtypeStruct(wdn_f32.shape[1:], BF16)],
        scratch_shapes=[pltpu.VMEM((D_MODEL, C_ALL), BF16), grouped(D_MODEL, F32)]
        + proj_bufs + proj_bufs
        + [grouped(ATT_KV, BF16)] * 4
        + [pltpu.VMEM((N_GROUPS, GLA_DK, GLA_DV_HEAD), F32)] * 2,
        compiler_params=_params(),
        name="mixer",
    )(sinks, x_groups, meta_block, mix_norm, w_in_t, b_cat, wal_hi, wal_lo, b_alpha, rope,
      tri, sel, lvl, head_norm, wpa, wpg, wo, wup_f32, wdn_f32)


def _ffn_kernel(tm, tiles_per_seq, h_ref, halo_ref, halo_meta_ref, g_ref, wup_ref, cw_ref,
                cb_ref, wdn_ref, gf_ref, out_ref, u_scr):
    def normed(x):
        var = jnp.mean(x * x, axis=-1, keepdims=True)
        return (x * lax.rsqrt(var + EPS) * g_ref[...]).astype(BF16)

    seq_start = lax.rem(pl.program_id(0), tiles_per_seq) == 0
    u_scr[0:HALO, :] = normed(jnp.where(seq_start, halo_meta_ref[...], halo_ref[...]))
    u_scr[HALO:, :] = normed(h_ref[...])

    part_rows = tm // FFN_PARTS

    def part_stages(pi):
        r0 = pi * part_rows
        u = u_scr[r0:r0 + HALO + part_rows, :]
        a = _dot(u, wup_ref[:, :D_FF])
        yield
        v = _dot(u[HALO:], wup_ref[:, D_FF:])
        yield
        cw = cw_ref[...]
        conv = (cw[0:1] * pltpu.roll(a, 2, 0) + cw[1:2] * pltpu.roll(a, 1, 0) + cw[2:3] * a
                + cb_ref[...])[HALO:]
        inner = 0.7978845608028654 * (conv + 0.044715 * (conv * conv * conv))
        act = conv * (0.5 * (1.0 + jnp.tanh(inner)))
        gated = (act * v).astype(BF16)
        yield
        h2 = h_ref[r0:r0 + part_rows, :] + _dot(gated, wdn_ref[...])
        yield
        var = jnp.mean(h2 * h2, axis=-1, keepdims=True)
        out_ref[r0:r0 + part_rows, :] = h2 * lax.rsqrt(var + EPS) * gf_ref[...]

    waiting = [part_stages(pi) for pi in range(FFN_PARTS)]
    active, rounds = [], 0
    while active or waiting:
        if waiting and rounds % FFN_LAG == 0:
            active.append(waiting.pop(0))
        active = _advance(active)
        rounds += 1


def _ffn(h_mid, h_meta, ffn_norm, wup, conv_w, conv_b, wdn, final_norm, seq, tm):
    rows_x = h_mid.shape[0]
    assert seq % tm == 0 and rows_x % seq == 0 and tm % (FFN_PARTS * HALO) == 0
    tiles_per_seq = seq // tm
    const = lambda i: (0, 0)
    row = lambda i: (i, 0)
    return pl.pallas_call(
        functools.partial(_ffn_kernel, tm, tiles_per_seq),
        grid=(rows_x // tm,),
        in_specs=[
            pl.BlockSpec((tm, D_MODEL), row),
            pl.BlockSpec((HALO, D_MODEL), lambda i: (jnp.maximum(i * (tm // HALO) - 1, 0), 0)),
            pl.BlockSpec((HALO, D_MODEL), lambda i: (BLK // HALO - 1, 0)),
            pl.BlockSpec((1, D_MODEL), const),
            pl.BlockSpec((D_MODEL, 2 * D_FF), const, pipeline_mode=pl.Buffered(1)),
            pl.BlockSpec((8, D_FF), const),
            pl.BlockSpec((1, D_FF), const),
            pl.BlockSpec((D_FF, D_MODEL), const, pipeline_mode=pl.Buffered(1)),
            pl.BlockSpec((1, D_MODEL), const),
        ],
        out_specs=pl.BlockSpec((tm, D_MODEL), row),
        out_shape=jax.ShapeDtypeStruct((rows_x, D_MODEL), F32),
        scratch_shapes=[pltpu.VMEM((HALO + tm, D_MODEL), BF16)],
        compiler_params=_params(),
        name="conv_ffn",
    )(h_mid, h_mid, h_meta, ffn_norm, wup, conv_w, conv_b, wdn, final_norm)


def _rope_tables(nb):
    half = HEAD_DIM // 2
    inv_freq = ROPE_THETA ** (-np.arange(half, dtype=np.float64) / half)
    pos = (np.arange((nb + 1) * BLK) - META_PAD).astype(np.float64)
    ang = pos[:, None] * inv_freq[None, :]
    cos2 = np.tile(np.cos(ang), (1, LANES // half))
    sin = np.sin(ang)
    sin2 = np.tile(np.concatenate([-sin, sin], axis=-1), (1, LANES // HEAD_DIM))
    return jnp.asarray(np.concatenate([cos2, sin2], axis=1), F32)


def kernel(x, meta_tokens, mix_norm, w_in, b_in, w_alpha, b_alpha, attn_sinks, gla_head_norm,
           w_proj_attn, w_proj_gla, w_out, ffn_norm, w_up, conv_w, conv_b, w_down, final_norm):
    batch, seq, _ = x.shape
    assert batch % N_GROUPS == 0 and seq % BLK == 0
    rows_x = batch * seq
    nb = seq // BLK
    l = 0

    meta_block = jnp.concatenate(
        [jnp.zeros((META_PAD, D_MODEL), x.dtype), meta_tokens.astype(x.dtype)], axis=0)

    lo0 = LOW0
    bi = b_in[l]
    w_in_t = jnp.swapaxes(w_in, 1, 2)
    b_cat = jnp.concatenate(
        [bi[:lo0], bi[lo0 + GLA_RANK:], bi[lo0:lo0 + GLA_RANK],
         jnp.zeros((C_LOW - GLA_RANK,), bi.dtype)])[None, :]
    wal = jnp.concatenate(
        [w_alpha[l], jnp.zeros((C_LOW - GLA_RANK, GLA_DK), w_alpha.dtype)], axis=0)
    wal_hi = wal.astype(BF16)
    wal_lo = (wal - wal_hi.astype(F32)).astype(BF16)

    rope = _rope_tables(nb)
    x_groups = x.reshape(N_GROUPS, rows_x // N_GROUPS, D_MODEL)
    h_mid, h_meta, wup, wdn = _mixer(
        x_groups, meta_block, mix_norm[l][None, :], w_in_t, b_cat, wal_hi, wal_lo,
        b_alpha[l][None, :], attn_sinks[l], rope, gla_head_norm[l][None, :],
        w_proj_attn[l].astype(BF16), w_proj_gla[l].astype(BF16), w_out[l].astype(BF16),
        w_up, w_down, l, nb)

    cw = jnp.concatenate([conv_w[l], jnp.zeros((8 - conv_w.shape[1], D_FF), conv_w.dtype)], axis=0)
    out = _ffn(h_mid.reshape(rows_x, D_MODEL), h_meta, ffn_norm[l][None, :], wup,
               cw, conv_b[l][None, :], wdn, final_norm[None, :], seq,
               tm=1024)
    return out.reshape(batch, seq, D_MODEL)
```

```python
import functools

import numpy as np
import jax
import jax.numpy as jnp
from jax import lax
from jax.experimental import pallas as pl
from jax.experimental.pallas import tpu as pltpu

D_MODEL = 1024
N_META = 16
EPS = 1e-6
ATT_HEADS = 8
ATT_KV_HEADS = 2
HEAD_DIM = 64
BLK = 128
ROPE_THETA = 10000.0
GLA_HEADS = 4
GLA_DK = 256
GLA_DV = 512
GLA_DK_HEAD = GLA_DK // GLA_HEADS
GLA_DV_HEAD = GLA_DV // GLA_HEADS
GLA_RANK = 16
GLA_TAU = 16.0
D_FF = 2816
ATT_Q = ATT_HEADS * HEAD_DIM
ATT_KV = ATT_KV_HEADS * HEAD_DIM

META_PAD = BLK - N_META
LANES = 128
HALO = 16
NEG = -1e30
VMEM_LIMIT = 56 * 1024 * 1024

C_QKV = ATT_Q + 2 * ATT_KV
C_GLA = 2 * GLA_DK + GLA_DV
C_GR = GLA_DV
C_GATE = 2 * D_MODEL
C_LOW = LANES
O_QKV = 0
O_GLA = O_QKV + C_QKV
O_GR = O_GLA + C_GLA
O_GATE = O_GR + C_GR
O_LOW = O_GATE + C_GATE
C_ALL = O_LOW + C_LOW
PROJ_CHUNK = 256

GLA_LEVELS = (64, 32, 16, 8, 4, 2, 1)
N_LEVELS = len(GLA_LEVELS)
GLA_FINE = tuple(h for h in GLA_LEVELS if h < 8)

N_GROUPS = 2
MIXER_STARTS = (5, 0, 12, 8)
FFN_PARTS = 2
FFN_LAG = 1

BF16 = jnp.bfloat16
F32 = jnp.float32


def _sigmoid(x):
    return 1.0 / (1.0 + jnp.exp(-x))


def _dot(a, b):
    return jnp.dot(a, b, preferred_element_type=F32)


def _dot_nt(a, b):
    return lax.dot_general(a, b, (((1,), (1,)), ((), ())), preferred_element_type=F32)


def _split3(x):
    hi = x.astype(BF16)
    r1 = x - hi.astype(F32)
    mid = r1.astype(BF16)
    lo = (r1 - mid.astype(F32)).astype(BF16)
    return hi, mid, lo


def _params():
    return pltpu.CompilerParams(dimension_semantics=("arbitrary",), vmem_limit_bytes=VMEM_LIMIT)


LOW0 = ATT_Q + 2 * ATT_KV + 2 * GLA_DK + 2 * GLA_DV
D_IN = LOW0 + GLA_RANK + C_GATE


def _regroup_w_in(wt_ref, out_ref):
    for t in range(C_ALL // LANES):
        col = t * LANES
        if col < LOW0:
            rows = wt_ref[col:col + LANES, :]
        elif col < O_LOW:
            src = LOW0 + GLA_RANK + (col - O_GATE)
            rows = wt_ref[src:src + LANES, :]
        else:
            rows = jnp.concatenate([wt_ref[LOW0:LOW0 + GLA_RANK, :],
                                    jnp.zeros((C_LOW - GLA_RANK, D_MODEL), F32)], axis=0)
        out_ref[:, col:col + LANES] = rows.T.astype(BF16)


def _inproj_stages(x, weights, store):
    g_ref, w_ref, b_ref, wal_hi_ref, wal_lo_ref, bal_ref = weights
    var = jnp.mean(x * x, axis=-1, keepdims=True)
    u = (x * lax.rsqrt(var + EPS) * g_ref[...]).astype(BF16)
    yield

    def proj(off, width):
        return _dot(u, w_ref[:, off:off + width]) + b_ref[:, off:off + width]

    low = proj(O_LOW, C_LOW)
    yield
    silu = lambda y: y * _sigmoid(y)
    plan = ((0, O_QKV, C_QKV, None), (1, O_GLA, C_GLA, None), (2, O_GR, C_GR, silu),
            (3, O_GATE, C_GATE, _sigmoid))
    low_pending = True
    for i, off, width, act in plan:
        for c in range(0, width, PROJ_CHUNK):
            y = proj(off + c, PROJ_CHUNK)
            store(i, c, c + PROJ_CHUNK, (y if act is None else act(y)).astype(BF16))
            yield
            if low_pending:
                low_pending = False
                low_hi = low.astype(BF16)
                low_lo = (low - low_hi.astype(F32)).astype(BF16)
                z = (_dot(low_hi, wal_hi_ref[...]) + _dot(low_lo, wal_hi_ref[...])
                     + _dot(low_hi, wal_lo_ref[...]) + bal_ref[...])
                log_sig = jnp.minimum(z, 0.0) - jnp.log1p(jnp.exp(-jnp.abs(z)))
                store(4, 0, GLA_DK, log_sig * (1.0 / GLA_TAU))
                yield


def _attn_stages(result, is_meta, first_x, sinks_ref, qkv_ref, rope_ref,
                 kprev, vprev, kmeta, vmeta):
    cos = rope_ref[:, :LANES]
    sin = rope_ref[:, LANES:]
    lane = lax.broadcasted_iota(jnp.int32, (BLK, LANES), 1)
    first_half = (lane & (HEAD_DIM - 1)) < HEAD_DIM // 2

    def rope(xc):
        rot = jnp.where(first_half, pltpu.roll(xc, LANES - HEAD_DIM // 2, 1),
                        pltpu.roll(xc, HEAD_DIM // 2, 1))
        return xc * cos + rot * sin

    k_cur = rope(qkv_ref[:, ATT_Q:ATT_Q + ATT_KV].astype(F32)).astype(BF16)
    v_cur = qkv_ref[:, ATT_Q + ATT_KV:ATT_Q + 2 * ATT_KV]

    q_rows = []
    for hg in range(ATT_HEADS):
        chunk, off, grp = hg // 2, hg % 2, hg // (ATT_HEADS // ATT_KV_HEADS)
        qc = rope(qkv_ref[:, chunk * LANES:(chunk + 1) * LANES].astype(F32))
        if off != grp:
            qc = pltpu.roll(qc, HEAD_DIM, 1)
        on_group = (lane >= grp * HEAD_DIM) & (lane < (grp + 1) * HEAD_DIM)
        q_rows.append(jnp.where(on_group, qc * (HEAD_DIM ** -0.5), 0.0).astype(BF16))
    q_all = jnp.concatenate(q_rows, axis=0)

    k_prev = jnp.where(first_x, kmeta[...], kprev[...])
    v_prev = jnp.where(first_x, vmeta[...], vprev[...])
    kk = jnp.concatenate([k_prev, k_cur], axis=0)
    vv = jnp.concatenate([v_prev, v_cur], axis=0)
    kprev[...] = k_cur
    vprev[...] = v_cur
    kmeta[...] = jnp.where(is_meta, k_cur, kmeta[...])
    vmeta[...] = jnp.where(is_meta, v_cur, vmeta[...])
    yield

    sc = _dot_nt(q_all, kk).reshape(ATT_HEADS, BLK, 2 * BLK)
    yield

    rq = lax.broadcasted_iota(jnp.int32, (BLK, BLK), 0)
    ck = lax.broadcasted_iota(jnp.int32, (BLK, BLK), 1)
    from_cur = ck <= rq
    cur_min = jnp.where(is_meta, META_PAD, 0)
    prev_min = jnp.where(is_meta, BLK, jnp.where(first_x, META_PAD, 0))
    valid = ck >= jnp.where(from_cur, cur_min, prev_min)
    sc = jnp.where(from_cur[None], sc[:, :, BLK:], sc[:, :, :BLK])
    sc = jnp.where(valid[None], sc, NEG)

    hid = lax.broadcasted_iota(jnp.int32, (ATT_HEADS, 1, 1), 0)
    sink = jnp.zeros((ATT_HEADS, 1, 1), F32)
    for hg in range(ATT_HEADS):
        sink = jnp.where(hid == hg, sinks_ref[hg], sink)

    m = jnp.maximum(jnp.max(sc, axis=-1, keepdims=True), sink)
    p = jnp.exp(sc - m)
    denom = jnp.sum(p, axis=-1, keepdims=True) + jnp.exp(sink - m)
    p = jnp.concatenate([jnp.where(from_cur[None], 0.0, p), jnp.where(from_cur[None], p, 0.0)],
                        axis=-1)
    p = p.astype(BF16).reshape(ATT_HEADS * BLK, 2 * BLK)
    yield

    o = _dot(p, vv)
    yield

    o = o.reshape(ATT_HEADS, BLK, LANES) * (1.0 / denom)
    chunks = []
    for chunk in range(ATT_HEADS // 2):
        grp = (2 * chunk) // (ATT_HEADS // ATT_KV_HEADS)
        o0, o1 = o[2 * chunk], o[2 * chunk + 1]
        if grp != 0:
            o0 = pltpu.roll(o0, HEAD_DIM, 1)
        if grp != 1:
            o1 = pltpu.roll(o1, HEAD_DIM, 1)
        chunks.append(jnp.where(lane < HEAD_DIM, o0, o1).astype(BF16))
    result.append(jnp.concatenate(chunks, axis=1))


def _gla_tables():
    t = np.arange(BLK)
    tri = (t[:, None] >= t[None, :]).astype(np.float32)
    sel = np.zeros((len(GLA_FINE) * BLK, BLK), np.float32)
    for li, half in enumerate(GLA_FINE):
        boundary = (t & ~(2 * half - 1)) + half - 1
        sel[li * BLK + t, boundary] = 1.0
    x = t[:, None] ^ t[None, :]
    top = np.floor(np.log2(np.maximum(x, 1))).astype(np.int64)
    lvl = np.array([GLA_LEVELS.index(1 << int(v)) for v in top.ravel()]).reshape(BLK, BLK)
    lvl = np.where(t[:, None] == t[None, :], N_LEVELS, lvl)
    lvl = np.where(t[:, None] < t[None, :], N_LEVELS + 1, lvl)
    lvl = np.tile(lvl, (1, GLA_HEADS))
    return (jnp.asarray(tri, BF16), jnp.asarray(sel, BF16), jnp.asarray(lvl, jnp.int32))


def _gla_stages(result, is_meta, first_x, gla_ref, loga_ref, gr_ref, tri_ref, sel_ref, lvl_ref,
                hn_ref, state, state_meta):
    row = lax.broadcasted_iota(jnp.int32, (BLK, GLA_DK), 0)
    lane = lax.broadcasted_iota(jnp.int32, (BLK, GLA_DK), 1)
    is_token = row >= jnp.where(is_meta, META_PAD, 0)

    g3 = jnp.concatenate(_split3(loga_ref[...]), axis=1)
    yield
    b3 = _dot(tri_ref[...], g3)
    yield
    b = b3[:, :GLA_DK] + b3[:, GLA_DK:2 * GLA_DK] + b3[:, 2 * GLA_DK:]
    b_last = b[BLK - 1:BLK, :]

    b_hi = b.astype(BF16)
    b_mid = (b - b_hi.astype(F32)).astype(BF16)
    fine = _dot(sel_ref[...], jnp.concatenate([b_hi, b_mid], axis=1))

    qf = gla_ref[:, :GLA_DK].astype(F32) * (GLA_DK_HEAD ** -0.5)
    kf = jnp.where(is_token, gla_ref[:, GLA_DK:2 * GLA_DK].astype(F32), 0.0)

    head_mask = [jnp.where((lane >= h * GLA_DK_HEAD) & (lane < (h + 1) * GLA_DK_HEAD), 1.0, 0.0)
                 .astype(BF16) for h in range(GLA_HEADS)]

    def stack_heads(x):
        xb = x.astype(BF16)
        return jnp.concatenate([xb * hm for hm in head_mask], axis=0)

    state_in = jnp.where(first_x, state_meta[...], state[...])
    q_in = stack_heads(qf * jnp.exp(b))
    q_diag, k_diag = qf.astype(BF16), stack_heads(kf)
    yield
    o_inter = _dot(q_in, state_in.astype(BF16))
    lvl = lvl_ref[...]
    a = jnp.where(lvl == N_LEVELS, _dot_nt(q_diag, k_diag), 0.0)
    zeros = lambda shape: jnp.zeros(shape, F32)
    for li, half in enumerate(GLA_LEVELS):
        if half in GLA_FINE:
            fi = GLA_FINE.index(half)
            r = fine[fi * BLK:(fi + 1) * BLK, :GLA_DK] + fine[fi * BLK:(fi + 1) * BLK, GLA_DK:]
            upper = (row & half) != 0
            t_l = jnp.where(upper, qf, kf) * jnp.exp(-jnp.abs(b - r))
            q_l = jnp.where(upper, t_l, 0.0)
            k_l = jnp.where(upper, 0.0, t_l)
        else:
            grp = (BLK // (2 * half), 2 * half, GLA_DK)
            b_g, q_g, k_g = b.reshape(grp), qf.reshape(grp), kf.reshape(grp)
            r = b_g[:, half - 1:half, :]
            q_up = q_g[:, half:, :] * jnp.exp(b_g[:, half:, :] - r)
            k_lo = k_g[:, :half, :] * jnp.exp(r - b_g[:, :half, :])
            q_l = jnp.concatenate([zeros(q_up.shape), q_up], axis=1).reshape(BLK, GLA_DK)
            k_l = jnp.concatenate([k_lo, zeros(k_lo.shape)], axis=1).reshape(BLK, GLA_DK)
        q_l, k_l = q_l.astype(BF16), stack_heads(k_l)
        yield
        a = jnp.where(lvl == li, _dot_nt(q_l, k_l), a)
    a = a.astype(BF16)

    k_out_t = (kf * jnp.exp(b_last - b)).T.astype(BF16)
    decay_t = jnp.broadcast_to(jnp.exp(b_last), (BLK, GLA_DK)).T
    yield

    hn = hn_ref[...]
    new_state, heads = [], []
    for h in range(GLA_HEADS):
        v_h = gla_ref[:, 2 * GLA_DK + h * GLA_DV_HEAD:2 * GLA_DK + (h + 1) * GLA_DV_HEAD]
        o_h = o_inter[h * BLK:(h + 1) * BLK, :] + _dot(a[:, h * BLK:(h + 1) * BLK], v_h)
        new_state.append(_dot(k_out_t[h * GLA_DK_HEAD:(h + 1) * GLA_DK_HEAD, :], v_h))
        heads.append(o_h)
    yield
    for h in range(GLA_HEADS):
        o_h = heads[h]
        var = jnp.mean(o_h * o_h, axis=-1, keepdims=True)
        o_h = o_h * lax.rsqrt(var + EPS) * hn
        gate = gr_ref[:, h * GLA_DV_HEAD:(h + 1) * GLA_DV_HEAD].astype(F32)
        heads[h] = (o_h * gate).astype(BF16)
    state_new = state_in * decay_t + jnp.concatenate(new_state, axis=0)
    state[...] = state_new
    state_meta[...] = jnp.where(is_meta, state_new, state_meta[...])
    result.append(jnp.concatenate(heads, axis=1))


def _merge_stages(result, y_att, y_gla, gate, h, wpa_ref, wpg_ref, wo_ref):
    pa = _dot(y_att, wpa_ref[...])
    pg = _dot(y_gla, wpg_ref[...])
    yield
    mixed = (gate[:, :D_MODEL].astype(F32) * pa + gate[:, D_MODEL:].astype(F32) * pg).astype(BF16)
    yield
    result.append(h + _dot(mixed, wo_ref[...]))


def _advance(gens):
    alive = []
    for gen in gens:
        try:
            next(gen)
            alive.append(gen)
        except StopIteration:
            pass
    return alive


def _mixer_kernel(nb, sinks_ref, xnext_ref, meta_ref, g_ref, w_in_t_ref, b_ref, wal_hi_ref, wal_lo_ref,
                  bal_ref, rope_ref, tri_ref, sel_ref, lvl_ref, hn_ref, wpa_ref, wpg_ref,
                  wo_ref, wup_f32_ref, wdn_f32_ref, out_ref, outm_ref, wup_bf16_ref, wdn_bf16_ref,
                  w_cat, cur_h, cur_qkv, cur_gla, cur_gr, cur_gate, cur_loga,
                  nxt_qkv, nxt_gla, nxt_gr, nxt_gate, nxt_loga,
                  kprev, vprev, kmeta, vmeta, state, state_meta):
    s = pl.program_id(0)
    is_meta = s == 0
    first_x = jnp.logical_and(s >= 1, lax.rem(jnp.maximum(s - 1, 0), nb) == 0)
    cur = (cur_qkv, cur_gla, cur_gr, cur_gate, cur_loga)
    nxt = (nxt_qkv, nxt_gla, nxt_gr, nxt_gate, nxt_loga)
    weights = (g_ref, w_cat, b_ref, wal_hi_ref, wal_lo_ref, bal_ref)
    groups = range(N_GROUPS)

    @pl.when(is_meta)
    def _():
        for ref in (kprev, vprev, kmeta, vmeta, state, state_meta):
            ref[...] = jnp.zeros_like(ref)
        _regroup_w_in(w_in_t_ref, w_cat)

        def store_all_groups(i, c0, c1, val):
            for g in groups:
                cur[i][g, :, c0:c1] = val

        for g in groups:
            cur_h[g] = meta_ref[...]
        meta_proj = [_inproj_stages(meta_ref[...], weights, store_all_groups)]
        while meta_proj:
            meta_proj = _advance(meta_proj)

    def store_next(i, c0, c1, val):
        nxt[i][:, :, c0:c1] = val.reshape(N_GROUPS, BLK, c1 - c0)

    x_next = xnext_ref[...]
    inproj = [_inproj_stages(x_next.reshape(N_GROUPS * BLK, D_MODEL), weights, store_next)]

    y_att, y_gla, h_mid = [], [], []
    mixers = []
    for g in groups:
        mixers.append(_attn_stages(y_att, is_meta, first_x, sinks_ref, cur_qkv.at[g], rope_ref,
                                   kprev.at[g], vprev.at[g], kmeta.at[g], vmeta.at[g]))
        mixers.append(_gla_stages(y_gla, is_meta, first_x, cur_gla.at[g], cur_loga.at[g],
                                  cur_gr.at[g], tri_ref, sel_ref, lvl_ref, hn_ref,
                                  state.at[g], state_meta.at[g]))
    starts = sorted(zip(MIXER_STARTS, range(len(mixers))))
    active, rounds = [], 0
    while active or starts:
        while starts and starts[0][0] <= rounds:
            active.append(mixers[starts.pop(0)[1]])
        active = _advance(active)
        inproj = _advance(inproj)
        rounds += 1

    gate = cur_gate[...].reshape(N_GROUPS * BLK, C_GATE)
    h = cur_h[...].reshape(N_GROUPS * BLK, D_MODEL)
    merge = [_merge_stages(h_mid, jnp.concatenate(y_att, axis=0), jnp.concatenate(y_gla, axis=0),
                           gate, h, wpa_ref, wpg_ref, wo_ref)]
    while merge or inproj:
        merge = _advance(merge)
        inproj = _advance(inproj)

    h_mid = h_mid[0].reshape(N_GROUPS, BLK, D_MODEL)
    out_ref[...] = h_mid

    wup_bf16_ref[...] = wup_f32_ref[...].astype(BF16)
    wdn_bf16_ref[...] = wdn_f32_ref[...].astype(BF16)

    cur_h[...] = x_next
    for c_ref, n_ref in zip(cur, nxt):
        c_ref[...] = n_ref[...]

    @pl.when(is_meta)
    def _():
        outm_ref[...] = out_ref[0]


def _mixer(x_groups, meta_block, mix_norm, w_in_t, b_cat, wal_hi, wal_lo, b_alpha, sinks, rope,
           head_norm, wpa, wpg, wo, wup_f32, wdn_f32, layer, nb):
    rows_g = x_groups.shape[1]
    nblocks_g = rows_g // BLK

    def slab(w, rows, out):
        n = w.shape[1] // rows
        assert w.shape[1] % rows == 0 and n <= nblocks_g
        idx = lambda s: jnp.minimum(jnp.maximum(s - 1, 0), n - 1)
        if out:
            return pl.BlockSpec((rows, w.shape[2]), lambda s: (idx(s), 0))
        return pl.BlockSpec((None, rows, w.shape[2]), lambda s: (layer, idx(s), 0))

    wup_rows, wdn_rows = 16, 64
    tri, sel, lvl = _gla_tables()
    const = lambda s: (0, 0)
    xnext = lambda s: (0, jnp.minimum(s, nblocks_g - 1), 0)
    xout = lambda s: (0, jnp.maximum(s - 1, 0), 0)
    pos = lambda s: (jnp.where(s == 0, 0, lax.rem(jnp.maximum(s - 1, 0), nb) + 1), 0)
    grouped = lambda cols, dt: pltpu.VMEM((N_GROUPS, BLK, cols), dt)
    proj_bufs = [grouped(C_QKV, BF16), grouped(C_GLA, BF16), grouped(C_GR, BF16),
                 grouped(C_GATE, BF16), grouped(GLA_DK, F32)]
    return pl.pallas_call(
        functools.partial(_mixer_kernel, nb),
        grid=(nblocks_g + 1,),
        in_specs=[
            pl.BlockSpec(memory_space=pltpu.SMEM),
            pl.BlockSpec((N_GROUPS, BLK, D_MODEL), xnext),
            pl.BlockSpec((BLK, D_MODEL), const),
            pl.BlockSpec((1, D_MODEL), const),
            pl.BlockSpec((None, D_IN, D_MODEL), lambda s: (layer, 0, 0),
                         pipeline_mode=pl.Buffered(1)),
            pl.BlockSpec((1, C_ALL), const),
            pl.BlockSpec((LANES, GLA_DK), const),
            pl.BlockSpec((LANES, GLA_DK), const),
            pl.BlockSpec((1, GLA_DK), const),
            pl.BlockSpec((BLK, 2 * LANES), pos),
            pl.BlockSpec((BLK, BLK), const),
            pl.BlockSpec((len(GLA_FINE) * BLK, BLK), const),
            pl.BlockSpec((BLK, GLA_HEADS * BLK), const),
            pl.BlockSpec((1, GLA_DV_HEAD), const),
            pl.BlockSpec((ATT_Q, D_MODEL), const),
            pl.BlockSpec((GLA_DV, D_MODEL), const),
            pl.BlockSpec((D_MODEL, D_MODEL), const),
            slab(wup_f32, wup_rows, False),
            slab(wdn_f32, wdn_rows, False),
        ],
        out_specs=[pl.BlockSpec((N_GROUPS, BLK, D_MODEL), xout),
                   pl.BlockSpec((BLK, D_MODEL), const),
                   slab(wup_f32, wup_rows, True), slab(wdn_f32, wdn_rows, True)],
        out_shape=[jax.ShapeDtypeStruct((N_GROUPS, rows_g, D_MODEL), F32),
                   jax.ShapeDtypeStruct((BLK, D_MODEL), F32),
                   jax.ShapeDtypeStruct(wup_f32.shape[1:], BF16),
                   jax.ShapeDtypeStruct(wdn_f32.shape[1:], BF16)],
        scratch_shapes=[pltpu.VMEM((D_MODEL, C_ALL), BF16), grouped(D_MODEL, F32)]
        + proj_bufs + proj_bufs
        + [grouped(ATT_KV, BF16)] * 4
        + [pltpu.VMEM((N_GROUPS, GLA_DK, GLA_DV_HEAD), F32)] * 2,
        compiler_params=_params(),
        name="mixer",
    )(sinks, x_groups, meta_block, mix_norm, w_in_t, b_cat, wal_hi, wal_lo, b_alpha, rope,
      tri, sel, lvl, head_norm, wpa, wpg, wo, wup_f32, wdn_f32)


def _ffn_kernel(tm, tiles_per_seq, h_ref, halo_ref, halo_meta_ref, g_ref, wup_ref, cw_ref,
                cb_ref, wdn_ref, gf_ref, out_ref, u_scr):
    def normed(x):
        var = jnp.mean(x * x, axis=-1, keepdims=True)
        return (x * lax.rsqrt(var + EPS) * g_ref[...]).astype(BF16)

    seq_start = lax.rem(pl.program_id(0), tiles_per_seq) == 0
    u_scr[0:HALO, :] = normed(jnp.where(seq_start, halo_meta_ref[...], halo_ref[...]))
    u_scr[HALO:, :] = normed(h_ref[...])

    part_rows = tm // FFN_PARTS

    def part_stages(pi):
        r0 = pi * part_rows
        u = u_scr[r0:r0 + HALO + part_rows, :]
        a = _dot(u, wup_ref[:, :D_FF])
        yield
        v = _dot(u[HALO:], wup_ref[:, D_FF:])
        yield
        cw = cw_ref[...]
        conv = (cw[0:1] * pltpu.roll(a, 2, 0) + cw[1:2] * pltpu.roll(a, 1, 0) + cw[2:3] * a
                + cb_ref[...])[HALO:]
        inner = 0.7978845608028654 * (conv + 0.044715 * (conv * conv * conv))
        act = conv * (0.5 * (1.0 + jnp.tanh(inner)))
        gated = (act * v).astype(BF16)
        yield
        h2 = h_ref[r0:r0 + part_rows, :] + _dot(gated, wdn_ref[...])
        yield
        var = jnp.mean(h2 * h2, axis=-1, keepdims=True)
        out_ref[r0:r0 + part_rows, :] = h2 * lax.rsqrt(var + EPS) * gf_ref[...]

    waiting = [part_stages(pi) for pi in range(FFN_PARTS)]
    active, rounds = [], 0
    while active or waiting:
        if waiting and rounds % FFN_LAG == 0:
            active.append(waiting.pop(0))
        active = _advance(active)
        rounds += 1


def _ffn(h_mid, h_meta, ffn_norm, wup, conv_w, conv_b, wdn, final_norm, seq, tm):
    rows_x = h_mid.shape[0]
    assert seq % tm == 0 and rows_x % seq == 0 and tm % (FFN_PARTS * HALO) == 0
    tiles_per_seq = seq // tm
    const = lambda i: (0, 0)
    row = lambda i: (i, 0)
    return pl.pallas_call(
        functools.partial(_ffn_kernel, tm, tiles_per_seq),
        grid=(rows_x // tm,),
        in_specs=[
            pl.BlockSpec((tm, D_MODEL), row),
            pl.BlockSpec((HALO, D_MODEL), lambda i: (jnp.maximum(i * (tm // HALO) - 1, 0), 0)),
            pl.BlockSpec((HALO, D_MODEL), lambda i: (BLK // HALO - 1, 0)),
            pl.BlockSpec((1, D_MODEL), const),
            pl.BlockSpec((D_MODEL, 2 * D_FF), const, pipeline_mode=pl.Buffered(1)),
            pl.BlockSpec((8, D_FF), const),
            pl.BlockSpec((1, D_FF), const),
            pl.BlockSpec((D_FF, D_MODEL), const, pipeline_mode=pl.Buffered(1)),
            pl.BlockSpec((1, D_MODEL), const),
        ],
        out_specs=pl.BlockSpec((tm, D_MODEL), row),
        out_shape=jax.ShapeDtypeStruct((rows_x, D_MODEL), F32),
        scratch_shapes=[pltpu.VMEM((HALO + tm, D_MODEL), BF16)],
        compiler_params=_params(),
        name="conv_ffn",
    )(h_mid, h_mid, h_meta, ffn_norm, wup, conv_w, conv_b, wdn, final_norm)


def _rope_tables(nb):
    half = HEAD_DIM // 2
    inv_freq = ROPE_THETA ** (-np.arange(half, dtype=np.float64) / half)
    pos = (np.arange((nb + 1) * BLK) - META_PAD).astype(np.float64)
    ang = pos[:, None] * inv_freq[None, :]
    cos2 = np.tile(np.cos(ang), (1, LANES // half))
    sin = np.sin(ang)
    sin2 = np.tile(np.concatenate([-sin, sin], axis=-1), (1, LANES // HEAD_DIM))
    return jnp.asarray(np.concatenate([cos2, sin2], axis=1), F32)


def kernel(x, meta_tokens, mix_norm, w_in, b_in, w_alpha, b_alpha, attn_sinks, gla_head_norm,
           w_proj_attn, w_proj_gla, w_out, ffn_norm, w_up, conv_w, conv_b, w_down, final_norm):
    batch, seq, _ = x.shape
    assert batch % N_GROUPS == 0 and seq % BLK == 0
    rows_x = batch * seq
    nb = seq // BLK
    l = 0

    meta_block = jnp.concatenate(
        [jnp.zeros((META_PAD, D_MODEL), x.dtype), meta_tokens.astype(x.dtype)], axis=0)

    lo0 = LOW0
    bi = b_in[l]
    w_in_t = jnp.swapaxes(w_in, 1, 2)
    b_cat = jnp.concatenate(
        [bi[:lo0], bi[lo0 + GLA_RANK:], bi[lo0:lo0 + GLA_RANK],
         jnp.zeros((C_LOW - GLA_RANK,), bi.dtype)])[None, :]
    wal = jnp.concatenate(
        [w_alpha[l], jnp.zeros((C_LOW - GLA_RANK, GLA_DK), w_alpha.dtype)], axis=0)
    wal_hi = wal.astype(BF16)
    wal_lo = (wal - wal_hi.astype(F32)).astype(BF16)

    rope = _rope_tables(nb)
    x_groups = x.reshape(N_GROUPS, rows_x // N_GROUPS, D_MODEL)
    h_mid, h_meta, wup, wdn = _mixer(
        x_groups, meta_block, mix_norm[l][None, :], w_in_t, b_cat, wal_hi, wal_lo,
        b_alpha[l][None, :], attn_sinks[l], rope, gla_head_norm[l][None, :],
        w_proj_attn[l].astype(BF16), w_proj_gla[l].astype(BF16), w_out[l].astype(BF16),
        w_up, w_down, l, nb)

    cw = jnp.concatenate([conv_w[l], jnp.zeros((8 - conv_w.shape[1], D_FF), conv_w.dtype)], axis=0)
    out = _ffn(h_mid.reshape(rows_x, D_MODEL), h_meta, ffn_norm[l][None, :], wup,
               cw, conv_b[l][None, :], wdn, final_norm[None, :], seq,
               tm=1024)
    return out.reshape(batch, seq, D_MODEL)
```

```python
import functools

import numpy as np
import jax
import jax.numpy as jnp
from jax import lax
from jax.experimental import pallas as pl
from jax.experimental.pallas import tpu as pltpu

D_MODEL = 1024
N_META = 16
EPS = 1e-6
ATT_HEADS = 8
ATT_KV_HEADS = 2
HEAD_DIM = 64
BLK = 128
ROPE_THETA = 10000.0
GLA_HEADS = 4
GLA_DK = 256
GLA_DV = 512
GLA_DK_HEAD = GLA_DK // GLA_HEADS
GLA_DV_HEAD = GLA_DV // GLA_HEADS
GLA_RANK = 16
GLA_TAU = 16.0
D_FF = 2816
ATT_Q = ATT_HEADS * HEAD_DIM
ATT_KV = ATT_KV_HEADS * HEAD_DIM

META_PAD = BLK - N_META
LANES = 128
HALO = 16
NEG = -1e30
VMEM_LIMIT = 56 * 1024 * 1024

C_QKV = ATT_Q + 2 * ATT_KV
C_GLA = 2 * GLA_DK + GLA_DV
C_GR = GLA_DV
C_GATE = 2 * D_MODEL
C_LOW = LANES
O_QKV = 0
O_GLA = O_QKV + C_QKV
O_GR = O_GLA + C_GLA
O_GATE = O_GR + C_GR
O_LOW = O_GATE + C_GATE
C_ALL = O_LOW + C_LOW
PROJ_CHUNK = 256

GLA_LEVELS = (64, 32, 16, 8, 4, 2, 1)
N_LEVELS = len(GLA_LEVELS)
GLA_FINE = tuple(h for h in GLA_LEVELS if h < 8)

N_GROUPS = 2
MIXER_STARTS = (6, 0, 14, 8)
FFN_PARTS = 2
FFN_LAG = 1

BF16 = jnp.bfloat16
F32 = jnp.float32


def _sigmoid(x):
    return 1.0 / (1.0 + jnp.exp(-x))


def _dot(a, b):
    return jnp.dot(a, b, preferred_element_type=F32)


def _dot_nt(a, b):
    return lax.dot_general(a, b, (((1,), (1,)), ((), ())), preferred_element_type=F32)


def _split3(x):
    hi = x.astype(BF16)
    r1 = x - hi.astype(F32)
    mid = r1.astype(BF16)
    lo = (r1 - mid.astype(F32)).astype(BF16)
    return hi, mid, lo


def _params():
    return pltpu.CompilerParams(dimension_semantics=("arbitrary",), vmem_limit_bytes=VMEM_LIMIT)


LOW0 = ATT_Q + 2 * ATT_KV + 2 * GLA_DK + 2 * GLA_DV
D_IN = LOW0 + GLA_RANK + C_GATE


def _regroup_w_in(wt_ref, out_ref):
    for t in range(C_ALL // LANES):
        col = t * LANES
        if col < LOW0:
            rows = wt_ref[col:col + LANES, :]
        elif col < O_LOW:
            src = LOW0 + GLA_RANK + (col - O_GATE)
            rows = wt_ref[src:src + LANES, :]
        else:
            rows = jnp.concatenate([wt_ref[LOW0:LOW0 + GLA_RANK, :],
                                    jnp.zeros((C_LOW - GLA_RANK, D_MODEL), F32)], axis=0)
        out_ref[:, col:col + LANES] = rows.T.astype(BF16)


def _inproj_stages(x, weights, store):
    g_ref, w_ref, b_ref, wal_hi_ref, wal_lo_ref, bal_ref = weights
    var = jnp.mean(x * x, axis=-1, keepdims=True)
    u = (x * lax.rsqrt(var + EPS) * g_ref[...]).astype(BF16)
    yield

    def proj(off, width):
        return _dot(u, w_ref[:, off:off + width]) + b_ref[:, off:off + width]

    low = proj(O_LOW, C_LOW)
    yield
    silu = lambda y: y * _sigmoid(y)
    plan = ((0, O_QKV, C_QKV, None), (1, O_GLA, C_GLA, None), (2, O_GR, C_GR, silu),
            (3, O_GATE, C_GATE, _sigmoid))
    low_pending = True
    for i, off, width, act in plan:
        for c in range(0, width, PROJ_CHUNK):
            y = proj(off + c, PROJ_CHUNK)
            store(i, c, c + PROJ_CHUNK, (y if act is None else act(y)).astype(BF16))
            yield
            if low_pending:
                low_pending = False
                low_hi = low.astype(BF16)
                low_lo = (low - low_hi.astype(F32)).astype(BF16)
                z = (_dot(low_hi, wal_hi_ref[...]) + _dot(low_lo, wal_hi_ref[...])
                     + _dot(low_hi, wal_lo_ref[...]) + bal_ref[...])
                log_sig = jnp.minimum(z, 0.0) - jnp.log1p(jnp.exp(-jnp.abs(z)))
                store(4, 0, GLA_DK, log_sig * (1.0 / GLA_TAU))
                yield


def _attn_stages(result, is_meta, first_x, sinks_ref, qkv_ref, rope_ref,
                 kprev, vprev, kmeta, vmeta):
    cos = rope_ref[:, :LANES]
    sin = rope_ref[:, LANES:]
    lane = lax.broadcasted_iota(jnp.int32, (BLK, LANES), 1)
    first_half = (lane & (HEAD_DIM - 1)) < HEAD_DIM // 2

    def rope(xc):
        rot = jnp.where(first_half, pltpu.roll(xc, LANES - HEAD_DIM // 2, 1),
                        pltpu.roll(xc, HEAD_DIM // 2, 1))
        return xc * cos + rot * sin

    k_cur = rope(qkv_ref[:, ATT_Q:ATT_Q + ATT_KV].astype(F32)).astype(BF16)
    v_cur = qkv_ref[:, ATT_Q + ATT_KV:ATT_Q + 2 * ATT_KV]

    q_rows = []
    for hg in range(ATT_HEADS):
        chunk, off, grp = hg // 2, hg % 2, hg // (ATT_HEADS // ATT_KV_HEADS)
        qc = rope(qkv_ref[:, chunk * LANES:(chunk + 1) * LANES].astype(F32))
        if off != grp:
            qc = pltpu.roll(qc, HEAD_DIM, 1)
        on_group = (lane >= grp * HEAD_DIM) & (lane < (grp + 1) * HEAD_DIM)
        q_rows.append(jnp.where(on_group, qc * (HEAD_DIM ** -0.5), 0.0).astype(BF16))
    q_all = jnp.concatenate(q_rows, axis=0)

    k_prev = jnp.where(first_x, kmeta[...], kprev[...])
    v_prev = jnp.where(first_x, vmeta[...], vprev[...])
    kk = jnp.concatenate([k_prev, k_cur], axis=0)
    vv = jnp.concatenate([v_prev, v_cur], axis=0)
    kprev[...] = k_cur
    vprev[...] = v_cur
    kmeta[...] = jnp.where(is_meta, k_cur, kmeta[...])
    vmeta[...] = jnp.where(is_meta, v_cur, vmeta[...])
    yield

    sc = _dot_nt(q_all, kk).reshape(ATT_HEADS, BLK, 2 * BLK)
    yield

    rq = lax.broadcasted_iota(jnp.int32, (BLK, BLK), 0)
    ck = lax.broadcasted_iota(jnp.int32, (BLK, BLK), 1)
    from_cur = ck <= rq
    cur_min = jnp.where(is_meta, META_PAD, 0)
    prev_min = jnp.where(is_meta, BLK, jnp.where(first_x, META_PAD, 0))
    valid = ck >= jnp.where(from_cur, cur_min, prev_min)
    sc = jnp.where(from_cur[None], sc[:, :, BLK:], sc[:, :, :BLK])
    sc = jnp.where(valid[None], sc, NEG)

    hid = lax.broadcasted_iota(jnp.int32, (ATT_HEADS, 1, 1), 0)
    sink = jnp.zeros((ATT_HEADS, 1, 1), F32)
    for hg in range(ATT_HEADS):
        sink = jnp.where(hid == hg, sinks_ref[hg], sink)

    m = jnp.maximum(jnp.max(sc, axis=-1, keepdims=True), sink)
    p = jnp.exp(sc - m)
    denom = jnp.sum(p, axis=-1, keepdims=True) + jnp.exp(sink - m)
    p = jnp.concatenate([jnp.where(from_cur[None], 0.0, p), jnp.where(from_cur[None], p, 0.0)],
                        axis=-1)
    p = p.astype(BF16).reshape(ATT_HEADS * BLK, 2 * BLK)
    yield

    o = _dot(p, vv)
    yield

    o = o.reshape(ATT_HEADS, BLK, LANES) * (1.0 / denom)
    chunks = []
    for chunk in range(ATT_HEADS // 2):
        grp = (2 * chunk) // (ATT_HEADS // ATT_KV_HEADS)
        o0, o1 = o[2 * chunk], o[2 * chunk + 1]
        if grp != 0:
            o0 = pltpu.roll(o0, HEAD_DIM, 1)
        if grp != 1:
            o1 = pltpu.roll(o1, HEAD_DIM, 1)
        chunks.append(jnp.where(lane < HEAD_DIM, o0, o1).astype(BF16))
    result.append(jnp.concatenate(chunks, axis=1))


def _gla_tables():
    t = np.arange(BLK)
    tri = (t[:, None] >= t[None, :]).astype(np.float32)
    sel = np.zeros((len(GLA_FINE) * BLK, BLK), np.float32)
    for li, half in enumerate(GLA_FINE):
        boundary = (t & ~(2 * half - 1)) + half - 1
        sel[li * BLK + t, boundary] = 1.0
    x = t[:, None] ^ t[None, :]
    top = np.floor(np.log2(np.maximum(x, 1))).astype(np.int64)
    lvl = np.array([GLA_LEVELS.index(1 << int(v)) for v in top.ravel()]).reshape(BLK, BLK)
    lvl = np.where(t[:, None] == t[None, :], N_LEVELS, lvl)
    lvl = np.where(t[:, None] < t[None, :], N_LEVELS + 1, lvl)
    lvl = np.tile(lvl, (1, GLA_HEADS))
    return (jnp.asarray(tri, BF16), jnp.asarray(sel, BF16), jnp.asarray(lvl, jnp.int32))


def _gla_stages(result, is_meta, first_x, gla_ref, loga_ref, gr_ref, tri_ref, sel_ref, lvl_ref,
                hn_ref, state, state_meta):
    row = lax.broadcasted_iota(jnp.int32, (BLK, GLA_DK), 0)
    lane = lax.broadcasted_iota(jnp.int32, (BLK, GLA_DK), 1)
    is_token = row >= jnp.where(is_meta, META_PAD, 0)

    g3 = jnp.concatenate(_split3(loga_ref[...]), axis=1)
    yield
    b3 = _dot(tri_ref[...], g3)
    yield
    b = b3[:, :GLA_DK] + b3[:, GLA_DK:2 * GLA_DK] + b3[:, 2 * GLA_DK:]
    b_last = b[BLK - 1:BLK, :]

    b_hi = b.astype(BF16)
    b_mid = (b - b_hi.astype(F32)).astype(BF16)
    fine = _dot(sel_ref[...], jnp.concatenate([b_hi, b_mid], axis=1))

    qf = gla_ref[:, :GLA_DK].astype(F32) * (GLA_DK_HEAD ** -0.5)
    kf = jnp.where(is_token, gla_ref[:, GLA_DK:2 * GLA_DK].astype(F32), 0.0)

    head_mask = [jnp.where((lane >= h * GLA_DK_HEAD) & (lane < (h + 1) * GLA_DK_HEAD), 1.0, 0.0)
                 .astype(BF16) for h in range(GLA_HEADS)]

    def stack_heads(x):
        xb = x.astype(BF16)
        return jnp.concatenate([xb * hm for hm in head_mask], axis=0)

    state_in = jnp.where(first_x, state_meta[...], state[...])
    q_in = stack_heads(qf * jnp.exp(b))
    q_diag, k_diag = qf.astype(BF16), stack_heads(kf)
    yield
    o_inter = _dot(q_in, state_in.astype(BF16))
    lvl = lvl_ref[...]
    a = jnp.where(lvl == N_LEVELS, _dot_nt(q_diag, k_diag), 0.0)
    zeros = lambda shape: jnp.zeros(shape, F32)
    for li, half in enumerate(GLA_LEVELS):
        if half in GLA_FINE:
            fi = GLA_FINE.index(half)
            r = fine[fi * BLK:(fi + 1) * BLK, :GLA_DK] + fine[fi * BLK:(fi + 1) * BLK, GLA_DK:]
            upper = (row & half) != 0
            t_l = jnp.where(upper, qf, kf) * jnp.exp(-jnp.abs(b - r))
            q_l = jnp.where(upper, t_l, 0.0)
            k_l = jnp.where(upper, 0.0, t_l)
        else:
            grp = (BLK // (2 * half), 2 * half, GLA_DK)
            b_g, q_g, k_g = b.reshape(grp), qf.reshape(grp), kf.reshape(grp)
            r = b_g[:, half - 1:half, :]
            q_up = q_g[:, half:, :] * jnp.exp(b_g[:, half:, :] - r)
            k_lo = k_g[:, :half, :] * jnp.exp(r - b_g[:, :half, :])
            q_l = jnp.concatenate([zeros(q_up.shape), q_up], axis=1).reshape(BLK, GLA_DK)
            k_l = jnp.concatenate([k_lo, zeros(k_lo.shape)], axis=1).reshape(BLK, GLA_DK)
        q_l, k_l = q_l.astype(BF16), stack_heads(k_l)
        yield
        a = jnp.where(lvl == li, _dot_nt(q_l, k_l), a)
    a = a.astype(BF16)

    k_out_t = (kf * jnp.exp(b_last - b)).T.astype(BF16)
    decay_t = jnp.broadcast_to(jnp.exp(b_last), (BLK, GLA_DK)).T
    yield

    hn = hn_ref[...]
    new_state, heads = [], []
    for h in range(GLA_HEADS):
        v_h = gla_ref[:, 2 * GLA_DK + h * GLA_DV_HEAD:2 * GLA_DK + (h + 1) * GLA_DV_HEAD]
        o_h = o_inter[h * BLK:(h + 1) * BLK, :] + _dot(a[:, h * BLK:(h + 1) * BLK], v_h)
        new_state.append(_dot(k_out_t[h * GLA_DK_HEAD:(h + 1) * GLA_DK_HEAD, :], v_h))
        heads.append(o_h)
    yield
    for h in range(GLA_HEADS):
        o_h = heads[h]
        var = jnp.mean(o_h * o_h, axis=-1, keepdims=True)
        o_h = o_h * lax.rsqrt(var + EPS) * hn
        gate = gr_ref[:, h * GLA_DV_HEAD:(h + 1) * GLA_DV_HEAD].astype(F32)
        heads[h] = (o_h * gate).astype(BF16)
    state_new = state_in * decay_t + jnp.concatenate(new_state, axis=0)
    state[...] = state_new
    state_meta[...] = jnp.where(is_meta, state_new, state_meta[...])
    result.append(jnp.concatenate(heads, axis=1))


def _merge_stages(result, y_att, y_gla, gate, h, wpa_ref, wpg_ref, wo_ref):
    pa = _dot(y_att, wpa_ref[...])
    pg = _dot(y_gla, wpg_ref[...])
    yield
    mixed = (gate[:, :D_MODEL].astype(F32) * pa + gate[:, D_MODEL:].astype(F32) * pg).astype(BF16)
    yield
    result.append(h + _dot(mixed, wo_ref[...]))


def _advance(gens):
    alive = []
    for gen in gens:
        try:
            next(gen)
            alive.append(gen)
        except StopIteration:
            pass
    return alive


def _mixer_kernel(nb, sinks_ref, xnext_ref, meta_ref, g_ref, w_in_t_ref, b_ref, wal_hi_ref, wal_lo_ref,
                  bal_ref, rope_ref, tri_ref, sel_ref, lvl_ref, hn_ref, wpa_ref, wpg_ref,
                  wo_ref, wup_f32_ref, wdn_f32_ref, out_ref, outm_ref, wup_bf16_ref, wdn_bf16_ref,
                  w_cat, cur_h, cur_qkv, cur_gla, cur_gr, cur_gate, cur_loga,
                  nxt_qkv, nxt_gla, nxt_gr, nxt_gate, nxt_loga,
                  kprev, vprev, kmeta, vmeta, state, state_meta):
    s = pl.program_id(0)
    is_meta = s == 0
    first_x = jnp.logical_and(s >= 1, lax.rem(jnp.maximum(s - 1, 0), nb) == 0)
    cur = (cur_qkv, cur_gla, cur_gr, cur_gate, cur_loga)
    nxt = (nxt_qkv, nxt_gla, nxt_gr, nxt_gate, nxt_loga)
    weights = (g_ref, w_cat, b_ref, wal_hi_ref, wal_lo_ref, bal_ref)
    groups = range(N_GROUPS)

    @pl.when(is_meta)
    def _():
        for ref in (kprev, vprev, kmeta, vmeta, state, state_meta):
            ref[...] = jnp.zeros_like(ref)
        _regroup_w_in(w_in_t_ref, w_cat)

        def store_all_groups(i, c0, c1, val):
            for g in groups:
                cur[i][g, :, c0:c1] = val

        for g in groups:
            cur_h[g] = meta_ref[...]
        meta_proj = [_inproj_stages(meta_ref[...], weights, store_all_groups)]
        while meta_proj:
            meta_proj = _advance(meta_proj)

    def store_next(i, c0, c1, val):
        nxt[i][:, :, c0:c1] = val.reshape(N_GROUPS, BLK, c1 - c0)

    x_next = xnext_ref[...]
    inproj = [_inproj_stages(x_next.reshape(N_GROUPS * BLK, D_MODEL), weights, store_next)]

    y_att, y_gla, h_mid = [], [], []
    mixers = []
    for g in groups:
        mixers.append(_attn_stages(y_att, is_meta, first_x, sinks_ref, cur_qkv.at[g], rope_ref,
                                   kprev.at[g], vprev.at[g], kmeta.at[g], vmeta.at[g]))
        mixers.append(_gla_stages(y_gla, is_meta, first_x, cur_gla.at[g], cur_loga.at[g],
                                  cur_gr.at[g], tri_ref, sel_ref, lvl_ref, hn_ref,
                                  state.at[g], state_meta.at[g]))
    starts = sorted(zip(MIXER_STARTS, range(len(mixers))))
    active, rounds = [], 0
    while active or starts:
        while starts and starts[0][0] <= rounds:
            active.append(mixers[starts.pop(0)[1]])
        inproj = _advance(inproj)
        active = _advance(active)
        rounds += 1

    gate = cur_gate[...].reshape(N_GROUPS * BLK, C_GATE)
    h = cur_h[...].reshape(N_GROUPS * BLK, D_MODEL)
    merge = [_merge_stages(h_mid, jnp.concatenate(y_att, axis=0), jnp.concatenate(y_gla, axis=0),
                           gate, h, wpa_ref, wpg_ref, wo_ref)]
    while merge or inproj:
        merge = _advance(merge)
        inproj = _advance(inproj)

    h_mid = h_mid[0].reshape(N_GROUPS, BLK, D_MODEL)
    out_ref[...] = h_mid

    wup_bf16_ref[...] = wup_f32_ref[...].astype(BF16)
    wdn_bf16_ref[...] = wdn_f32_ref[...].astype(BF16)

    cur_h[...] = x_next
    for c_ref, n_ref in zip(cur, nxt):
        c_ref[...] = n_ref[...]

    @pl.when(is_meta)
    def _():
        outm_ref[...] = out_ref[0]


def _mixer(x_groups, meta_block, mix_norm, w_in_t, b_cat, wal_hi, wal_lo, b_alpha, sinks, rope,
           head_norm, wpa, wpg, wo, wup_f32, wdn_f32, layer, nb):
    rows_g = x_groups.shape[1]
    nblocks_g = rows_g // BLK

    def slab(w, rows, out):
        n = w.shape[1] // rows
        assert w.shape[1] % rows == 0 and n <= nblocks_g
        idx = lambda s: jnp.minimum(jnp.maximum(s - 1, 0), n - 1)
        if out:
            return pl.BlockSpec((rows, w.shape[2]), lambda s: (idx(s), 0))
        return pl.BlockSpec((None, rows, w.shape[2]), lambda s: (layer, idx(s), 0))

    wup_rows, wdn_rows = 16, 64
    tri, sel, lvl = _gla_tables()
    const = lambda s: (0, 0)
    xnext = lambda s: (0, jnp.minimum(s, nblocks_g - 1), 0)
    xout = lambda s: (0, jnp.maximum(s - 1, 0), 0)
    pos = lambda s: (jnp.where(s == 0, 0, lax.rem(jnp.maximum(s - 1, 0), nb) + 1), 0)
    grouped = lambda cols, dt: pltpu.VMEM((N_GROUPS, BLK, cols), dt)
    proj_bufs = [grouped(C_QKV, BF16), grouped(C_GLA, BF16), grouped(C_GR, BF16),
                 grouped(C_GATE, BF16), grouped(GLA_DK, F32)]
    return pl.pallas_call(
        functools.partial(_mixer_kernel, nb),
        grid=(nblocks_g + 1,),
        in_specs=[
            pl.BlockSpec(memory_space=pltpu.SMEM),
            pl.BlockSpec((N_GROUPS, BLK, D_MODEL), xnext),
            pl.BlockSpec((BLK, D_MODEL), const),
            pl.BlockSpec((1, D_MODEL), const),
            pl.BlockSpec((None, D_IN, D_MODEL), lambda s: (layer, 0, 0),
                         pipeline_mode=pl.Buffered(1)),
            pl.BlockSpec((1, C_ALL), const),
            pl.BlockSpec((LANES, GLA_DK), const),
            pl.BlockSpec((LANES, GLA_DK), const),
            pl.BlockSpec((1, GLA_DK), const),
            pl.BlockSpec((BLK, 2 * LANES), pos),
            pl.BlockSpec((BLK, BLK), const),
            pl.BlockSpec((len(GLA_FINE) * BLK, BLK), const),
            pl.BlockSpec((BLK, GLA_HEADS * BLK), const),
            pl.BlockSpec((1, GLA_DV_HEAD), const),
            pl.BlockSpec((ATT_Q, D_MODEL), const),
            pl.BlockSpec((GLA_DV, D_MODEL), const),
            pl.BlockSpec((D_MODEL, D_MODEL), const),
            slab(wup_f32, wup_rows, False),
            slab(wdn_f32, wdn_rows, False),
        ],
        out_specs=[pl.BlockSpec((N_GROUPS, BLK, D_MODEL), xout),
                   pl.BlockSpec((BLK, D_MODEL), const),
                   slab(wup_f32, wup_rows, True), slab(wdn_f32, wdn_rows, True)],
        out_shape=[jax.ShapeDtypeStruct((N_GROUPS, rows_g, D_MODEL), F32),
                   jax.ShapeDtypeStruct((BLK, D_MODEL), F32),
                   jax.ShapeDtypeStruct(wup_f32.shape[1:], BF16),
                   jax.ShapeDtypeStruct(wdn_f32.shape[1:], BF16)],
        scratch_shapes=[pltpu.VMEM((D_MODEL, C_ALL), BF16), grouped(D_MODEL, F32)]
        + proj_bufs + proj_bufs
        + [grouped(ATT_KV, BF16)] * 4
        + [pltpu.VMEM((N_GROUPS, GLA_DK, GLA_DV_HEAD), F32)] * 2,
        compiler_params=_params(),
        name="mixer",
    )(sinks, x_groups, meta_block, mix_norm, w_in_t, b_cat, wal_hi, wal_lo, b_alpha, rope,
      tri, sel, lvl, head_norm, wpa, wpg, wo, wup_f32, wdn_f32)


def _ffn_kernel(tm, tiles_per_seq, h_ref, halo_ref, halo_meta_ref, g_ref, wup_ref, cw_ref,
                cb_ref, wdn_ref, gf_ref, out_ref, u_scr):
    def normed(x):
        var = jnp.mean(x * x, axis=-1, keepdims=True)
        return (x * lax.rsqrt(var + EPS) * g_ref[...]).astype(BF16)

    seq_start = lax.rem(pl.program_id(0), tiles_per_seq) == 0
    u_scr[0:HALO, :] = normed(jnp.where(seq_start, halo_meta_ref[...], halo_ref[...]))
    u_scr[HALO:, :] = normed(h_ref[...])

    part_rows = tm // FFN_PARTS

    def part_stages(pi):
        r0 = pi * part_rows
        u = u_scr[r0:r0 + HALO + part_rows, :]
        a = _dot(u, wup_ref[:, :D_FF])
        yield
        v = _dot(u[HALO:], wup_ref[:, D_FF:])
        yield
        cw = cw_ref[...]
        conv = (cw[0:1] * pltpu.roll(a, 2, 0) + cw[1:2] * pltpu.roll(a, 1, 0) + cw[2:3] * a
                + cb_ref[...])[HALO:]
        inner = 0.7978845608028654 * (conv + 0.044715 * (conv * conv * conv))
        act = conv * (0.5 * (1.0 + jnp.tanh(inner)))
        gated = (act * v).astype(BF16)
        yield
        h2 = h_ref[r0:r0 + part_rows, :] + _dot(gated, wdn_ref[...])
        yield
        var = jnp.mean(h2 * h2, axis=-1, keepdims=True)
        out_ref[r0:r0 + part_rows, :] = h2 * lax.rsqrt(var + EPS) * gf_ref[...]

    waiting = [part_stages(pi) for pi in range(FFN_PARTS)]
    active, rounds = [], 0
    while active or waiting:
        if waiting and rounds % FFN_LAG == 0:
            active.append(waiting.pop(0))
        active = _advance(active)
        rounds += 1


def _ffn(h_mid, h_meta, ffn_norm, wup, conv_w, conv_b, wdn, final_norm, seq, tm):
    rows_x = h_mid.shape[0]
    assert seq % tm == 0 and rows_x % seq == 0 and tm % (FFN_PARTS * HALO) == 0
    tiles_per_seq = seq // tm
    const = lambda i: (0, 0)
    row = lambda i: (i, 0)
    return pl.pallas_call(
        functools.partial(_ffn_kernel, tm, tiles_per_seq),
        grid=(rows_x // tm,),
        in_specs=[
            pl.BlockSpec((tm, D_MODEL), row),
            pl.BlockSpec((HALO, D_MODEL), lambda i: (jnp.maximum(i * (tm // HALO) - 1, 0), 0)),
            pl.BlockSpec((HALO, D_MODEL), lambda i: (BLK // HALO - 1, 0)),
            pl.BlockSpec((1, D_MODEL), const),
            pl.BlockSpec((D_MODEL, 2 * D_FF), const, pipeline_mode=pl.Buffered(1)),
            pl.BlockSpec((8, D_FF), const),
            pl.BlockSpec((1, D_FF), const),
            pl.BlockSpec((D_FF, D_MODEL), const, pipeline_mode=pl.Buffered(1)),
            pl.BlockSpec((1, D_MODEL), const),
        ],
        out_specs=pl.BlockSpec((tm, D_MODEL), row),
        out_shape=jax.ShapeDtypeStruct((rows_x, D_MODEL), F32),
        scratch_shapes=[pltpu.VMEM((HALO + tm, D_MODEL), BF16)],
        compiler_params=_params(),
        name="conv_ffn",
    )(h_mid, h_mid, h_meta, ffn_norm, wup, conv_w, conv_b, wdn, final_norm)


def _rope_tables(nb):
    half = HEAD_DIM // 2
    inv_freq = ROPE_THETA ** (-np.arange(half, dtype=np.float64) / half)
    pos = (np.arange((nb + 1) * BLK) - META_PAD).astype(np.float64)
    ang = pos[:, None] * inv_freq[None, :]
    cos2 = np.tile(np.cos(ang), (1, LANES // half))
    sin = np.sin(ang)
    sin2 = np.tile(np.concatenate([-sin, sin], axis=-1), (1, LANES // HEAD_DIM))
    return jnp.asarray(np.concatenate([cos2, sin2], axis=1), F32)


def kernel(x, meta_tokens, mix_norm, w_in, b_in, w_alpha, b_alpha, attn_sinks, gla_head_norm,
           w_proj_attn, w_proj_gla, w_out, ffn_norm, w_up, conv_w, conv_b, w_down, final_norm):
    batch, seq, _ = x.shape
    assert batch % N_GROUPS == 0 and seq % BLK == 0
    rows_x = batch * seq
    nb = seq // BLK
    l = 0

    meta_block = jnp.concatenate(
        [jnp.zeros((META_PAD, D_MODEL), x.dtype), meta_tokens.astype(x.dtype)], axis=0)

    lo0 = LOW0
    bi = b_in[l]
    w_in_t = jnp.swapaxes(w_in, 1, 2)
    b_cat = jnp.concatenate(
        [bi[:lo0], bi[lo0 + GLA_RANK:], bi[lo0:lo0 + GLA_RANK],
         jnp.zeros((C_LOW - GLA_RANK,), bi.dtype)])[None, :]
    wal = jnp.concatenate(
        [w_alpha[l], jnp.zeros((C_LOW - GLA_RANK, GLA_DK), w_alpha.dtype)], axis=0)
    wal_hi = wal.astype(BF16)
    wal_lo = (wal - wal_hi.astype(F32)).astype(BF16)

    rope = _rope_tables(nb)
    x_groups = x.reshape(N_GROUPS, rows_x // N_GROUPS, D_MODEL)
    h_mid, h_meta, wup, wdn = _mixer(
        x_groups, meta_block, mix_norm[l][None, :], w_in_t, b_cat, wal_hi, wal_lo,
        b_alpha[l][None, :], attn_sinks[l], rope, gla_head_norm[l][None, :],
        w_proj_attn[l].astype(BF16), w_proj_gla[l].astype(BF16), w_out[l].astype(BF16),
        w_up, w_down, l, nb)

    cw = jnp.concatenate([conv_w[l], jnp.zeros((8 - conv_w.shape[1], D_FF), conv_w.dtype)], axis=0)
    out = _ffn(h_mid.reshape(rows_x, D_MODEL), h_meta, ffn_norm[l][None, :], wup,
               cw, conv_b[l][None, :], wdn, final_norm[None, :], seq,
               tm=1024)
    return out.reshape(batch, seq, D_MODEL)
```

```python
import functools

import numpy as np
import jax
import jax.numpy as jnp
from jax import lax
from jax.experimental import pallas as pl
from jax.experimental.pallas import tpu as pltpu

D_MODEL = 1024
N_META = 16
EPS = 1e-6
ATT_HEADS = 8
ATT_KV_HEADS = 2
HEAD_DIM = 64
BLK = 128
ROPE_THETA = 10000.0
GLA_HEADS = 4
GLA_DK = 256
GLA_DV = 512
GLA_DK_HEAD = GLA_DK // GLA_HEADS
GLA_DV_HEAD = GLA_DV // GLA_HEADS
GLA_RANK = 16
GLA_TAU = 16.0
D_FF = 2816
ATT_Q = ATT_HEADS * HEAD_DIM
ATT_KV = ATT_KV_HEADS * HEAD_DIM

META_PAD = BLK - N_META
LANES = 128
HALO = 16
NEG = -1e30
VMEM_LIMIT = 56 * 1024 * 1024

C_QKV = ATT_Q + 2 * ATT_KV
C_GLA = 2 * GLA_DK + GLA_DV
C_GR = GLA_DV
C_GATE = 2 * D_MODEL
C_LOW = LANES
O_QKV = 0
O_GLA = O_QKV + C_QKV
O_GR = O_GLA + C_GLA
O_GATE = O_GR + C_GR
O_LOW = O_GATE + C_GATE
C_ALL = O_LOW + C_LOW
PROJ_CHUNK = 256

GLA_LEVELS = (64, 32, 16, 8, 4, 2, 1)
N_LEVELS = len(GLA_LEVELS)
GLA_FINE = tuple(h for h in GLA_LEVELS if h < 8)

N_GROUPS = 2
MIXER_STARTS = (4, 0, 12, 8)
FFN_PARTS = 2
FFN_LAG = 1

BF16 = jnp.bfloat16
F32 = jnp.float32


def _sigmoid(x):
    return 1.0 / (1.0 + jnp.exp(-x))


def _dot(a, b):
    return jnp.dot(a, b, preferred_element_type=F32)


def _dot_nt(a, b):
    return lax.dot_general(a, b, (((1,), (1,)), ((), ())), preferred_element_type=F32)


def _split3(x):
    hi = x.astype(BF16)
    r1 = x - hi.astype(F32)
    mid = r1.astype(BF16)
    lo = (r1 - mid.astype(F32)).astype(BF16)
    return hi, mid, lo


def _params():
    return pltpu.CompilerParams(dimension_semantics=("arbitrary",), vmem_limit_bytes=VMEM_LIMIT)


LOW0 = ATT_Q + 2 * ATT_KV + 2 * GLA_DK + 2 * GLA_DV
D_IN = LOW0 + GLA_RANK + C_GATE


def _regroup_w_in(wt_ref, out_ref):
    for t in range(C_ALL // LANES):
        col = t * LANES
        if col < LOW0:
            rows = wt_ref[col:col + LANES, :]
        elif col < O_LOW:
            src = LOW0 + GLA_RANK + (col - O_GATE)
            rows = wt_ref[src:src + LANES, :]
        else:
            rows = jnp.concatenate([wt_ref[LOW0:LOW0 + GLA_RANK, :],
                                    jnp.zeros((C_LOW - GLA_RANK, D_MODEL), F32)], axis=0)
        out_ref[:, col:col + LANES] = rows.T.astype(BF16)


def _inproj_stages(x, weights, store):
    g_ref, w_ref, b_ref, wal_hi_ref, wal_lo_ref, bal_ref = weights
    var = jnp.mean(x * x, axis=-1, keepdims=True)
    u = (x * lax.rsqrt(var + EPS) * g_ref[...]).astype(BF16)
    yield

    def proj(off, width):
        return _dot(u, w_ref[:, off:off + width]) + b_ref[:, off:off + width]

    low = proj(O_LOW, C_LOW)
    yield
    silu = lambda y: y * _sigmoid(y)
    plan = ((0, O_QKV, C_QKV, None), (1, O_GLA, C_GLA, None), (2, O_GR, C_GR, silu),
            (3, O_GATE, C_GATE, _sigmoid))
    low_pending = True
    for i, off, width, act in plan:
        for c in range(0, width, PROJ_CHUNK):
            y = proj(off + c, PROJ_CHUNK)
            store(i, c, c + PROJ_CHUNK, (y if act is None else act(y)).astype(BF16))
            yield
            if low_pending:
                low_pending = False
                low_hi = low.astype(BF16)
                low_lo = (low - low_hi.astype(F32)).astype(BF16)
                z = (_dot(low_hi, wal_hi_ref[...]) + _dot(low_lo, wal_hi_ref[...])
                     + _dot(low_hi, wal_lo_ref[...]) + bal_ref[...])
                log_sig = jnp.minimum(z, 0.0) - jnp.log1p(jnp.exp(-jnp.abs(z)))
                store(4, 0, GLA_DK, log_sig * (1.0 / GLA_TAU))
                yield


def _attn_stages(result, is_meta, first_x, sinks_ref, qkv_ref, rope_ref,
                 kprev, vprev, kmeta, vmeta):
    cos = rope_ref[:, :LANES]
    sin = rope_ref[:, LANES:]
    lane = lax.broadcasted_iota(jnp.int32, (BLK, LANES), 1)
    first_half = (lane & (HEAD_DIM - 1)) < HEAD_DIM // 2

    def rope(xc):
        rot = jnp.where(first_half, pltpu.roll(xc, LANES - HEAD_DIM // 2, 1),
                        pltpu.roll(xc, HEAD_DIM // 2, 1))
        return xc * cos + rot * sin

    k_cur = rope(qkv_ref[:, ATT_Q:ATT_Q + ATT_KV].astype(F32)).astype(BF16)
    v_cur = qkv_ref[:, ATT_Q + ATT_KV:ATT_Q + 2 * ATT_KV]

    q_rows = []
    for hg in range(ATT_HEADS):
        chunk, off, grp = hg // 2, hg % 2, hg // (ATT_HEADS // ATT_KV_HEADS)
        qc = rope(qkv_ref[:, chunk * LANES:(chunk + 1) * LANES].astype(F32))
        if off != grp:
            qc = pltpu.roll(qc, HEAD_DIM, 1)
        on_group = (lane >= grp * HEAD_DIM) & (lane < (grp + 1) * HEAD_DIM)
        q_rows.append(jnp.where(on_group, qc * (HEAD_DIM ** -0.5), 0.0).astype(BF16))
    q_all = jnp.concatenate(q_rows, axis=0)

    k_prev = jnp.where(first_x, kmeta[...], kprev[...])
    v_prev = jnp.where(first_x, vmeta[...], vprev[...])
    kk = jnp.concatenate([k_prev, k_cur], axis=0)
    vv = jnp.concatenate([v_prev, v_cur], axis=0)
    kprev[...] = k_cur
    vprev[...] = v_cur
    kmeta[...] = jnp.where(is_meta, k_cur, kmeta[...])
    vmeta[...] = jnp.where(is_meta, v_cur, vmeta[...])
    yield

    sc = _dot_nt(q_all, kk).reshape(ATT_HEADS, BLK, 2 * BLK)
    yield

    rq = lax.broadcasted_iota(jnp.int32, (BLK, BLK), 0)
    ck = lax.broadcasted_iota(jnp.int32, (BLK, BLK), 1)
    from_cur = ck <= rq
    cur_min = jnp.where(is_meta, META_PAD, 0)
    prev_min = jnp.where(is_meta, BLK, jnp.where(first_x, META_PAD, 0))
    valid = ck >= jnp.where(from_cur, cur_min, prev_min)
    sc = jnp.where(from_cur[None], sc[:, :, BLK:], sc[:, :, :BLK])
    sc = jnp.where(valid[None], sc, NEG)

    hid = lax.broadcasted_iota(jnp.int32, (ATT_HEADS, 1, 1), 0)
    sink = jnp.zeros((ATT_HEADS, 1, 1), F32)
    for hg in range(ATT_HEADS):
        sink = jnp.where(hid == hg, sinks_ref[hg], sink)

    m = jnp.maximum(jnp.max(sc, axis=-1, keepdims=True), sink)
    p = jnp.exp(sc - m)
    denom = jnp.sum(p, axis=-1, keepdims=True) + jnp.exp(sink - m)
    p = jnp.concatenate([jnp.where(from_cur[None], 0.0, p), jnp.where(from_cur[None], p, 0.0)],
                        axis=-1)
    p = p.astype(BF16).reshape(ATT_HEADS * BLK, 2 * BLK)
    yield

    o = _dot(p, vv)
    yield

    o = o.reshape(ATT_HEADS, BLK, LANES) * (1.0 / denom)
    chunks = []
    for chunk in range(ATT_HEADS // 2):
        grp = (2 * chunk) // (ATT_HEADS // ATT_KV_HEADS)
        o0, o1 = o[2 * chunk], o[2 * chunk + 1]
        if grp != 0:
            o0 = pltpu.roll(o0, HEAD_DIM, 1)
        if grp != 1:
            o1 = pltpu.roll(o1, HEAD_DIM, 1)
        chunks.append(jnp.where(lane < HEAD_DIM, o0, o1).astype(BF16))
    result.append(jnp.concatenate(chunks, axis=1))


def _gla_tables():
    t = np.arange(BLK)
    tri = (t[:, None] >= t[None, :]).astype(np.float32)
    sel = np.zeros((len(GLA_FINE) * BLK, BLK), np.float32)
    for li, half in enumerate(GLA_FINE):
        boundary = (t & ~(2 * half - 1)) + half - 1
        sel[li * BLK + t, boundary] = 1.0
    x = t[:, None] ^ t[None, :]
    top = np.floor(np.log2(np.maximum(x, 1))).astype(np.int64)
    lvl = np.array([GLA_LEVELS.index(1 << int(v)) for v in top.ravel()]).reshape(BLK, BLK)
    lvl = np.where(t[:, None] == t[None, :], N_LEVELS, lvl)
    lvl = np.where(t[:, None] < t[None, :], N_LEVELS + 1, lvl)
    lvl = np.tile(lvl, (1, GLA_HEADS))
    return (jnp.asarray(tri, BF16), jnp.asarray(sel, BF16), jnp.asarray(lvl, jnp.int32))


def _gla_stages(result, is_meta, first_x, gla_ref, loga_ref, gr_ref, tri_ref, sel_ref, lvl_ref,
                hn_ref, state, state_meta):
    row = lax.broadcasted_iota(jnp.int32, (BLK, GLA_DK), 0)
    lane = lax.broadcasted_iota(jnp.int32, (BLK, GLA_DK), 1)
    is_token = row >= jnp.where(is_meta, META_PAD, 0)

    g3 = jnp.concatenate(_split3(loga_ref[...]), axis=1)
    yield
    b3 = _dot(tri_ref[...], g3)
    yield
    b = b3[:, :GLA_DK] + b3[:, GLA_DK:2 * GLA_DK] + b3[:, 2 * GLA_DK:]
    b_last = b[BLK - 1:BLK, :]

    b_hi = b.astype(BF16)
    b_mid = (b - b_hi.astype(F32)).astype(BF16)
    fine = _dot(sel_ref[...], jnp.concatenate([b_hi, b_mid], axis=1))

    qf = gla_ref[:, :GLA_DK].astype(F32) * (GLA_DK_HEAD ** -0.5)
    kf = jnp.where(is_token, gla_ref[:, GLA_DK:2 * GLA_DK].astype(F32), 0.0)

    head_mask = [jnp.where((lane >= h * GLA_DK_HEAD) & (lane < (h + 1) * GLA_DK_HEAD), 1.0, 0.0)
                 .astype(BF16) for h in range(GLA_HEADS)]

    def stack_heads(x):
        xb = x.astype(BF16)
        return jnp.concatenate([xb * hm for hm in head_mask], axis=0)

    state_in = jnp.where(first_x, state_meta[...], state[...])
    q_in = stack_heads(qf * jnp.exp(b))
    q_diag, k_diag = qf.astype(BF16), stack_heads(kf)
    yield
    o_inter = _dot(q_in, state_in.astype(BF16))
    lvl = lvl_ref[...]
    a = jnp.where(lvl == N_LEVELS, _dot_nt(q_diag, k_diag), 0.0)
    zeros = lambda shape: jnp.zeros(shape, F32)
    for li, half in enumerate(GLA_LEVELS):
        if half in GLA_FINE:
            fi = GLA_FINE.index(half)
            r = fine[fi * BLK:(fi + 1) * BLK, :GLA_DK] + fine[fi * BLK:(fi + 1) * BLK, GLA_DK:]
            upper = (row & half) != 0
            t_l = jnp.where(upper, qf, kf) * jnp.exp(-jnp.abs(b - r))
            q_l = jnp.where(upper, t_l, 0.0)
            k_l = jnp.where(upper, 0.0, t_l)
        else:
            grp = (BLK // (2 * half), 2 * half, GLA_DK)
            b_g, q_g, k_g = b.reshape(grp), qf.reshape(grp), kf.reshape(grp)
            r = b_g[:, half - 1:half, :]
            q_up = q_g[:, half:, :] * jnp.exp(b_g[:, half:, :] - r)
            k_lo = k_g[:, :half, :] * jnp.exp(r - b_g[:, :half, :])
            q_l = jnp.concatenate([zeros(q_up.shape), q_up], axis=1).reshape(BLK, GLA_DK)
            k_l = jnp.concatenate([k_lo, zeros(k_lo.shape)], axis=1).reshape(BLK, GLA_DK)
        q_l, k_l = q_l.astype(BF16), stack_heads(k_l)
        yield
        a = jnp.where(lvl == li, _dot_nt(q_l, k_l), a)
    a = a.astype(BF16)

    k_out_t = (kf * jnp.exp(b_last - b)).T.astype(BF16)
    decay_t = jnp.broadcast_to(jnp.exp(b_last), (BLK, GLA_DK)).T
    yield

    hn = hn_ref[...]
    new_state, heads = [], []
    for h in range(GLA_HEADS):
        v_h = gla_ref[:, 2 * GLA_DK + h * GLA_DV_HEAD:2 * GLA_DK + (h + 1) * GLA_DV_HEAD]
        o_h = o_inter[h * BLK:(h + 1) * BLK, :] + _dot(a[:, h * BLK:(h + 1) * BLK], v_h)
        new_state.append(_dot(k_out_t[h * GLA_DK_HEAD:(h + 1) * GLA_DK_HEAD, :], v_h))
        heads.append(o_h)
    yield
    for h in range(GLA_HEADS):
        o_h = heads[h]
        var = jnp.mean(o_h * o_h, axis=-1, keepdims=True)
        o_h = o_h * lax.rsqrt(var + EPS) * hn
        gate = gr_ref[:, h * GLA_DV_HEAD:(h + 1) * GLA_DV_HEAD].astype(F32)
        heads[h] = (o_h * gate).astype(BF16)
    state_new = state_in * decay_t + jnp.concatenate(new_state, axis=0)
    state[...] = state_new
    state_meta[...] = jnp.where(is_meta, state_new, state_meta[...])
    result.append(jnp.concatenate(heads, axis=1))


def _merge_stages(result, y_att, y_gla, gate, h, wpa_ref, wpg_ref, wo_ref):
    pa = _dot(y_att, wpa_ref[...])
    pg = _dot(y_gla, wpg_ref[...])
    yield
    mixed = (gate[:, :D_MODEL].astype(F32) * pa + gate[:, D_MODEL:].astype(F32) * pg).astype(BF16)
    yield
    result.append(h + _dot(mixed, wo_ref[...]))


def _advance(gens):
    alive = []
    for gen in gens:
        try:
            next(gen)
            alive.append(gen)
        except StopIteration:
            pass
    return alive


def _mixer_kernel(nb, sinks_ref, xnext_ref, meta_ref, g_ref, w_in_t_ref, b_ref, wal_hi_ref, wal_lo_ref,
                  bal_ref, rope_ref, tri_ref, sel_ref, lvl_ref, hn_ref, wpa_ref, wpg_ref,
                  wo_ref, wup_f32_ref, wdn_f32_ref, out_ref, outm_ref, wup_bf16_ref, wdn_bf16_ref,
                  w_cat, cur_h, cur_qkv, cur_gla, cur_gr, cur_gate, cur_loga,
                  nxt_qkv, nxt_gla, nxt_gr, nxt_gate, nxt_loga,
                  kprev, vprev, kmeta, vmeta, state, state_meta):
    s = pl.program_id(0)
    is_meta = s == 0
    first_x = jnp.logical_and(s >= 1, lax.rem(jnp.maximum(s - 1, 0), nb) == 0)
    cur = (cur_qkv, cur_gla, cur_gr, cur_gate, cur_loga)
    nxt = (nxt_qkv, nxt_gla, nxt_gr, nxt_gate, nxt_loga)
    weights = (g_ref, w_cat, b_ref, wal_hi_ref, wal_lo_ref, bal_ref)
    groups = range(N_GROUPS)

    @pl.when(is_meta)
    def _():
        for ref in (kprev, vprev, kmeta, vmeta, state, state_meta):
            ref[...] = jnp.zeros_like(ref)
        _regroup_w_in(w_in_t_ref, w_cat)

        def store_all_groups(i, c0, c1, val):
            for g in groups:
                cur[i][g, :, c0:c1] = val

        for g in groups:
            cur_h[g] = meta_ref[...]
        meta_proj = [_inproj_stages(meta_ref[...], weights, store_all_groups)]
        while meta_proj:
            meta_proj = _advance(meta_proj)

    def store_next(i, c0, c1, val):
        nxt[i][:, :, c0:c1] = val.reshape(N_GROUPS, BLK, c1 - c0)

    x_next = xnext_ref[...]
    inproj = [_inproj_stages(x_next.reshape(N_GROUPS * BLK, D_MODEL), weights, store_next)]

    y_att, y_gla, h_mid = [], [], []
    mixers = []
    for g in groups:
        mixers.append(_attn_stages(y_att, is_meta, first_x, sinks_ref, cur_qkv.at[g], rope_ref,
                                   kprev.at[g], vprev.at[g], kmeta.at[g], vmeta.at[g]))
        mixers.append(_gla_stages(y_gla, is_meta, first_x, cur_gla.at[g], cur_loga.at[g],
                                  cur_gr.at[g], tri_ref, sel_ref, lvl_ref, hn_ref,
                                  state.at[g], state_meta.at[g]))
    starts = sorted(zip(MIXER_STARTS, range(len(mixers))))
    active, rounds = [], 0
    while active or starts:
        while starts and starts[0][0] <= rounds:
            active.append(mixers[starts.pop(0)[1]])
        inproj = _advance(inproj)
        active = _advance(active)
        rounds += 1

    gate = cur_gate[...].reshape(N_GROUPS * BLK, C_GATE)
    h = cur_h[...].reshape(N_GROUPS * BLK, D_MODEL)
    merge = [_merge_stages(h_mid, jnp.concatenate(y_att, axis=0), jnp.concatenate(y_gla, axis=0),
                           gate, h, wpa_ref, wpg_ref, wo_ref)]
    while merge or inproj:
        merge = _advance(merge)
        inproj = _advance(inproj)

    h_mid = h_mid[0].reshape(N_GROUPS, BLK, D_MODEL)
    out_ref[...] = h_mid

    wup_bf16_ref[...] = wup_f32_ref[...].astype(BF16)
    wdn_bf16_ref[...] = wdn_f32_ref[...].astype(BF16)

    cur_h[...] = x_next
    for c_ref, n_ref in zip(cur, nxt):
        c_ref[...] = n_ref[...]

    @pl.when(is_meta)
    def _():
        outm_ref[...] = out_ref[0]


def _mixer(x_groups, meta_block, mix_norm, w_in_t, b_cat, wal_hi, wal_lo, b_alpha, sinks, rope,
           head_norm, wpa, wpg, wo, wup_f32, wdn_f32, layer, nb):
    rows_g = x_groups.shape[1]
    nblocks_g = rows_g // BLK

    def slab(w, rows, out):
        n = w.shape[1] // rows
        assert w.shape[1] % rows == 0 and n <= nblocks_g
        idx = lambda s: jnp.minimum(jnp.maximum(s - 1, 0), n - 1)
        if out:
            return pl.BlockSpec((rows, w.shape[2]), lambda s: (idx(s), 0))
        return pl.BlockSpec((None, rows, w.shape[2]), lambda s: (layer, idx(s), 0))

    wup_rows, wdn_rows = 16, 64
    tri, sel, lvl = _gla_tables()
    const = lambda s: (0, 0)
    xnext = lambda s: (0, jnp.minimum(s, nblocks_g - 1), 0)
    xout = lambda s: (0, jnp.maximum(s - 1, 0), 0)
    pos = lambda s: (jnp.where(s == 0, 0, lax.rem(jnp.maximum(s - 1, 0), nb) + 1), 0)
    grouped = lambda cols, dt: pltpu.VMEM((N_GROUPS, BLK, cols), dt)
    proj_bufs = [grouped(C_QKV, BF16), grouped(C_GLA, BF16), grouped(C_GR, BF16),
                 grouped(C_GATE, BF16), grouped(GLA_DK, F32)]
    return pl.pallas_call(
        functools.partial(_mixer_kernel, nb),
        grid=(nblocks_g + 1,),
        in_specs=[
            pl.BlockSpec(memory_space=pltpu.SMEM),
            pl.BlockSpec((N_GROUPS, BLK, D_MODEL), xnext),
            pl.BlockSpec((BLK, D_MODEL), const),
            pl.BlockSpec((1, D_MODEL), const),
            pl.BlockSpec((None, D_IN, D_MODEL), lambda s: (layer, 0, 0),
                         pipeline_mode=pl.Buffered(1)),
            pl.BlockSpec((1, C_ALL), const),
            pl.BlockSpec((LANES, GLA_DK), const),
            pl.BlockSpec((LANES, GLA_DK), const),
            pl.BlockSpec((1, GLA_DK), const),
            pl.BlockSpec((BLK, 2 * LANES), pos),
            pl.BlockSpec((BLK, BLK), const),
            pl.BlockSpec((len(GLA_FINE) * BLK, BLK), const),
            pl.BlockSpec((BLK, GLA_HEADS * BLK), const),
            pl.BlockSpec((1, GLA_DV_HEAD), const),
            pl.BlockSpec((ATT_Q, D_MODEL), const),
            pl.BlockSpec((GLA_DV, D_MODEL), const),
            pl.BlockSpec((D_MODEL, D_MODEL), const),
            slab(wup_f32, wup_rows, False),
            slab(wdn_f32, wdn_rows, False),
        ],
        out_specs=[pl.BlockSpec((N_GROUPS, BLK, D_MODEL), xout),
                   pl.BlockSpec((BLK, D_MODEL), const),
                   slab(wup_f32, wup_rows, True), slab(wdn_f32, wdn_rows, True)],
        out_shape=[jax.ShapeDtypeStruct((N_GROUPS, rows_g, D_MODEL), F32),
                   jax.ShapeDtypeStruct((BLK, D_MODEL), F32),
                   jax.ShapeDtypeStruct(wup_f32.shape[1:], BF16),
                   jax.ShapeDtypeStruct(wdn_f32.shape[1:], BF16)],
        scratch_shapes=[pltpu.VMEM((D_MODEL, C_ALL), BF16), grouped(D_MODEL, F32)]
        + proj_bufs + proj_bufs
        + [grouped(ATT_KV, BF16)] * 4
        + [pltpu.VMEM((N_GROUPS, GLA_DK, GLA_DV_HEAD), F32)] * 2,
        compiler_params=_params(),
        name="mixer",
    )(sinks, x_groups, meta_block, mix_norm, w_in_t, b_cat, wal_hi, wal_lo, b_alpha, rope,
      tri, sel, lvl, head_norm, wpa, wpg, wo, wup_f32, wdn_f32)


def _ffn_kernel(tm, tiles_per_seq, h_ref, halo_ref, halo_meta_ref, g_ref, wup_ref, cw_ref,
                cb_ref, wdn_ref, gf_ref, out_ref, u_scr):
    def normed(x):
        var = jnp.mean(x * x, axis=-1, keepdims=True)
        return (x * lax.rsqrt(var + EPS) * g_ref[...]).astype(BF16)

    seq_start = lax.rem(pl.program_id(0), tiles_per_seq) == 0
    u_scr[0:HALO, :] = normed(jnp.where(seq_start, halo_meta_ref[...], halo_ref[...]))
    u_scr[HALO:, :] = normed(h_ref[...])

    part_rows = tm // FFN_PARTS

    def part_stages(pi):
        r0 = pi * part_rows
        u = u_scr[r0:r0 + HALO + part_rows, :]
        a = _dot(u, wup_ref[:, :D_FF])
        yield
        v = _dot(u[HALO:], wup_ref[:, D_FF:])
        yield
        cw = cw_ref[...]
        conv = (cw[0:1] * pltpu.roll(a, 2, 0) + cw[1:2] * pltpu.roll(a, 1, 0) + cw[2:3] * a
                + cb_ref[...])[HALO:]
        inner = 0.7978845608028654 * (conv + 0.044715 * (conv * conv * conv))
        act = conv * (0.5 * (1.0 + jnp.tanh(inner)))
        gated = (act * v).astype(BF16)
        yield
        h2 = h_ref[r0:r0 + part_rows, :] + _dot(gated, wdn_ref[...])
        yield
        var = jnp.mean(h2 * h2, axis=-1, keepdims=True)
        out_ref[r0:r0 + part_rows, :] = h2 * lax.rsqrt(var + EPS) * gf_ref[...]

    waiting = [part_stages(pi) for pi in range(FFN_PARTS)]
    active, rounds = [], 0
    while active or waiting:
        if waiting and rounds % FFN_LAG == 0:
            active.append(waiting.pop(0))
        active = _advance(active)
        rounds += 1


def _ffn(h_mid, h_meta, ffn_norm, wup, conv_w, conv_b, wdn, final_norm, seq, tm):
    rows_x = h_mid.shape[0]
    assert seq % tm == 0 and rows_x % seq == 0 and tm % (FFN_PARTS * HALO) == 0
    tiles_per_seq = seq // tm
    const = lambda i: (0, 0)
    row = lambda i: (i, 0)
    return pl.pallas_call(
        functools.partial(_ffn_kernel, tm, tiles_per_seq),
        grid=(rows_x // tm,),
        in_specs=[
            pl.BlockSpec((tm, D_MODEL), row),
            pl.BlockSpec((HALO, D_MODEL), lambda i: (jnp.maximum(i * (tm // HALO) - 1, 0), 0)),
            pl.BlockSpec((HALO, D_MODEL), lambda i: (BLK // HALO - 1, 0)),
            pl.BlockSpec((1, D_MODEL), const),
            pl.BlockSpec((D_MODEL, 2 * D_FF), const, pipeline_mode=pl.Buffered(1)),
            pl.BlockSpec((8, D_FF), const),
            pl.BlockSpec((1, D_FF), const),
            pl.BlockSpec((D_FF, D_MODEL), const, pipeline_mode=pl.Buffered(1)),
            pl.BlockSpec((1, D_MODEL), const),
        ],
        out_specs=pl.BlockSpec((tm, D_MODEL), row),
        out_shape=jax.ShapeDtypeStruct((rows_x, D_MODEL), F32),
        scratch_shapes=[pltpu.VMEM((HALO + tm, D_MODEL), BF16)],
        compiler_params=_params(),
        name="conv_ffn",
    )(h_mid, h_mid, h_meta, ffn_norm, wup, conv_w, conv_b, wdn, final_norm)


def _rope_tables(nb):
    half = HEAD_DIM // 2
    inv_freq = ROPE_THETA ** (-np.arange(half, dtype=np.float64) / half)
    pos = (np.arange((nb + 1) * BLK) - META_PAD).astype(np.float64)
    ang = pos[:, None] * inv_freq[None, :]
    cos2 = np.tile(np.cos(ang), (1, LANES // half))
    sin = np.sin(ang)
    sin2 = np.tile(np.concatenate([-sin, sin], axis=-1), (1, LANES // HEAD_DIM))
    return jnp.asarray(np.concatenate([cos2, sin2], axis=1), F32)


def kernel(x, meta_tokens, mix_norm, w_in, b_in, w_alpha, b_alpha, attn_sinks, gla_head_norm,
           w_proj_attn, w_proj_gla, w_out, ffn_norm, w_up, conv_w, conv_b, w_down, final_norm):
    batch, seq, _ = x.shape
    assert batch % N_GROUPS == 0 and seq % BLK == 0
    rows_x = batch * seq
    nb = seq // BLK
    l = 0

    meta_block = jnp.concatenate(
        [jnp.zeros((META_PAD, D_MODEL), x.dtype), meta_tokens.astype(x.dtype)], axis=0)

    w_in_t = jnp.swapaxes(w_in, 1, 2)
    bi = b_in[l]
    b_cat = jnp.concatenate(
        [bi[:LOW0], bi[LOW0 + GLA_RANK:], bi[LOW0:LOW0 + GLA_RANK],
         jnp.zeros((C_LOW - GLA_RANK,), bi.dtype)])[None, :]
    wal = jnp.concatenate(
        [w_alpha[l], jnp.zeros((C_LOW - GLA_RANK, GLA_DK), w_alpha.dtype)], axis=0)
    wal_hi = wal.astype(BF16)
    wal_lo = (wal - wal_hi.astype(F32)).astype(BF16)

    rope = _rope_tables(nb)
    x_groups = x.reshape(N_GROUPS, rows_x // N_GROUPS, D_MODEL)
    h_mid, h_meta, wup, wdn = _mixer(
        x_groups, meta_block, mix_norm[l][None, :], w_in_t, b_cat, wal_hi, wal_lo,
        b_alpha[l][None, :], attn_sinks[l], rope, gla_head_norm[l][None, :],
        w_proj_attn[l].astype(BF16), w_proj_gla[l].astype(BF16), w_out[l].astype(BF16),
        w_up, w_down, l, nb)

    cw = jnp.concatenate([conv_w[l], jnp.zeros((8 - conv_w.shape[1], D_FF), conv_w.dtype)], axis=0)
    out = _ffn(h_mid.reshape(rows_x, D_MODEL), h_meta, ffn_norm[l][None, :], wup,
               cw, conv_b[l][None, :], wdn, final_norm[None, :], seq,
               tm=1024)
    return out.reshape(batch, seq, D_MODEL)
```

```python
import functools

import numpy as np
import jax
import jax.numpy as jnp
from jax import lax
from jax.experimental import pallas as pl
from jax.experimental.pallas import tpu as pltpu

D_MODEL = 1024
N_META = 16
EPS = 1e-6
ATT_HEADS = 8
ATT_KV_HEADS = 2
HEAD_DIM = 64
BLK = 128
ROPE_THETA = 10000.0
GLA_HEADS = 4
GLA_DK = 256
GLA_DV = 512
GLA_DK_HEAD = GLA_DK // GLA_HEADS
GLA_DV_HEAD = GLA_DV // GLA_HEADS
GLA_RANK = 16
GLA_TAU = 16.0
D_FF = 2816
ATT_Q = ATT_HEADS * HEAD_DIM
ATT_KV = ATT_KV_HEADS * HEAD_DIM

META_PAD = BLK - N_META
LANES = 128
HALO = 16
NEG = -1e30
VMEM_LIMIT = 56 * 1024 * 1024

C_QKV = ATT_Q + 2 * ATT_KV
C_GLA = 2 * GLA_DK + GLA_DV
C_GR = GLA_DV
C_GATE = 2 * D_MODEL
C_LOW = LANES
O_QKV = 0
O_GLA = O_QKV + C_QKV
O_GR = O_GLA + C_GLA
O_GATE = O_GR + C_GR
O_LOW = O_GATE + C_GATE
C_ALL = O_LOW + C_LOW
PROJ_CHUNK = 256

GLA_LEVELS = (64, 32, 16, 8, 4, 2, 1)
N_LEVELS = len(GLA_LEVELS)
GLA_FINE = tuple(h for h in GLA_LEVELS if h < 8)

N_GROUPS = 2
MIXER_STARTS = (0, 0, 8, 8)
FFN_PARTS = 2
FFN_LAG = 1

BF16 = jnp.bfloat16
F32 = jnp.float32


def _sigmoid(x):
    return 1.0 / (1.0 + jnp.exp(-x))


def _dot(a, b):
    return jnp.dot(a, b, preferred_element_type=F32)


def _dot_nt(a, b):
    return lax.dot_general(a, b, (((1,), (1,)), ((), ())), preferred_element_type=F32)


def _split3(x):
    hi = x.astype(BF16)
    r1 = x - hi.astype(F32)
    mid = r1.astype(BF16)
    lo = (r1 - mid.astype(F32)).astype(BF16)
    return hi, mid, lo


def _params():
    return pltpu.CompilerParams(dimension_semantics=("arbitrary",), vmem_limit_bytes=VMEM_LIMIT)


LOW0 = ATT_Q + 2 * ATT_KV + 2 * GLA_DK + 2 * GLA_DV
D_IN = LOW0 + GLA_RANK + C_GATE


def _regroup_w_in(wt_ref, out_ref):
    for t in range(C_ALL // LANES):
        col = t * LANES
        if col < LOW0:
            rows = wt_ref[col:col + LANES, :]
        elif col < O_LOW:
            src = LOW0 + GLA_RANK + (col - O_GATE)
            rows = wt_ref[src:src + LANES, :]
        else:
            rows = jnp.concatenate([wt_ref[LOW0:LOW0 + GLA_RANK, :],
                                    jnp.zeros((C_LOW - GLA_RANK, D_MODEL), F32)], axis=0)
        out_ref[:, col:col + LANES] = rows.T.astype(BF16)


def _inproj_stages(x, weights, store):
    g_ref, w_ref, b_ref, wal_hi_ref, wal_lo_ref, bal_ref = weights
    var = jnp.mean(x * x, axis=-1, keepdims=True)
    u = (x * lax.rsqrt(var + EPS) * g_ref[...]).astype(BF16)
    yield

    def proj(off, width):
        return _dot(u, w_ref[:, off:off + width]) + b_ref[:, off:off + width]

    low = proj(O_LOW, C_LOW)
    yield
    silu = lambda y: y * _sigmoid(y)
    plan = ((0, O_QKV, C_QKV, None), (1, O_GLA, C_GLA, None), (2, O_GR, C_GR, silu),
            (3, O_GATE, C_GATE, _sigmoid))
    low_pending = True
    for i, off, width, act in plan:
        for c in range(0, width, PROJ_CHUNK):
            y = proj(off + c, PROJ_CHUNK)
            store(i, c, c + PROJ_CHUNK, (y if act is None else act(y)).astype(BF16))
            yield
            if low_pending:
                low_pending = False
                low_hi = low.astype(BF16)
                low_lo = (low - low_hi.astype(F32)).astype(BF16)
                z = (_dot(low_hi, wal_hi_ref[...]) + _dot(low_lo, wal_hi_ref[...])
                     + _dot(low_hi, wal_lo_ref[...]) + bal_ref[...])
                log_sig = jnp.minimum(z, 0.0) - jnp.log1p(jnp.exp(-jnp.abs(z)))
                store(4, 0, GLA_DK, log_sig * (1.0 / GLA_TAU))
                yield


def _attn_stages(result, is_meta, first_x, sinks_ref, qkv_ref, rope_ref,
                 kprev, vprev, kmeta, vmeta):
    cos = rope_ref[:, :LANES]
    sin = rope_ref[:, LANES:]
    lane = lax.broadcasted_iota(jnp.int32, (BLK, LANES), 1)
    first_half = (lane & (HEAD_DIM - 1)) < HEAD_DIM // 2

    def rope(xc):
        rot = jnp.where(first_half, pltpu.roll(xc, LANES - HEAD_DIM // 2, 1),
                        pltpu.roll(xc, HEAD_DIM // 2, 1))
        return xc * cos + rot * sin

    k_cur = rope(qkv_ref[:, ATT_Q:ATT_Q + ATT_KV].astype(F32)).astype(BF16)
    v_cur = qkv_ref[:, ATT_Q + ATT_KV:ATT_Q + 2 * ATT_KV]

    q_rows = []
    for hg in range(ATT_HEADS):
        chunk, off, grp = hg // 2, hg % 2, hg // (ATT_HEADS // ATT_KV_HEADS)
        qc = rope(qkv_ref[:, chunk * LANES:(chunk + 1) * LANES].astype(F32))
        if off != grp:
            qc = pltpu.roll(qc, HEAD_DIM, 1)
        on_group = (lane >= grp * HEAD_DIM) & (lane < (grp + 1) * HEAD_DIM)
        q_rows.append(jnp.where(on_group, qc * (HEAD_DIM ** -0.5), 0.0).astype(BF16))
    q_all = jnp.concatenate(q_rows, axis=0)

    k_prev = jnp.where(first_x, kmeta[...], kprev[...])
    v_prev = jnp.where(first_x, vmeta[...], vprev[...])
    kk = jnp.concatenate([k_prev, k_cur], axis=0)
    vv = jnp.concatenate([v_prev, v_cur], axis=0)
    kprev[...] = k_cur
    vprev[...] = v_cur
    kmeta[...] = jnp.where(is_meta, k_cur, kmeta[...])
    vmeta[...] = jnp.where(is_meta, v_cur, vmeta[...])
    yield

    sc = _dot_nt(q_all, kk).reshape(ATT_HEADS, BLK, 2 * BLK)
    yield

    rq = lax.broadcasted_iota(jnp.int32, (BLK, BLK), 0)
    ck = lax.broadcasted_iota(jnp.int32, (BLK, BLK), 1)
    from_cur = ck <= rq
    cur_min = jnp.where(is_meta, META_PAD, 0)
    prev_min = jnp.where(is_meta, BLK, jnp.where(first_x, META_PAD, 0))
    valid = ck >= jnp.where(from_cur, cur_min, prev_min)
    sc = jnp.where(from_cur[None], sc[:, :, BLK:], sc[:, :, :BLK])
    sc = jnp.where(valid[None], sc, NEG)

    hid = lax.broadcasted_iota(jnp.int32, (ATT_HEADS, 1, 1), 0)
    sink = jnp.zeros((ATT_HEADS, 1, 1), F32)
    for hg in range(ATT_HEADS):
        sink = jnp.where(hid == hg, sinks_ref[hg], sink)

    m = jnp.maximum(jnp.max(sc, axis=-1, keepdims=True), sink)
    p = jnp.exp(sc - m)
    denom = jnp.sum(p, axis=-1, keepdims=True) + jnp.exp(sink - m)
    p = jnp.concatenate([jnp.where(from_cur[None], 0.0, p), jnp.where(from_cur[None], p, 0.0)],
                        axis=-1)
    p = p.astype(BF16).reshape(ATT_HEADS * BLK, 2 * BLK)
    yield

    o = _dot(p, vv)
    yield

    o = o.reshape(ATT_HEADS, BLK, LANES) * (1.0 / denom)
    chunks = []
    for chunk in range(ATT_HEADS // 2):
        grp = (2 * chunk) // (ATT_HEADS // ATT_KV_HEADS)
        o0, o1 = o[2 * chunk], o[2 * chunk + 1]
        if grp != 0:
            o0 = pltpu.roll(o0, HEAD_DIM, 1)
        if grp != 1:
            o1 = pltpu.roll(o1, HEAD_DIM, 1)
        chunks.append(jnp.where(lane < HEAD_DIM, o0, o1).astype(BF16))
    result.append(jnp.concatenate(chunks, axis=1))


def _gla_tables():
    t = np.arange(BLK)
    tri = (t[:, None] >= t[None, :]).astype(np.float32)
    sel = np.zeros((len(GLA_FINE) * BLK, BLK), np.float32)
    for li, half in enumerate(GLA_FINE):
        boundary = (t & ~(2 * half - 1)) + half - 1
        sel[li * BLK + t, boundary] = 1.0
    x = t[:, None] ^ t[None, :]
    top = np.floor(np.log2(np.maximum(x, 1))).astype(np.int64)
    lvl = np.array([GLA_LEVELS.index(1 << int(v)) for v in top.ravel()]).reshape(BLK, BLK)
    lvl = np.where(t[:, None] == t[None, :], N_LEVELS, lvl)
    lvl = np.where(t[:, None] < t[None, :], N_LEVELS + 1, lvl)
    lvl = np.tile(lvl, (1, GLA_HEADS))
    return (jnp.asarray(tri, BF16), jnp.asarray(sel, BF16), jnp.asarray(lvl, jnp.int32))


def _gla_stages(result, is_meta, first_x, gla_ref, loga_ref, gr_ref, tri_ref, sel_ref, lvl_ref,
                hn_ref, state, state_meta):
    row = lax.broadcasted_iota(jnp.int32, (BLK, GLA_DK), 0)
    lane = lax.broadcasted_iota(jnp.int32, (BLK, GLA_DK), 1)
    is_token = row >= jnp.where(is_meta, META_PAD, 0)

    g3 = jnp.concatenate(_split3(loga_ref[...]), axis=1)
    yield
    b3 = _dot(tri_ref[...], g3)
    yield
    b = b3[:, :GLA_DK] + b3[:, GLA_DK:2 * GLA_DK] + b3[:, 2 * GLA_DK:]
    b_last = b[BLK - 1:BLK, :]

    b_hi = b.astype(BF16)
    b_mid = (b - b_hi.astype(F32)).astype(BF16)
    fine = _dot(sel_ref[...], jnp.concatenate([b_hi, b_mid], axis=1))

    qf = gla_ref[:, :GLA_DK].astype(F32) * (GLA_DK_HEAD ** -0.5)
    kf = jnp.where(is_token, gla_ref[:, GLA_DK:2 * GLA_DK].astype(F32), 0.0)

    head_mask = [jnp.where((lane >= h * GLA_DK_HEAD) & (lane < (h + 1) * GLA_DK_HEAD), 1.0, 0.0)
                 .astype(BF16) for h in range(GLA_HEADS)]

    def stack_heads(x):
        xb = x.astype(BF16)
        return jnp.concatenate([xb * hm for hm in head_mask], axis=0)

    state_in = jnp.where(first_x, state_meta[...], state[...])
    q_in = stack_heads(qf * jnp.exp(b))
    q_diag, k_diag = qf.astype(BF16), stack_heads(kf)
    yield
    o_inter = _dot(q_in, state_in.astype(BF16))
    lvl = lvl_ref[...]
    a = jnp.where(lvl == N_LEVELS, _dot_nt(q_diag, k_diag), 0.0)
    zeros = lambda shape: jnp.zeros(shape, F32)
    for li, half in enumerate(GLA_LEVELS):
        if half in GLA_FINE:
            fi = GLA_FINE.index(half)
            r = fine[fi * BLK:(fi + 1) * BLK, :GLA_DK] + fine[fi * BLK:(fi + 1) * BLK, GLA_DK:]
            upper = (row & half) != 0
            t_l = jnp.where(upper, qf, kf) * jnp.exp(-jnp.abs(b - r))
            q_l = jnp.where(upper, t_l, 0.0)
            k_l = jnp.where(upper, 0.0, t_l)
        else:
            grp = (BLK // (2 * half), 2 * half, GLA_DK)
            b_g, q_g, k_g = b.reshape(grp), qf.reshape(grp), kf.reshape(grp)
            r = b_g[:, half - 1:half, :]
            q_up = q_g[:, half:, :] * jnp.exp(b_g[:, half:, :] - r)
            k_lo = k_g[:, :half, :] * jnp.exp(r - b_g[:, :half, :])
            q_l = jnp.concatenate([zeros(q_up.shape), q_up], axis=1).reshape(BLK, GLA_DK)
            k_l = jnp.concatenate([k_lo, zeros(k_lo.shape)], axis=1).reshape(BLK, GLA_DK)
        q_l, k_l = q_l.astype(BF16), stack_heads(k_l)
        yield
        a = jnp.where(lvl == li, _dot_nt(q_l, k_l), a)
    a = a.astype(BF16)

    k_out_t = (kf * jnp.exp(b_last - b)).T.astype(BF16)
    decay_t = jnp.broadcast_to(jnp.exp(b_last), (BLK, GLA_DK)).T
    yield

    hn = hn_ref[...]
    new_state, heads = [], []
    for h in range(GLA_HEADS):
        v_h = gla_ref[:, 2 * GLA_DK + h * GLA_DV_HEAD:2 * GLA_DK + (h + 1) * GLA_DV_HEAD]
        o_h = o_inter[h * BLK:(h + 1) * BLK, :] + _dot(a[:, h * BLK:(h + 1) * BLK], v_h)
        new_state.append(_dot(k_out_t[h * GLA_DK_HEAD:(h + 1) * GLA_DK_HEAD, :], v_h))
        heads.append(o_h)
    yield
    for h in range(GLA_HEADS):
        o_h = heads[h]
        var = jnp.mean(o_h * o_h, axis=-1, keepdims=True)
        o_h = o_h * lax.rsqrt(var + EPS) * hn
        gate = gr_ref[:, h * GLA_DV_HEAD:(h + 1) * GLA_DV_HEAD].astype(F32)
        heads[h] = (o_h * gate).astype(BF16)
    state_new = state_in * decay_t + jnp.concatenate(new_state, axis=0)
    state[...] = state_new
    state_meta[...] = jnp.where(is_meta, state_new, state_meta[...])
    result.append(jnp.concatenate(heads, axis=1))


def _merge_stages(result, y_att, y_gla, gate, h, wpa_ref, wpg_ref, wo_ref):
    pa = _dot(y_att, wpa_ref[...])
    pg = _dot(y_gla, wpg_ref[...])
    yield
    mixed = (gate[:, :D_MODEL].astype(F32) * pa + gate[:, D_MODEL:].astype(F32) * pg).astype(BF16)
    yield
    result.append(h + _dot(mixed, wo_ref[...]))


def _advance(gens):
    alive = []
    for gen in gens:
        try:
            next(gen)
            alive.append(gen)
        except StopIteration:
            pass
    return alive


def _mixer_kernel(nb, sinks_ref, xnext_ref, meta_ref, g_ref, w_in_t_ref, b_ref, wal_hi_ref, wal_lo_ref,
                  bal_ref, rope_ref, tri_ref, sel_ref, lvl_ref, hn_ref, wpa_ref, wpg_ref,
                  wo_ref, wup_f32_ref, wdn_f32_ref, out_ref, outm_ref, wup_bf16_ref, wdn_bf16_ref,
                  w_cat, cur_h, cur_qkv, cur_gla, cur_gr, cur_gate, cur_loga,
                  nxt_qkv, nxt_gla, nxt_gr, nxt_gate, nxt_loga,
                  kprev, vprev, kmeta, vmeta, state, state_meta):
    s = pl.program_id(0)
    is_meta = s == 0
    first_x = jnp.logical_and(s >= 1, lax.rem(jnp.maximum(s - 1, 0), nb) == 0)
    cur = (cur_qkv, cur_gla, cur_gr, cur_gate, cur_loga)
    nxt = (nxt_qkv, nxt_gla, nxt_gr, nxt_gate, nxt_loga)
    weights = (g_ref, w_cat, b_ref, wal_hi_ref, wal_lo_ref, bal_ref)
    groups = range(N_GROUPS)

    @pl.when(is_meta)
    def _():
        for ref in (kprev, vprev, kmeta, vmeta, state, state_meta):
            ref[...] = jnp.zeros_like(ref)
        _regroup_w_in(w_in_t_ref, w_cat)

        def store_all_groups(i, c0, c1, val):
            for g in groups:
                cur[i][g, :, c0:c1] = val

        for g in groups:
            cur_h[g] = meta_ref[...]
        meta_proj = [_inproj_stages(meta_ref[...], weights, store_all_groups)]
        while meta_proj:
            meta_proj = _advance(meta_proj)

    def store_next(i, c0, c1, val):
        nxt[i][:, :, c0:c1] = val.reshape(N_GROUPS, BLK, c1 - c0)

    x_next = xnext_ref[...]
    inproj = [_inproj_stages(x_next.reshape(N_GROUPS * BLK, D_MODEL), weights, store_next)]

    y_att, y_gla, h_mid = [], [], []
    mixers = []
    for g in groups:
        mixers.append(_attn_stages(y_att, is_meta, first_x, sinks_ref, cur_qkv.at[g], rope_ref,
                                   kprev.at[g], vprev.at[g], kmeta.at[g], vmeta.at[g]))
        mixers.append(_gla_stages(y_gla, is_meta, first_x, cur_gla.at[g], cur_loga.at[g],
                                  cur_gr.at[g], tri_ref, sel_ref, lvl_ref, hn_ref,
                                  state.at[g], state_meta.at[g]))
    starts = sorted(zip(MIXER_STARTS, range(len(mixers))))
    active, rounds = [], 0
    while active or starts:
        while starts and starts[0][0] <= rounds:
            active.append(mixers[starts.pop(0)[1]])
        inproj = _advance(inproj)
        active = _advance(active)
        rounds += 1

    gate = cur_gate[...].reshape(N_GROUPS * BLK, C_GATE)
    h = cur_h[...].reshape(N_GROUPS * BLK, D_MODEL)
    merge = [_merge_stages(h_mid, jnp.concatenate(y_att, axis=0), jnp.concatenate(y_gla, axis=0),
                           gate, h, wpa_ref, wpg_ref, wo_ref)]
    while merge or inproj:
        merge = _advance(merge)
        inproj = _advance(inproj)

    h_mid = h_mid[0].reshape(N_GROUPS, BLK, D_MODEL)
    out_ref[...] = h_mid

    wup_bf16_ref[...] = wup_f32_ref[...].astype(BF16)
    wdn_bf16_ref[...] = wdn_f32_ref[...].astype(BF16)

    cur_h[...] = x_next
    for c_ref, n_ref in zip(cur, nxt):
        c_ref[...] = n_ref[...]

    @pl.when(is_meta)
    def _():
        outm_ref[...] = out_ref[0]


def _mixer(x_groups, meta_block, mix_norm, w_in_t, b_cat, wal_hi, wal_lo, b_alpha, sinks, rope,
           head_norm, wpa, wpg, wo, wup_f32, wdn_f32, layer, nb):
    rows_g = x_groups.shape[1]
    nblocks_g = rows_g // BLK

    def slab(w, rows, out):
        n = w.shape[1] // rows
        assert w.shape[1] % rows == 0 and n <= nblocks_g
        idx = lambda s: jnp.minimum(jnp.maximum(s - 1, 0), n - 1)
        if out:
            return pl.BlockSpec((rows, w.shape[2]), lambda s: (idx(s), 0))
        return pl.BlockSpec((None, rows, w.shape[2]), lambda s: (layer, idx(s), 0))

    wup_rows, wdn_rows = 16, 64
    tri, sel, lvl = _gla_tables()
    const = lambda s: (0, 0)
    xnext = lambda s: (0, jnp.minimum(s, nblocks_g - 1), 0)
    xout = lambda s: (0, jnp.maximum(s - 1, 0), 0)
    pos = lambda s: (jnp.where(s == 0, 0, lax.rem(jnp.maximum(s - 1, 0), nb) + 1), 0)
    grouped = lambda cols, dt: pltpu.VMEM((N_GROUPS, BLK, cols), dt)
    proj_bufs = [grouped(C_QKV, BF16), grouped(C_GLA, BF16), grouped(C_GR, BF16),
                 grouped(C_GATE, BF16), grouped(GLA_DK, F32)]
    return pl.pallas_call(
        functools.partial(_mixer_kernel, nb),
        grid=(nblocks_g + 1,),
        in_specs=[
            pl.BlockSpec(memory_space=pltpu.SMEM),
            pl.BlockSpec((N_GROUPS, BLK, D_MODEL), xnext),
            pl.BlockSpec((BLK, D_MODEL), const),
            pl.BlockSpec((1, D_MODEL), const),
            pl.BlockSpec((None, D_IN, D_MODEL), lambda s: (layer, 0, 0),
                         pipeline_mode=pl.Buffered(1)),
            pl.BlockSpec((1, C_ALL), const),
            pl.BlockSpec((LANES, GLA_DK), const),
            pl.BlockSpec((LANES, GLA_DK), const),
            pl.BlockSpec((1, GLA_DK), const),
            pl.BlockSpec((BLK, 2 * LANES), pos),
            pl.BlockSpec((BLK, BLK), const),
            pl.BlockSpec((len(GLA_FINE) * BLK, BLK), const),
            pl.BlockSpec((BLK, GLA_HEADS * BLK), const),
            pl.BlockSpec((1, GLA_DV_HEAD), const),
            pl.BlockSpec((ATT_Q, D_MODEL), const),
            pl.BlockSpec((GLA_DV, D_MODEL), const),
            pl.BlockSpec((D_MODEL, D_MODEL), const),
            slab(wup_f32, wup_rows, False),
            slab(wdn_f32, wdn_rows, False),
        ],
        out_specs=[pl.BlockSpec((N_GROUPS, BLK, D_MODEL), xout),
                   pl.BlockSpec((BLK, D_MODEL), const),
                   slab(wup_f32, wup_rows, True), slab(wdn_f32, wdn_rows, True)],
        out_shape=[jax.ShapeDtypeStruct((N_GROUPS, rows_g, D_MODEL), F32),
                   jax.ShapeDtypeStruct((BLK, D_MODEL), F32),
                   jax.ShapeDtypeStruct(wup_f32.shape[1:], BF16),
                   jax.ShapeDtypeStruct(wdn_f32.shape[1:], BF16)],
        scratch_shapes=[pltpu.VMEM((D_MODEL, C_ALL), BF16), grouped(D_MODEL, F32)]
        + proj_bufs + proj_bufs
        + [grouped(ATT_KV, BF16)] * 4
        + [pltpu.VMEM((N_GROUPS, GLA_DK, GLA_DV_HEAD), F32)] * 2,
        compiler_params=_params(),
        name="mixer",
    )(sinks, x_groups, meta_block, mix_norm, w_in_t, b_cat, wal_hi, wal_lo, b_alpha, rope,
      tri, sel, lvl, head_norm, wpa, wpg, wo, wup_f32, wdn_f32)


def _ffn_kernel(tm, tiles_per_seq, h_ref, halo_ref, halo_meta_ref, g_ref, wup_ref, cw_ref,
                cb_ref, wdn_ref, gf_ref, out_ref, u_scr):
    def normed(x):
        var = jnp.mean(x * x, axis=-1, keepdims=True)
        return (x * lax.rsqrt(var + EPS) * g_ref[...]).astype(BF16)

    seq_start = lax.rem(pl.program_id(0), tiles_per_seq) == 0
    u_scr[0:HALO, :] = normed(jnp.where(seq_start, halo_meta_ref[...], halo_ref[...]))
    u_scr[HALO:, :] = normed(h_ref[...])

    part_rows = tm // FFN_PARTS

    def part_stages(pi):
        r0 = pi * part_rows
        u = u_scr[r0:r0 + HALO + part_rows, :]
        a = _dot(u, wup_ref[:, :D_FF])
        yield
        v = _dot(u[HALO:], wup_ref[:, D_FF:])
        yield
        cw = cw_ref[...]
        conv = (cw[0:1] * pltpu.roll(a, 2, 0) + cw[1:2] * pltpu.roll(a, 1, 0) + cw[2:3] * a
                + cb_ref[...])[HALO:]
        inner = 0.7978845608028654 * (conv + 0.044715 * (conv * conv * conv))
        act = conv * (0.5 * (1.0 + jnp.tanh(inner)))
        gated = (act * v).astype(BF16)
        yield
        h2 = h_ref[r0:r0 + part_rows, :] + _dot(gated, wdn_ref[...])
        yield
        var = jnp.mean(h2 * h2, axis=-1, keepdims=True)
        out_ref[r0:r0 + part_rows, :] = h2 * lax.rsqrt(var + EPS) * gf_ref[...]

    waiting = [part_stages(pi) for pi in range(FFN_PARTS)]
    active, rounds = [], 0
    while active or waiting:
        if waiting and rounds % FFN_LAG == 0:
            active.append(waiting.pop(0))
        active = _advance(active)
        rounds += 1


def _ffn(h_mid, h_meta, ffn_norm, wup, conv_w, conv_b, wdn, final_norm, seq, tm):
    rows_x = h_mid.shape[0]
    assert seq % tm == 0 and rows_x % seq == 0 and tm % (FFN_PARTS * HALO) == 0
    tiles_per_seq = seq // tm
    const = lambda i: (0, 0)
    row = lambda i: (i, 0)
    return pl.pallas_call(
        functools.partial(_ffn_kernel, tm, tiles_per_seq),
        grid=(rows_x // tm,),
        in_specs=[
            pl.BlockSpec((tm, D_MODEL), row),
            pl.BlockSpec((HALO, D_MODEL), lambda i: (jnp.maximum(i * (tm // HALO) - 1, 0), 0)),
            pl.BlockSpec((HALO, D_MODEL), lambda i: (BLK // HALO - 1, 0)),
            pl.BlockSpec((1, D_MODEL), const),
            pl.BlockSpec((D_MODEL, 2 * D_FF), const, pipeline_mode=pl.Buffered(1)),
            pl.BlockSpec((8, D_FF), const),
            pl.BlockSpec((1, D_FF), const),
            pl.BlockSpec((D_FF, D_MODEL), const, pipeline_mode=pl.Buffered(1)),
            pl.BlockSpec((1, D_MODEL), const),
        ],
        out_specs=pl.BlockSpec((tm, D_MODEL), row),
        out_shape=jax.ShapeDtypeStruct((rows_x, D_MODEL), F32),
        scratch_shapes=[pltpu.VMEM((HALO + tm, D_MODEL), BF16)],
        compiler_params=_params(),
        name="conv_ffn",
    )(h_mid, h_mid, h_meta, ffn_norm, wup, conv_w, conv_b, wdn, final_norm)


def _rope_tables(nb):
    half = HEAD_DIM // 2
    inv_freq = ROPE_THETA ** (-np.arange(half, dtype=np.float64) / half)
    pos = (np.arange((nb + 1) * BLK) - META_PAD).astype(np.float64)
    ang = pos[:, None] * inv_freq[None, :]
    cos2 = np.tile(np.cos(ang), (1, LANES // half))
    sin = np.sin(ang)
    sin2 = np.tile(np.concatenate([-sin, sin], axis=-1), (1, LANES // HEAD_DIM))
    return jnp.asarray(np.concatenate([cos2, sin2], axis=1), F32)


def kernel(x, meta_tokens, mix_norm, w_in, b_in, w_alpha, b_alpha, attn_sinks, gla_head_norm,
           w_proj_attn, w_proj_gla, w_out, ffn_norm, w_up, conv_w, conv_b, w_down, final_norm):
    batch, seq, _ = x.shape
    assert batch % N_GROUPS == 0 and seq % BLK == 0
    rows_x = batch * seq
    nb = seq // BLK
    l = 0

    meta_block = jnp.concatenate(
        [jnp.zeros((META_PAD, D_MODEL), x.dtype), meta_tokens.astype(x.dtype)], axis=0)

    w_in_t = jnp.swapaxes(w_in, 1, 2)
    bi = b_in[l]
    b_cat = jnp.concatenate(
        [bi[:LOW0], bi[LOW0 + GLA_RANK:], bi[LOW0:LOW0 + GLA_RANK],
         jnp.zeros((C_LOW - GLA_RANK,), bi.dtype)])[None, :]
    wal = jnp.concatenate(
        [w_alpha[l], jnp.zeros((C_LOW - GLA_RANK, GLA_DK), w_alpha.dtype)], axis=0)
    wal_hi = wal.astype(BF16)
    wal_lo = (wal - wal_hi.astype(F32)).astype(BF16)

    rope = _rope_tables(nb)
    x_groups = x.reshape(N_GROUPS, rows_x // N_GROUPS, D_MODEL)
    h_mid, h_meta, wup, wdn = _mixer(
        x_groups, meta_block, mix_norm[l][None, :], w_in_t, b_cat, wal_hi, wal_lo,
        b_alpha[l][None, :], attn_sinks[l], rope, gla_head_norm[l][None, :],
        w_proj_attn[l].astype(BF16), w_proj_gla[l].astype(BF16), w_out[l].astype(BF16),
        w_up, w_down, l, nb)

    cw = jnp.concatenate([conv_w[l], jnp.zeros((8 - conv_w.shape[1], D_FF), conv_w.dtype)], axis=0)
    out = _ffn(h_mid.reshape(rows_x, D_MODEL), h_meta, ffn_norm[l][None, :], wup,
               cw, conv_b[l][None, :], wdn, final_norm[None, :], seq,
               tm=1024)
    return out.reshape(batch, seq, D_MODEL)
```

```python
import functools

import numpy as np
import jax
import jax.numpy as jnp
from jax import lax
from jax.experimental import pallas as pl
from jax.experimental.pallas import tpu as pltpu

D_MODEL = 1024
N_META = 16
EPS = 1e-6
ATT_HEADS = 8
ATT_KV_HEADS = 2
HEAD_DIM = 64
BLK = 128
ROPE_THETA = 10000.0
GLA_HEADS = 4
GLA_DK = 256
GLA_DV = 512
GLA_DK_HEAD = GLA_DK // GLA_HEADS
GLA_DV_HEAD = GLA_DV // GLA_HEADS
GLA_RANK = 16
GLA_TAU = 16.0
D_FF = 2816
ATT_Q = ATT_HEADS * HEAD_DIM
ATT_KV = ATT_KV_HEADS * HEAD_DIM

META_PAD = BLK - N_META
LANES = 128
HALO = 16
NEG = -1e30
VMEM_LIMIT = 56 * 1024 * 1024

C_QKV = ATT_Q + 2 * ATT_KV
C_GLA = 2 * GLA_DK + GLA_DV
C_GR = GLA_DV
C_GATE = 2 * D_MODEL
C_LOW = LANES
O_QKV = 0
O_GLA = O_QKV + C_QKV
O_GR = O_GLA + C_GLA
O_GATE = O_GR + C_GR
O_LOW = O_GATE + C_GATE
C_ALL = O_LOW + C_LOW
PROJ_CHUNK = 256

GLA_LEVELS = (64, 32, 16, 8, 4, 2, 1)
N_LEVELS = len(GLA_LEVELS)
GLA_FINE = tuple(h for h in GLA_LEVELS if h < 8)

N_GROUPS = 2
MIXER_STARTS = (4, 0, 12, 8)
FFN_PARTS = 2
FFN_LAG = 1

BF16 = jnp.bfloat16
F32 = jnp.float32


def _sigmoid(x):
    return 1.0 / (1.0 + jnp.exp(-x))


def _dot(a, b):
    return jnp.dot(a, b, preferred_element_type=F32)


def _dot_nt(a, b):
    return lax.dot_general(a, b, (((1,), (1,)), ((), ())), preferred_element_type=F32)


def _split3(x):
    hi = x.astype(BF16)
    r1 = x - hi.astype(F32)
    mid = r1.astype(BF16)
    lo = (r1 - mid.astype(F32)).astype(BF16)
    return hi, mid, lo


def _params():
    return pltpu.CompilerParams(dimension_semantics=("arbitrary",), vmem_limit_bytes=VMEM_LIMIT)


LOW0 = ATT_Q + 2 * ATT_KV + 2 * GLA_DK + 2 * GLA_DV
D_IN = LOW0 + GLA_RANK + C_GATE


def _regroup_w_in(wt_ref, out_ref):
    for t in range(C_ALL // LANES):
        col = t * LANES
        if col < LOW0:
            rows = wt_ref[col:col + LANES, :]
        elif col < O_LOW:
            src = LOW0 + GLA_RANK + (col - O_GATE)
            rows = wt_ref[src:src + LANES, :]
        else:
            rows = jnp.concatenate([wt_ref[LOW0:LOW0 + GLA_RANK, :],
                                    jnp.zeros((C_LOW - GLA_RANK, D_MODEL), F32)], axis=0)
        out_ref[:, col:col + LANES] = rows.T.astype(BF16)


def _inproj_stages(x, weights, store):
    g_ref, w_ref, b_ref, wal_hi_ref, wal_lo_ref, bal_ref = weights
    var = jnp.mean(x * x, axis=-1, keepdims=True)
    u = (x * lax.rsqrt(var + EPS) * g_ref[...]).astype(BF16)
    yield

    def proj(off, width):
        return _dot(u, w_ref[:, off:off + width]) + b_ref[:, off:off + width]

    low = proj(O_LOW, C_LOW)
    yield
    silu = lambda y: y * _sigmoid(y)
    plan = ((3, O_GATE, C_GATE, _sigmoid), (2, O_GR, C_GR, silu), (1, O_GLA, C_GLA, None),
            (0, O_QKV, C_QKV, None))
    low_pending = True
    for i, off, width, act in plan:
        for c in range(0, width, PROJ_CHUNK):
            y = proj(off + c, PROJ_CHUNK)
            store(i, c, c + PROJ_CHUNK, (y if act is None else act(y)).astype(BF16))
            yield
            if low_pending:
                low_pending = False
                low_hi = low.astype(BF16)
                low_lo = (low - low_hi.astype(F32)).astype(BF16)
                z = (_dot(low_hi, wal_hi_ref[...]) + _dot(low_lo, wal_hi_ref[...])
                     + _dot(low_hi, wal_lo_ref[...]) + bal_ref[...])
                log_sig = jnp.minimum(z, 0.0) - jnp.log1p(jnp.exp(-jnp.abs(z)))
                store(4, 0, GLA_DK, log_sig * (1.0 / GLA_TAU))
                yield


def _attn_stages(result, is_meta, first_x, sinks_ref, qkv_ref, rope_ref,
                 kprev, vprev, kmeta, vmeta):
    cos = rope_ref[:, :LANES]
    sin = rope_ref[:, LANES:]
    lane = lax.broadcasted_iota(jnp.int32, (BLK, LANES), 1)
    first_half = (lane & (HEAD_DIM - 1)) < HEAD_DIM // 2

    def rope(xc):
        rot = jnp.where(first_half, pltpu.roll(xc, LANES - HEAD_DIM // 2, 1),
                        pltpu.roll(xc, HEAD_DIM // 2, 1))
        return xc * cos + rot * sin

    k_cur = rope(qkv_ref[:, ATT_Q:ATT_Q + ATT_KV].astype(F32)).astype(BF16)
    v_cur = qkv_ref[:, ATT_Q + ATT_KV:ATT_Q + 2 * ATT_KV]

    q_rows = []
    for hg in range(ATT_HEADS):
        chunk, off, grp = hg // 2, hg % 2, hg // (ATT_HEADS // ATT_KV_HEADS)
        qc = rope(qkv_ref[:, chunk * LANES:(chunk + 1) * LANES].astype(F32))
        if off != grp:
            qc = pltpu.roll(qc, HEAD_DIM, 1)
        on_group = (lane >= grp * HEAD_DIM) & (lane < (grp + 1) * HEAD_DIM)
        q_rows.append(jnp.where(on_group, qc * (HEAD_DIM ** -0.5), 0.0).astype(BF16))
    q_all = jnp.concatenate(q_rows, axis=0)

    k_prev = jnp.where(first_x, kmeta[...], kprev[...])
    v_prev = jnp.where(first_x, vmeta[...], vprev[...])
    kk = jnp.concatenate([k_prev, k_cur], axis=0)
    vv = jnp.concatenate([v_prev, v_cur], axis=0)
    kprev[...] = k_cur
    vprev[...] = v_cur
    kmeta[...] = jnp.where(is_meta, k_cur, kmeta[...])
    vmeta[...] = jnp.where(is_meta, v_cur, vmeta[...])
    yield

    sc = _dot_nt(q_all, kk).reshape(ATT_HEADS, BLK, 2 * BLK)
    yield

    rq = lax.broadcasted_iota(jnp.int32, (BLK, BLK), 0)
    ck = lax.broadcasted_iota(jnp.int32, (BLK, BLK), 1)
    from_cur = ck <= rq
    cur_min = jnp.where(is_meta, META_PAD, 0)
    prev_min = jnp.where(is_meta, BLK, jnp.where(first_x, META_PAD, 0))
    valid = ck >= jnp.where(from_cur, cur_min, prev_min)
    sc = jnp.where(from_cur[None], sc[:, :, BLK:], sc[:, :, :BLK])
    sc = jnp.where(valid[None], sc, NEG)

    hid = lax.broadcasted_iota(jnp.int32, (ATT_HEADS, 1, 1), 0)
    sink = jnp.zeros((ATT_HEADS, 1, 1), F32)
    for hg in range(ATT_HEADS):
        sink = jnp.where(hid == hg, sinks_ref[hg], sink)

    m = jnp.maximum(jnp.max(sc, axis=-1, keepdims=True), sink)
    p = jnp.exp(sc - m)
    denom = jnp.sum(p, axis=-1, keepdims=True) + jnp.exp(sink - m)
    p = jnp.concatenate([jnp.where(from_cur[None], 0.0, p), jnp.where(from_cur[None], p, 0.0)],
                        axis=-1)
    p = p.astype(BF16).reshape(ATT_HEADS * BLK, 2 * BLK)
    yield

    o = _dot(p, vv)
    yield

    o = o.reshape(ATT_HEADS, BLK, LANES) * (1.0 / denom)
    chunks = []
    for chunk in range(ATT_HEADS // 2):
        grp = (2 * chunk) // (ATT_HEADS // ATT_KV_HEADS)
        o0, o1 = o[2 * chunk], o[2 * chunk + 1]
        if grp != 0:
            o0 = pltpu.roll(o0, HEAD_DIM, 1)
        if grp != 1:
            o1 = pltpu.roll(o1, HEAD_DIM, 1)
        chunks.append(jnp.where(lane < HEAD_DIM, o0, o1).astype(BF16))
    result.append(jnp.concatenate(chunks, axis=1))


def _gla_tables():
    t = np.arange(BLK)
    tri = (t[:, None] >= t[None, :]).astype(np.float32)
    sel = np.zeros((len(GLA_FINE) * BLK, BLK), np.float32)
    for li, half in enumerate(GLA_FINE):
        boundary = (t & ~(2 * half - 1)) + half - 1
        sel[li * BLK + t, boundary] = 1.0
    x = t[:, None] ^ t[None, :]
    top = np.floor(np.log2(np.maximum(x, 1))).astype(np.int64)
    lvl = np.array([GLA_LEVELS.index(1 << int(v)) for v in top.ravel()]).reshape(BLK, BLK)
    lvl = np.where(t[:, None] == t[None, :], N_LEVELS, lvl)
    lvl = np.where(t[:, None] < t[None, :], N_LEVELS + 1, lvl)
    lvl = np.tile(lvl, (1, GLA_HEADS))
    return (jnp.asarray(tri, BF16), jnp.asarray(sel, BF16), jnp.asarray(lvl, jnp.int32))


def _gla_stages(result, is_meta, first_x, gla_ref, loga_ref, gr_ref, tri_ref, sel_ref, lvl_ref,
                hn_ref, state, state_meta):
    row = lax.broadcasted_iota(jnp.int32, (BLK, GLA_DK), 0)
    lane = lax.broadcasted_iota(jnp.int32, (BLK, GLA_DK), 1)
    is_token = row >= jnp.where(is_meta, META_PAD, 0)

    g3 = jnp.concatenate(_split3(loga_ref[...]), axis=1)
    yield
    b3 = _dot(tri_ref[...], g3)
    yield
    b = b3[:, :GLA_DK] + b3[:, GLA_DK:2 * GLA_DK] + b3[:, 2 * GLA_DK:]
    b_last = b[BLK - 1:BLK, :]

    b_hi = b.astype(BF16)
    b_mid = (b - b_hi.astype(F32)).astype(BF16)
    fine = _dot(sel_ref[...], jnp.concatenate([b_hi, b_mid], axis=1))

    qf = gla_ref[:, :GLA_DK].astype(F32) * (GLA_DK_HEAD ** -0.5)
    kf = jnp.where(is_token, gla_ref[:, GLA_DK:2 * GLA_DK].astype(F32), 0.0)

    head_mask = [jnp.where((lane >= h * GLA_DK_HEAD) & (lane < (h + 1) * GLA_DK_HEAD), 1.0, 0.0)
                 .astype(BF16) for h in range(GLA_HEADS)]

    def stack_heads(x):
        xb = x.astype(BF16)
        return jnp.concatenate([xb * hm for hm in head_mask], axis=0)

    state_in = jnp.where(first_x, state_meta[...], state[...])
    q_in = stack_heads(qf * jnp.exp(b))
    q_diag, k_diag = qf.astype(BF16), stack_heads(kf)
    yield
    o_inter = _dot(q_in, state_in.astype(BF16))
    lvl = lvl_ref[...]
    a = jnp.where(lvl == N_LEVELS, _dot_nt(q_diag, k_diag), 0.0)
    zeros = lambda shape: jnp.zeros(shape, F32)
    for li, half in enumerate(GLA_LEVELS):
        if half in GLA_FINE:
            fi = GLA_FINE.index(half)
            r = fine[fi * BLK:(fi + 1) * BLK, :GLA_DK] + fine[fi * BLK:(fi + 1) * BLK, GLA_DK:]
            upper = (row & half) != 0
            t_l = jnp.where(upper, qf, kf) * jnp.exp(-jnp.abs(b - r))
            q_l = jnp.where(upper, t_l, 0.0)
            k_l = jnp.where(upper, 0.0, t_l)
        else:
            grp = (BLK // (2 * half), 2 * half, GLA_DK)
            b_g, q_g, k_g = b.reshape(grp), qf.reshape(grp), kf.reshape(grp)
            r = b_g[:, half - 1:half, :]
            q_up = q_g[:, half:, :] * jnp.exp(b_g[:, half:, :] - r)
            k_lo = k_g[:, :half, :] * jnp.exp(r - b_g[:, :half, :])
            q_l = jnp.concatenate([zeros(q_up.shape), q_up], axis=1).reshape(BLK, GLA_DK)
            k_l = jnp.concatenate([k_lo, zeros(k_lo.shape)], axis=1).reshape(BLK, GLA_DK)
        q_l, k_l = q_l.astype(BF16), stack_heads(k_l)
        yield
        a = jnp.where(lvl == li, _dot_nt(q_l, k_l), a)
    a = a.astype(BF16)

    k_out_t = (kf * jnp.exp(b_last - b)).T.astype(BF16)
    decay_t = jnp.broadcast_to(jnp.exp(b_last), (BLK, GLA_DK)).T
    yield

    hn = hn_ref[...]
    new_state, heads = [], []
    for h in range(GLA_HEADS):
        v_h = gla_ref[:, 2 * GLA_DK + h * GLA_DV_HEAD:2 * GLA_DK + (h + 1) * GLA_DV_HEAD]
        o_h = o_inter[h * BLK:(h + 1) * BLK, :] + _dot(a[:, h * BLK:(h + 1) * BLK], v_h)
        new_state.append(_dot(k_out_t[h * GLA_DK_HEAD:(h + 1) * GLA_DK_HEAD, :], v_h))
        heads.append(o_h)
    yield
    for h in range(GLA_HEADS):
        o_h = heads[h]
        var = jnp.mean(o_h * o_h, axis=-1, keepdims=True)
        o_h = o_h * lax.rsqrt(var + EPS) * hn
        gate = gr_ref[:, h * GLA_DV_HEAD:(h + 1) * GLA_DV_HEAD].astype(F32)
        heads[h] = (o_h * gate).astype(BF16)
    state_new = state_in * decay_t + jnp.concatenate(new_state, axis=0)
    state[...] = state_new
    state_meta[...] = jnp.where(is_meta, state_new, state_meta[...])
    result.append(jnp.concatenate(heads, axis=1))


def _merge_stages(result, y_att, y_gla, gate, h, wpa_ref, wpg_ref, wo_ref):
    pa = _dot(y_att, wpa_ref[...])
    pg = _dot(y_gla, wpg_ref[...])
    yield
    mixed = (gate[:, :D_MODEL].astype(F32) * pa + gate[:, D_MODEL:].astype(F32) * pg).astype(BF16)
    yield
    result.append(h + _dot(mixed, wo_ref[...]))


def _advance(gens):
    alive = []
    for gen in gens:
        try:
            next(gen)
            alive.append(gen)
        except StopIteration:
            pass
    return alive


def _mixer_kernel(nb, sinks_ref, xnext_ref, meta_ref, g_ref, w_in_t_ref, b_ref, wal_hi_ref, wal_lo_ref,
                  bal_ref, rope_ref, tri_ref, sel_ref, lvl_ref, hn_ref, wpa_ref, wpg_ref,
                  wo_ref, wup_f32_ref, wdn_f32_ref, out_ref, outm_ref, wup_bf16_ref, wdn_bf16_ref,
                  w_cat, cur_h, cur_qkv, cur_gla, cur_gr, cur_gate, cur_loga,
                  nxt_qkv, nxt_gla, nxt_gr, nxt_gate, nxt_loga,
                  kprev, vprev, kmeta, vmeta, state, state_meta):
    s = pl.program_id(0)
    is_meta = s == 0
    first_x = jnp.logical_and(s >= 1, lax.rem(jnp.maximum(s - 1, 0), nb) == 0)
    cur = (cur_qkv, cur_gla, cur_gr, cur_gate, cur_loga)
    nxt = (nxt_qkv, nxt_gla, nxt_gr, nxt_gate, nxt_loga)
    weights = (g_ref, w_cat, b_ref, wal_hi_ref, wal_lo_ref, bal_ref)
    groups = range(N_GROUPS)

    @pl.when(is_meta)
    def _():
        for ref in (kprev, vprev, kmeta, vmeta, state, state_meta):
            ref[...] = jnp.zeros_like(ref)
        _regroup_w_in(w_in_t_ref, w_cat)

        def store_all_groups(i, c0, c1, val):
            for g in groups:
                cur[i][g, :, c0:c1] = val

        for g in groups:
            cur_h[g] = meta_ref[...]
        meta_proj = [_inproj_stages(meta_ref[...], weights, store_all_groups)]
        while meta_proj:
            meta_proj = _advance(meta_proj)

    def store_next(i, c0, c1, val):
        nxt[i][:, :, c0:c1] = val.reshape(N_GROUPS, BLK, c1 - c0)

    x_next = xnext_ref[...]
    inproj = [_inproj_stages(x_next.reshape(N_GROUPS * BLK, D_MODEL), weights, store_next)]

    y_att, y_gla, h_mid = [], [], []
    mixers = []
    for g in groups:
        mixers.append(_attn_stages(y_att, is_meta, first_x, sinks_ref, cur_qkv.at[g], rope_ref,
                                   kprev.at[g], vprev.at[g], kmeta.at[g], vmeta.at[g]))
        mixers.append(_gla_stages(y_gla, is_meta, first_x, cur_gla.at[g], cur_loga.at[g],
                                  cur_gr.at[g], tri_ref, sel_ref, lvl_ref, hn_ref,
                                  state.at[g], state_meta.at[g]))
    starts = sorted(zip(MIXER_STARTS, range(len(mixers))))
    active, rounds = [], 0
    while active or starts:
        while starts and starts[0][0] <= rounds:
            active.append(mixers[starts.pop(0)[1]])
        inproj = _advance(inproj)
        active = _advance(active)
        rounds += 1

    gate = cur_gate[...].reshape(N_GROUPS * BLK, C_GATE)
    h = cur_h[...].reshape(N_GROUPS * BLK, D_MODEL)
    merge = [_merge_stages(h_mid, jnp.concatenate(y_att, axis=0), jnp.concatenate(y_gla, axis=0),
                           gate, h, wpa_ref, wpg_ref, wo_ref)]
    while merge or inproj:
        merge = _advance(merge)
        inproj = _advance(inproj)

    h_mid = h_mid[0].reshape(N_GROUPS, BLK, D_MODEL)
    out_ref[...] = h_mid

    wup_bf16_ref[...] = wup_f32_ref[...].astype(BF16)
    wdn_bf16_ref[...] = wdn_f32_ref[...].astype(BF16)

    cur_h[...] = x_next
    for c_ref, n_ref in zip(cur, nxt):
        c_ref[...] = n_ref[...]

    @pl.when(is_meta)
    def _():
        outm_ref[...] = out_ref[0]


def _mixer(x_groups, meta_block, mix_norm, w_in_t, b_cat, wal_hi, wal_lo, b_alpha, sinks, rope,
           head_norm, wpa, wpg, wo, wup_f32, wdn_f32, layer, nb):
    rows_g = x_groups.shape[1]
    nblocks_g = rows_g // BLK

    def slab(w, rows, out):
        n = w.shape[1] // rows
        assert w.shape[1] % rows == 0 and n <= nblocks_g
        idx = lambda s: jnp.minimum(jnp.maximum(s - 1, 0), n - 1)
        if out:
            return pl.BlockSpec((rows, w.shape[2]), lambda s: (idx(s), 0))
        return pl.BlockSpec((None, rows, w.shape[2]), lambda s: (layer, idx(s), 0))

    wup_rows, wdn_rows = 16, 64
    tri, sel, lvl = _gla_tables()
    const = lambda s: (0, 0)
    xnext = lambda s: (0, jnp.minimum(s, nblocks_g - 1), 0)
    xout = lambda s: (0, jnp.maximum(s - 1, 0), 0)
    pos = lambda s: (jnp.where(s == 0, 0, lax.rem(jnp.maximum(s - 1, 0), nb) + 1), 0)
    grouped = lambda cols, dt: pltpu.VMEM((N_GROUPS, BLK, cols), dt)
    proj_bufs = [grouped(C_QKV, BF16), grouped(C_GLA, BF16), grouped(C_GR, BF16),
                 grouped(C_GATE, BF16), grouped(GLA_DK, F32)]
    return pl.pallas_call(
        functools.partial(_mixer_kernel, nb),
        grid=(nblocks_g + 1,),
        in_specs=[
            pl.BlockSpec(memory_space=pltpu.SMEM),
            pl.BlockSpec((N_GROUPS, BLK, D_MODEL), xnext),
            pl.BlockSpec((BLK, D_MODEL), const),
            pl.BlockSpec((1, D_MODEL), const),
            pl.BlockSpec((None, D_IN, D_MODEL), lambda s: (layer, 0, 0),
                         pipeline_mode=pl.Buffered(1)),
            pl.BlockSpec((1, C_ALL), const),
            pl.BlockSpec((LANES, GLA_DK), const),
            pl.BlockSpec((LANES, GLA_DK), const),
            pl.BlockSpec((1, GLA_DK), const),
            pl.BlockSpec((BLK, 2 * LANES), pos),
            pl.BlockSpec((BLK, BLK), const),
            pl.BlockSpec((len(GLA_FINE) * BLK, BLK), const),
            pl.BlockSpec((BLK, GLA_HEADS * BLK), const),
            pl.BlockSpec((1, GLA_DV_HEAD), const),
            pl.BlockSpec((ATT_Q, D_MODEL), const),
            pl.BlockSpec((GLA_DV, D_MODEL), const),
            pl.BlockSpec((D_MODEL, D_MODEL), const),
            slab(wup_f32, wup_rows, False),
            slab(wdn_f32, wdn_rows, False),
        ],
        out_specs=[pl.BlockSpec((N_GROUPS, BLK, D_MODEL), xout),
                   pl.BlockSpec((BLK, D_MODEL), const),
                   slab(wup_f32, wup_rows, True), slab(wdn_f32, wdn_rows, True)],
        out_shape=[jax.ShapeDtypeStruct((N_GROUPS, rows_g, D_MODEL), F32),
                   jax.ShapeDtypeStruct((BLK, D_MODEL), F32),
                   jax.ShapeDtypeStruct(wup_f32.shape[1:], BF16),
                   jax.ShapeDtypeStruct(wdn_f32.shape[1:], BF16)],
        scratch_shapes=[pltpu.VMEM((D_MODEL, C_ALL), BF16), grouped(D_MODEL, F32)]
        + proj_bufs + proj_bufs
        + [grouped(ATT_KV, BF16)] * 4
        + [pltpu.VMEM((N_GROUPS, GLA_DK, GLA_DV_HEAD), F32)] * 2,
        compiler_params=_params(),
        name="mixer",
    )(sinks, x_groups, meta_block, mix_norm, w_in_t, b_cat, wal_hi, wal_lo, b_alpha, rope,
      tri, sel, lvl, head_norm, wpa, wpg, wo, wup_f32, wdn_f32)


def _ffn_kernel(tm, tiles_per_seq, h_ref, halo_ref, halo_meta_ref, g_ref, wup_ref, cw_ref,
                cb_ref, wdn_ref, gf_ref, out_ref, u_scr):
    def normed(x):
        var = jnp.mean(x * x, axis=-1, keepdims=True)
        return (x * lax.rsqrt(var + EPS) * g_ref[...]).astype(BF16)

    seq_start = lax.rem(pl.program_id(0), tiles_per_seq) == 0
    u_scr[0:HALO, :] = normed(jnp.where(seq_start, halo_meta_ref[...], halo_ref[...]))
    u_scr[HALO:, :] = normed(h_ref[...])

    part_rows = tm // FFN_PARTS

    def part_stages(pi):
        r0 = pi * part_rows
        u = u_scr[r0:r0 + HALO + part_rows, :]
        a = _dot(u, wup_ref[:, :D_FF])
        yield
        v = _dot(u[HALO:], wup_ref[:, D_FF:])
        yield
        cw = cw_ref[...]
        conv = (cw[0:1] * pltpu.roll(a, 2, 0) + cw[1:2] * pltpu.roll(a, 1, 0) + cw[2:3] * a
                + cb_ref[...])[HALO:]
        inner = 0.7978845608028654 * (conv + 0.044715 * (conv * conv * conv))
        act = conv * (0.5 * (1.0 + jnp.tanh(inner)))
        gated = (act * v).astype(BF16)
        yield
        h2 = h_ref[r0:r0 + part_rows, :] + _dot(gated, wdn_ref[...])
        yield
        var = jnp.mean(h2 * h2, axis=-1, keepdims=True)
        out_ref[r0:r0 + part_rows, :] = h2 * lax.rsqrt(var + EPS) * gf_ref[...]

    waiting = [part_stages(pi) for pi in range(FFN_PARTS)]
    active, rounds = [], 0
    while active or waiting:
        if waiting and rounds % FFN_LAG == 0:
            active.append(waiting.pop(0))
        active = _advance(active)
        rounds += 1


def _ffn(h_mid, h_meta, ffn_norm, wup, conv_w, conv_b, wdn, final_norm, seq, tm):
    rows_x = h_mid.shape[0]
    assert seq % tm == 0 and rows_x % seq == 0 and tm % (FFN_PARTS * HALO) == 0
    tiles_per_seq = seq // tm
    const = lambda i: (0, 0)
    row = lambda i: (i, 0)
    return pl.pallas_call(
        functools.partial(_ffn_kernel, tm, tiles_per_seq),
        grid=(rows_x // tm,),
        in_specs=[
            pl.BlockSpec((tm, D_MODEL), row),
            pl.BlockSpec((HALO, D_MODEL), lambda i: (jnp.maximum(i * (tm // HALO) - 1, 0), 0)),
            pl.BlockSpec((HALO, D_MODEL), lambda i: (BLK // HALO - 1, 0)),
            pl.BlockSpec((1, D_MODEL), const),
            pl.BlockSpec((D_MODEL, 2 * D_FF), const, pipeline_mode=pl.Buffered(1)),
            pl.BlockSpec((8, D_FF), const),
            pl.BlockSpec((1, D_FF), const),
            pl.BlockSpec((D_FF, D_MODEL), const, pipeline_mode=pl.Buffered(1)),
            pl.BlockSpec((1, D_MODEL), const),
        ],
        out_specs=pl.BlockSpec((tm, D_MODEL), row),
        out_shape=jax.ShapeDtypeStruct((rows_x, D_MODEL), F32),
        scratch_shapes=[pltpu.VMEM((HALO + tm, D_MODEL), BF16)],
        compiler_params=_params(),
        name="conv_ffn",
    )(h_mid, h_mid, h_meta, ffn_norm, wup, conv_w, conv_b, wdn, final_norm)


def _rope_tables(nb):
    half = HEAD_DIM // 2
    inv_freq = ROPE_THETA ** (-np.arange(half, dtype=np.float64) / half)
    pos = (np.arange((nb + 1) * BLK) - META_PAD).astype(np.float64)
    ang = pos[:, None] * inv_freq[None, :]
    cos2 = np.tile(np.cos(ang), (1, LANES // half))
    sin = np.sin(ang)
    sin2 = np.tile(np.concatenate([-sin, sin], axis=-1), (1, LANES // HEAD_DIM))
    return jnp.asarray(np.concatenate([cos2, sin2], axis=1), F32)


def kernel(x, meta_tokens, mix_norm, w_in, b_in, w_alpha, b_alpha, attn_sinks, gla_head_norm,
           w_proj_attn, w_proj_gla, w_out, ffn_norm, w_up, conv_w, conv_b, w_down, final_norm):
    batch, seq, _ = x.shape
    assert batch % N_GROUPS == 0 and seq % BLK == 0
    rows_x = batch * seq
    nb = seq // BLK
    l = 0

    meta_block = jnp.concatenate(
        [jnp.zeros((META_PAD, D_MODEL), x.dtype), meta_tokens.astype(x.dtype)], axis=0)

    w_in_t = jnp.swapaxes(w_in, 1, 2)
    bi = b_in[l]
    b_cat = jnp.concatenate(
        [bi[:LOW0], bi[LOW0 + GLA_RANK:], bi[LOW0:LOW0 + GLA_RANK],
         jnp.zeros((C_LOW - GLA_RANK,), bi.dtype)])[None, :]
    wal = jnp.concatenate(
        [w_alpha[l], jnp.zeros((C_LOW - GLA_RANK, GLA_DK), w_alpha.dtype)], axis=0)
    wal_hi = wal.astype(BF16)
    wal_lo = (wal - wal_hi.astype(F32)).astype(BF16)

    rope = _rope_tables(nb)
    x_groups = x.reshape(N_GROUPS, rows_x // N_GROUPS, D_MODEL)
    h_mid, h_meta, wup, wdn = _mixer(
        x_groups, meta_block, mix_norm[l][None, :], w_in_t, b_cat, wal_hi, wal_lo,
        b_alpha[l][None, :], attn_sinks[l], rope, gla_head_norm[l][None, :],
        w_proj_attn[l].astype(BF16), w_proj_gla[l].astype(BF16), w_out[l].astype(BF16),
        w_up, w_down, l, nb)

    cw = jnp.concatenate([conv_w[l], jnp.zeros((8 - conv_w.shape[1], D_FF), conv_w.dtype)], axis=0)
    out = _ffn(h_mid.reshape(rows_x, D_MODEL), h_meta, ffn_norm[l][None, :], wup,
               cw, conv_b[l][None, :], wdn, final_norm[None, :], seq,
               tm=1024)
    return out.reshape(batch, seq, D_MODEL)
```

```python
import functools

import numpy as np
import jax
import jax.numpy as jnp
from jax import lax
from jax.experimental import pallas as pl
from jax.experimental.pallas import tpu as pltpu

D_MODEL = 1024
N_META = 16
EPS = 1e-6
ATT_HEADS = 8
ATT_KV_HEADS = 2
HEAD_DIM = 64
BLK = 128
ROPE_THETA = 10000.0
GLA_HEADS = 4
GLA_DK = 256
GLA_DV = 512
GLA_DK_HEAD = GLA_DK // GLA_HEADS
GLA_DV_HEAD = GLA_DV // GLA_HEADS
GLA_RANK = 16
GLA_TAU = 16.0
D_FF = 2816
ATT_Q = ATT_HEADS * HEAD_DIM
ATT_KV = ATT_KV_HEADS * HEAD_DIM

META_PAD = BLK - N_META
LANES = 128
HALO = 16
NEG = -1e30
VMEM_LIMIT = 56 * 1024 * 1024

C_QKV = ATT_Q + 2 * ATT_KV
C_GLA = 2 * GLA_DK + GLA_DV
C_GR = GLA_DV
C_GATE = 2 * D_MODEL
C_LOW = LANES
O_QKV = 0
O_GLA = O_QKV + C_QKV
O_GR = O_GLA + C_GLA
O_GATE = O_GR + C_GR
O_LOW = O_GATE + C_GATE
C_ALL = O_LOW + C_LOW
PROJ_CHUNK = 256

GLA_LEVELS = (64, 32, 16, 8, 4, 2, 1)
N_LEVELS = len(GLA_LEVELS)
GLA_FINE = tuple(h for h in GLA_LEVELS if h < 8)

N_GROUPS = 2
MIXER_STARTS = (4, 0, 12, 8)
FFN_PARTS = 2
FFN_LAG = 1

BF16 = jnp.bfloat16
F32 = jnp.float32


def _sigmoid(x):
    return 1.0 / (1.0 + jnp.exp(-x))


def _dot(a, b):
    return jnp.dot(a, b, preferred_element_type=F32)


def _dot_nt(a, b):
    return lax.dot_general(a, b, (((1,), (1,)), ((), ())), preferred_element_type=F32)


def _split3(x):
    hi = x.astype(BF16)
    r1 = x - hi.astype(F32)
    mid = r1.astype(BF16)
    lo = (r1 - mid.astype(F32)).astype(BF16)
    return hi, mid, lo


def _params():
    return pltpu.CompilerParams(dimension_semantics=("arbitrary",), vmem_limit_bytes=VMEM_LIMIT)


LOW0 = ATT_Q + 2 * ATT_KV + 2 * GLA_DK + 2 * GLA_DV
D_IN = LOW0 + GLA_RANK + C_GATE


def _regroup_w_in(wt_ref, out_ref):
    for t in range(C_ALL // LANES):
        col = t * LANES
        if col < LOW0:
            rows = wt_ref[col:col + LANES, :]
        elif col < O_LOW:
            src = LOW0 + GLA_RANK + (col - O_GATE)
            rows = wt_ref[src:src + LANES, :]
        else:
            rows = jnp.concatenate([wt_ref[LOW0:LOW0 + GLA_RANK, :],
                                    jnp.zeros((C_LOW - GLA_RANK, D_MODEL), F32)], axis=0)
        out_ref[:, col:col + LANES] = rows.T.astype(BF16)


def _inproj_stages(x, weights, store):
    g_ref, w_ref, b_ref, wal_hi_ref, wal_lo_ref, bal_ref = weights
    var = jnp.mean(x * x, axis=-1, keepdims=True)
    u = (x * lax.rsqrt(var + EPS) * g_ref[...]).astype(BF16)
    yield

    def proj(off, width):
        return _dot(u, w_ref[:, off:off + width]) + b_ref[:, off:off + width]

    low = proj(O_LOW, C_LOW)
    yield
    silu = lambda y: y * _sigmoid(y)
    plan = ((3, O_GATE, C_GATE, _sigmoid), (2, O_GR, C_GR, silu), (1, O_GLA, C_GLA, None),
            (0, O_QKV, C_QKV, None))
    low_pending = True
    chunks = [(i, off, act, c) for i, off, width, act in plan for c in range(0, width, PROJ_CHUNK)]
    heavy = [t for t in chunks if t[2] is not None]
    light = [t for t in chunks if t[2] is None]
    order = []
    while heavy or light:
        order += heavy[:1] + light[:1]
        heavy, light = heavy[1:], light[1:]
    for i, off, act, c in order:
        if True:
            y = proj(off + c, PROJ_CHUNK)
            store(i, c, c + PROJ_CHUNK, (y if act is None else act(y)).astype(BF16))
            yield
            if low_pending:
                low_pending = False
                low_hi = low.astype(BF16)
                low_lo = (low - low_hi.astype(F32)).astype(BF16)
                z = (_dot(low_hi, wal_hi_ref[...]) + _dot(low_lo, wal_hi_ref[...])
                     + _dot(low_hi, wal_lo_ref[...]) + bal_ref[...])
                log_sig = jnp.minimum(z, 0.0) - jnp.log1p(jnp.exp(-jnp.abs(z)))
                store(4, 0, GLA_DK, log_sig * (1.0 / GLA_TAU))
                yield


def _attn_stages(result, is_meta, first_x, sinks_ref, qkv_ref, rope_ref,
                 kprev, vprev, kmeta, vmeta):
    cos = rope_ref[:, :LANES]
    sin = rope_ref[:, LANES:]
    lane = lax.broadcasted_iota(jnp.int32, (BLK, LANES), 1)
    first_half = (lane & (HEAD_DIM - 1)) < HEAD_DIM // 2

    def rope(xc):
        rot = jnp.where(first_half, pltpu.roll(xc, LANES - HEAD_DIM // 2, 1),
                        pltpu.roll(xc, HEAD_DIM // 2, 1))
        return xc * cos + rot * sin

    k_cur = rope(qkv_ref[:, ATT_Q:ATT_Q + ATT_KV].astype(F32)).astype(BF16)
    v_cur = qkv_ref[:, ATT_Q + ATT_KV:ATT_Q + 2 * ATT_KV]

    q_rows = []
    for hg in range(ATT_HEADS):
        chunk, off, grp = hg // 2, hg % 2, hg // (ATT_HEADS // ATT_KV_HEADS)
        qc = rope(qkv_ref[:, chunk * LANES:(chunk + 1) * LANES].astype(F32))
        if off != grp:
            qc = pltpu.roll(qc, HEAD_DIM, 1)
        on_group = (lane >= grp * HEAD_DIM) & (lane < (grp + 1) * HEAD_DIM)
        q_rows.append(jnp.where(on_group, qc * (HEAD_DIM ** -0.5), 0.0).astype(BF16))
    q_all = jnp.concatenate(q_rows, axis=0)

    k_prev = jnp.where(first_x, kmeta[...], kprev[...])
    v_prev = jnp.where(first_x, vmeta[...], vprev[...])
    kk = jnp.concatenate([k_prev, k_cur], axis=0)
    vv = jnp.concatenate([v_prev, v_cur], axis=0)
    kprev[...] = k_cur
    vprev[...] = v_cur
    kmeta[...] = jnp.where(is_meta, k_cur, kmeta[...])
    vmeta[...] = jnp.where(is_meta, v_cur, vmeta[...])
    yield

    sc = _dot_nt(q_all, kk).reshape(ATT_HEADS, BLK, 2 * BLK)
    yield

    rq = lax.broadcasted_iota(jnp.int32, (BLK, BLK), 0)
    ck = lax.broadcasted_iota(jnp.int32, (BLK, BLK), 1)
    from_cur = ck <= rq
    cur_min = jnp.where(is_meta, META_PAD, 0)
    prev_min = jnp.where(is_meta, BLK, jnp.where(first_x, META_PAD, 0))
    valid = ck >= jnp.where(from_cur, cur_min, prev_min)
    sc = jnp.where(from_cur[None], sc[:, :, BLK:], sc[:, :, :BLK])
    sc = jnp.where(valid[None], sc, NEG)

    hid = lax.broadcasted_iota(jnp.int32, (ATT_HEADS, 1, 1), 0)
    sink = jnp.zeros((ATT_HEADS, 1, 1), F32)
    for hg in range(ATT_HEADS):
        sink = jnp.where(hid == hg, sinks_ref[hg], sink)

    m = jnp.maximum(jnp.max(sc, axis=-1, keepdims=True), sink)
    p = jnp.exp(sc - m)
    denom = jnp.sum(p, axis=-1, keepdims=True) + jnp.exp(sink - m)
    p = jnp.concatenate([jnp.where(from_cur[None], 0.0, p), jnp.where(from_cur[None], p, 0.0)],
                        axis=-1)
    p = p.astype(BF16).reshape(ATT_HEADS * BLK, 2 * BLK)
    yield

    o = _dot(p, vv)
    yield

    o = o.reshape(ATT_HEADS, BLK, LANES) * (1.0 / denom)
    chunks = []
    for chunk in range(ATT_HEADS // 2):
        grp = (2 * chunk) // (ATT_HEADS // ATT_KV_HEADS)
        o0, o1 = o[2 * chunk], o[2 * chunk + 1]
        if grp != 0:
            o0 = pltpu.roll(o0, HEAD_DIM, 1)
        if grp != 1:
            o1 = pltpu.roll(o1, HEAD_DIM, 1)
        chunks.append(jnp.where(lane < HEAD_DIM, o0, o1).astype(BF16))
    result.append(jnp.concatenate(chunks, axis=1))


def _gla_tables():
    t = np.arange(BLK)
    tri = (t[:, None] >= t[None, :]).astype(np.float32)
    sel = np.zeros((len(GLA_FINE) * BLK, BLK), np.float32)
    for li, half in enumerate(GLA_FINE):
        boundary = (t & ~(2 * half - 1)) + half - 1
        sel[li * BLK + t, boundary] = 1.0
    x = t[:, None] ^ t[None, :]
    top = np.floor(np.log2(np.maximum(x, 1))).astype(np.int64)
    lvl = np.array([GLA_LEVELS.index(1 << int(v)) for v in top.ravel()]).reshape(BLK, BLK)
    lvl = np.where(t[:, None] == t[None, :], N_LEVELS, lvl)
    lvl = np.where(t[:, None] < t[None, :], N_LEVELS + 1, lvl)
    lvl = np.tile(lvl, (1, GLA_HEADS))
    return (jnp.asarray(tri, BF16), jnp.asarray(sel, BF16), jnp.asarray(lvl, jnp.int32))


def _gla_stages(result, is_meta, first_x, gla_ref, loga_ref, gr_ref, tri_ref, sel_ref, lvl_ref,
                hn_ref, state, state_meta):
    row = lax.broadcasted_iota(jnp.int32, (BLK, GLA_DK), 0)
    lane = lax.broadcasted_iota(jnp.int32, (BLK, GLA_DK), 1)
    is_token = row >= jnp.where(is_meta, META_PAD, 0)

    g3 = jnp.concatenate(_split3(loga_ref[...]), axis=1)
    yield
    b3 = _dot(tri_ref[...], g3)
    yield
    b = b3[:, :GLA_DK] + b3[:, GLA_DK:2 * GLA_DK] + b3[:, 2 * GLA_DK:]
    b_last = b[BLK - 1:BLK, :]

    b_hi = b.astype(BF16)
    b_mid = (b - b_hi.astype(F32)).astype(BF16)
    fine = _dot(sel_ref[...], jnp.concatenate([b_hi, b_mid], axis=1))

    qf = gla_ref[:, :GLA_DK].astype(F32) * (GLA_DK_HEAD ** -0.5)
    kf = jnp.where(is_token, gla_ref[:, GLA_DK:2 * GLA_DK].astype(F32), 0.0)

    head_mask = [jnp.where((lane >= h * GLA_DK_HEAD) & (lane < (h + 1) * GLA_DK_HEAD), 1.0, 0.0)
                 .astype(BF16) for h in range(GLA_HEADS)]

    def stack_heads(x):
        xb = x.astype(BF16)
        return jnp.concatenate([xb * hm for hm in head_mask], axis=0)

    state_in = jnp.where(first_x, state_meta[...], state[...])
    q_in = stack_heads(qf * jnp.exp(b))
    q_diag, k_diag = qf.astype(BF16), stack_heads(kf)
    yield
    o_inter = _dot(q_in, state_in.astype(BF16))
    lvl = lvl_ref[...]
    a = jnp.where(lvl == N_LEVELS, _dot_nt(q_diag, k_diag), 0.0)
    zeros = lambda shape: jnp.zeros(shape, F32)
    for li, half in enumerate(GLA_LEVELS):
        if half in GLA_FINE:
            fi = GLA_FINE.index(half)
            r = fine[fi * BLK:(fi + 1) * BLK, :GLA_DK] + fine[fi * BLK:(fi + 1) * BLK, GLA_DK:]
            upper = (row & half) != 0
            t_l = jnp.where(upper, qf, kf) * jnp.exp(-jnp.abs(b - r))
            q_l = jnp.where(upper, t_l, 0.0)
            k_l = jnp.where(upper, 0.0, t_l)
        else:
            grp = (BLK // (2 * half), 2 * half, GLA_DK)
            b_g, q_g, k_g = b.reshape(grp), qf.reshape(grp), kf.reshape(grp)
            r = b_g[:, half - 1:half, :]
            q_up = q_g[:, half:, :] * jnp.exp(b_g[:, half:, :] - r)
            k_lo = k_g[:, :half, :] * jnp.exp(r - b_g[:, :half, :])
            q_l = jnp.concatenate([zeros(q_up.shape), q_up], axis=1).reshape(BLK, GLA_DK)
            k_l = jnp.concatenate([k_lo, zeros(k_lo.shape)], axis=1).reshape(BLK, GLA_DK)
        q_l, k_l = q_l.astype(BF16), stack_heads(k_l)
        yield
        a = jnp.where(lvl == li, _dot_nt(q_l, k_l), a)
    a = a.astype(BF16)

    k_out_t = (kf * jnp.exp(b_last - b)).T.astype(BF16)
    decay_t = jnp.broadcast_to(jnp.exp(b_last), (BLK, GLA_DK)).T
    yield

    hn = hn_ref[...]
    new_state, heads = [], []
    for h in range(GLA_HEADS):
        v_h = gla_ref[:, 2 * GLA_DK + h * GLA_DV_HEAD:2 * GLA_DK + (h + 1) * GLA_DV_HEAD]
        o_h = o_inter[h * BLK:(h + 1) * BLK, :] + _dot(a[:, h * BLK:(h + 1) * BLK], v_h)
        new_state.append(_dot(k_out_t[h * GLA_DK_HEAD:(h + 1) * GLA_DK_HEAD, :], v_h))
        heads.append(o_h)
    yield
    for h in range(GLA_HEADS):
        o_h = heads[h]
        var = jnp.mean(o_h * o_h, axis=-1, keepdims=True)
        o_h = o_h * lax.rsqrt(var + EPS) * hn
        gate = gr_ref[:, h * GLA_DV_HEAD:(h + 1) * GLA_DV_HEAD].astype(F32)
        heads[h] = (o_h * gate).astype(BF16)
    state_new = state_in * decay_t + jnp.concatenate(new_state, axis=0)
    state[...] = state_new
    state_meta[...] = jnp.where(is_meta, state_new, state_meta[...])
    result.append(jnp.concatenate(heads, axis=1))


def _merge_stages(result, y_att, y_gla, gate, h, wpa_ref, wpg_ref, wo_ref):
    pa = _dot(y_att, wpa_ref[...])
    pg = _dot(y_gla, wpg_ref[...])
    yield
    mixed = (gate[:, :D_MODEL].astype(F32) * pa + gate[:, D_MODEL:].astype(F32) * pg).astype(BF16)
    yield
    result.append(h + _dot(mixed, wo_ref[...]))


def _advance(gens):
    alive = []
    for gen in gens:
        try:
            next(gen)
            alive.append(gen)
        except StopIteration:
            pass
    return alive


def _mixer_kernel(nb, sinks_ref, xnext_ref, meta_ref, g_ref, w_in_t_ref, b_ref, wal_hi_ref, wal_lo_ref,
                  bal_ref, rope_ref, tri_ref, sel_ref, lvl_ref, hn_ref, wpa_ref, wpg_ref,
                  wo_ref, wup_f32_ref, wdn_f32_ref, out_ref, outm_ref, wup_bf16_ref, wdn_bf16_ref,
                  w_cat, cur_h, cur_qkv, cur_gla, cur_gr, cur_gate, cur_loga,
                  nxt_qkv, nxt_gla, nxt_gr, nxt_gate, nxt_loga,
                  kprev, vprev, kmeta, vmeta, state, state_meta):
    s = pl.program_id(0)
    is_meta = s == 0
    first_x = jnp.logical_and(s >= 1, lax.rem(jnp.maximum(s - 1, 0), nb) == 0)
    cur = (cur_qkv, cur_gla, cur_gr, cur_gate, cur_loga)
    nxt = (nxt_qkv, nxt_gla, nxt_gr, nxt_gate, nxt_loga)
    weights = (g_ref, w_cat, b_ref, wal_hi_ref, wal_lo_ref, bal_ref)
    groups = range(N_GROUPS)

    @pl.when(is_meta)
    def _():
        for ref in (kprev, vprev, kmeta, vmeta, state, state_meta):
            ref[...] = jnp.zeros_like(ref)
        _regroup_w_in(w_in_t_ref, w_cat)

        def store_all_groups(i, c0, c1, val):
            for g in groups:
                cur[i][g, :, c0:c1] = val

        for g in groups:
            cur_h[g] = meta_ref[...]
        meta_proj = [_inproj_stages(meta_ref[...], weights, store_all_groups)]
        while meta_proj:
            meta_proj = _advance(meta_proj)

    def store_next(i, c0, c1, val):
        nxt[i][:, :, c0:c1] = val.reshape(N_GROUPS, BLK, c1 - c0)

    x_next = xnext_ref[...]
    inproj = [_inproj_stages(x_next.reshape(N_GROUPS * BLK, D_MODEL), weights, store_next)]

    y_att, y_gla, h_mid = [], [], []
    mixers = []
    for g in groups:
        mixers.append(_attn_stages(y_att, is_meta, first_x, sinks_ref, cur_qkv.at[g], rope_ref,
                                   kprev.at[g], vprev.at[g], kmeta.at[g], vmeta.at[g]))
        mixers.append(_gla_stages(y_gla, is_meta, first_x, cur_gla.at[g], cur_loga.at[g],
                                  cur_gr.at[g], tri_ref, sel_ref, lvl_ref, hn_ref,
                                  state.at[g], state_meta.at[g]))
    starts = sorted(zip(MIXER_STARTS, range(len(mixers))))
    active, rounds = [], 0
    while active or starts:
        while starts and starts[0][0] <= rounds:
            active.append(mixers[starts.pop(0)[1]])
        inproj = _advance(inproj)
        active = _advance(active)
        rounds += 1

    gate = cur_gate[...].reshape(N_GROUPS * BLK, C_GATE)
    h = cur_h[...].reshape(N_GROUPS * BLK, D_MODEL)
    merge = [_merge_stages(h_mid, jnp.concatenate(y_att, axis=0), jnp.concatenate(y_gla, axis=0),
                           gate, h, wpa_ref, wpg_ref, wo_ref)]
    while merge or inproj:
        merge = _advance(merge)
        inproj = _advance(inproj)

    h_mid = h_mid[0].reshape(N_GROUPS, BLK, D_MODEL)
    out_ref[...] = h_mid

    wup_bf16_ref[...] = wup_f32_ref[...].astype(BF16)
    wdn_bf16_ref[...] = wdn_f32_ref[...].astype(BF16)

    cur_h[...] = x_next
    for c_ref, n_ref in zip(cur, nxt):
        c_ref[...] = n_ref[...]

    @pl.when(is_meta)
    def _():
        outm_ref[...] = out_ref[0]


def _mixer(x_groups, meta_block, mix_norm, w_in_t, b_cat, wal_hi, wal_lo, b_alpha, sinks, rope,
           head_norm, wpa, wpg, wo, wup_f32, wdn_f32, layer, nb):
    rows_g = x_groups.shape[1]
    nblocks_g = rows_g // BLK

    def slab(w, rows, out):
        n = w.shape[1] // rows
        assert w.shape[1] % rows == 0 and n <= nblocks_g
        idx = lambda s: jnp.minimum(jnp.maximum(s - 1, 0), n - 1)
        if out:
            return pl.BlockSpec((rows, w.shape[2]), lambda s: (idx(s), 0))
        return pl.BlockSpec((None, rows, w.shape[2]), lambda s: (layer, idx(s), 0))

    wup_rows, wdn_rows = 16, 64
    tri, sel, lvl = _gla_tables()
    const = lambda s: (0, 0)
    xnext = lambda s: (0, jnp.minimum(s, nblocks_g - 1), 0)
    xout = lambda s: (0, jnp.maximum(s - 1, 0), 0)
    pos = lambda s: (jnp.where(s == 0, 0, lax.rem(jnp.maximum(s - 1, 0), nb) + 1), 0)
    grouped = lambda cols, dt: pltpu.VMEM((N_GROUPS, BLK, cols), dt)
    proj_bufs = [grouped(C_QKV, BF16), grouped(C_GLA, BF16), grouped(C_GR, BF16),
                 grouped(C_GATE, BF16), grouped(GLA_DK, F32)]
    return pl.pallas_call(
        functools.partial(_mixer_kernel, nb),
        grid=(nblocks_g + 1,),
        in_specs=[
            pl.BlockSpec(memory_space=pltpu.SMEM),
            pl.BlockSpec((N_GROUPS, BLK, D_MODEL), xnext),
            pl.BlockSpec((BLK, D_MODEL), const),
            pl.BlockSpec((1, D_MODEL), const),
            pl.BlockSpec((None, D_IN, D_MODEL), lambda s: (layer, 0, 0),
                         pipeline_mode=pl.Buffered(1)),
            pl.BlockSpec((1, C_ALL), const),
            pl.BlockSpec((LANES, GLA_DK), const),
            pl.BlockSpec((LANES, GLA_DK), const),
            pl.BlockSpec((1, GLA_DK), const),
            pl.BlockSpec((BLK, 2 * LANES), pos),
            pl.BlockSpec((BLK, BLK), const),
            pl.BlockSpec((len(GLA_FINE) * BLK, BLK), const),
            pl.BlockSpec((BLK, GLA_HEADS * BLK), const),
            pl.BlockSpec((1, GLA_DV_HEAD), const),
            pl.BlockSpec((ATT_Q, D_MODEL), const),
            pl.BlockSpec((GLA_DV, D_MODEL), const),
            pl.BlockSpec((D_MODEL, D_MODEL), const),
            slab(wup_f32, wup_rows, False),
            slab(wdn_f32, wdn_rows, False),
        ],
        out_specs=[pl.BlockSpec((N_GROUPS, BLK, D_MODEL), xout),
                   pl.BlockSpec((BLK, D_MODEL), const),
                   slab(wup_f32, wup_rows, True), slab(wdn_f32, wdn_rows, True)],
        out_shape=[jax.ShapeDtypeStruct((N_GROUPS, rows_g, D_MODEL), F32),
                   jax.ShapeDtypeStruct((BLK, D_MODEL), F32),
                   jax.ShapeDtypeStruct(wup_f32.shape[1:], BF16),
                   jax.ShapeDtypeStruct(wdn_f32.shape[1:], BF16)],
        scratch_shapes=[pltpu.VMEM((D_MODEL, C_ALL), BF16), grouped(D_MODEL, F32)]
        + proj_bufs + proj_bufs
        + [grouped(ATT_KV, BF16)] * 4
        + [pltpu.VMEM((N_GROUPS, GLA_DK, GLA_DV_HEAD), F32)] * 2,
        compiler_params=_params(),
        name="mixer",
    )(sinks, x_groups, meta_block, mix_norm, w_in_t, b_cat, wal_hi, wal_lo, b_alpha, rope,
      tri, sel, lvl, head_norm, wpa, wpg, wo, wup_f32, wdn_f32)


def _ffn_kernel(tm, tiles_per_seq, h_ref, halo_ref, halo_meta_ref, g_ref, wup_ref, cw_ref,
                cb_ref, wdn_ref, gf_ref, out_ref, u_scr):
    def normed(x):
        var = jnp.mean(x * x, axis=-1, keepdims=True)
        return (x * lax.rsqrt(var + EPS) * g_ref[...]).astype(BF16)

    seq_start = lax.rem(pl.program_id(0), tiles_per_seq) == 0
    u_scr[0:HALO, :] = normed(jnp.where(seq_start, halo_meta_ref[...], halo_ref[...]))
    u_scr[HALO:, :] = normed(h_ref[...])

    part_rows = tm // FFN_PARTS

    def part_stages(pi):
        r0 = pi * part_rows
        u = u_scr[r0:r0 + HALO + part_rows, :]
        a = _dot(u, wup_ref[:, :D_FF])
        yield
        v = _dot(u[HALO:], wup_ref[:, D_FF:])
        yield
        cw = cw_ref[...]
        conv = (cw[0:1] * pltpu.roll(a, 2, 0) + cw[1:2] * pltpu.roll(a, 1, 0) + cw[2:3] * a
                + cb_ref[...])[HALO:]
        inner = 0.7978845608028654 * (conv + 0.044715 * (conv * conv * conv))
        act = conv * (0.5 * (1.0 + jnp.tanh(inner)))
        gated = (act * v).astype(BF16)
        yield
        h2 = h_ref[r0:r0 + part_rows, :] + _dot(gated, wdn_ref[...])
        yield
        var = jnp.mean(h2 * h2, axis=-1, keepdims=True)
        out_ref[r0:r0 + part_rows, :] = h2 * lax.rsqrt(var + EPS) * gf_ref[...]

    waiting = [part_stages(pi) for pi in range(FFN_PARTS)]
    active, rounds = [], 0
    while active or waiting:
        if waiting and rounds % FFN_LAG == 0:
            active.append(waiting.pop(0))
        active = _advance(active)
        rounds += 1


def _ffn(h_mid, h_meta, ffn_norm, wup, conv_w, conv_b, wdn, final_norm, seq, tm):
    rows_x = h_mid.shape[0]
    assert seq % tm == 0 and rows_x % seq == 0 and tm % (FFN_PARTS * HALO) == 0
    tiles_per_seq = seq // tm
    const = lambda i: (0, 0)
    row = lambda i: (i, 0)
    return pl.pallas_call(
        functools.partial(_ffn_kernel, tm, tiles_per_seq),
        grid=(rows_x // tm,),
        in_specs=[
            pl.BlockSpec((tm, D_MODEL), row),
            pl.BlockSpec((HALO, D_MODEL), lambda i: (jnp.maximum(i * (tm // HALO) - 1, 0), 0)),
            pl.BlockSpec((HALO, D_MODEL), lambda i: (BLK // HALO - 1, 0)),
            pl.BlockSpec((1, D_MODEL), const),
            pl.BlockSpec((D_MODEL, 2 * D_FF), const, pipeline_mode=pl.Buffered(1)),
            pl.BlockSpec((8, D_FF), const),
            pl.BlockSpec((1, D_FF), const),
            pl.BlockSpec((D_FF, D_MODEL), const, pipeline_mode=pl.Buffered(1)),
            pl.BlockSpec((1, D_MODEL), const),
        ],
        out_specs=pl.BlockSpec((tm, D_MODEL), row),
        out_shape=jax.ShapeDtypeStruct((rows_x, D_MODEL), F32),
        scratch_shapes=[pltpu.VMEM((HALO + tm, D_MODEL), BF16)],
        compiler_params=_params(),
        name="conv_ffn",
    )(h_mid, h_mid, h_meta, ffn_norm, wup, conv_w, conv_b, wdn, final_norm)


def _rope_tables(nb):
    half = HEAD_DIM // 2
    inv_freq = ROPE_THETA ** (-np.arange(half, dtype=np.float64) / half)
    pos = (np.arange((nb + 1) * BLK) - META_PAD).astype(np.float64)
    ang = pos[:, None] * inv_freq[None, :]
    cos2 = np.tile(np.cos(ang), (1, LANES // half))
    sin = np.sin(ang)
    sin2 = np.tile(np.concatenate([-sin, sin], axis=-1), (1, LANES // HEAD_DIM))
    return jnp.asarray(np.concatenate([cos2, sin2], axis=1), F32)


def kernel(x, meta_tokens, mix_norm, w_in, b_in, w_alpha, b_alpha, attn_sinks, gla_head_norm,
           w_proj_attn, w_proj_gla, w_out, ffn_norm, w_up, conv_w, conv_b, w_down, final_norm):
    batch, seq, _ = x.shape
    assert batch % N_GROUPS == 0 and seq % BLK == 0
    rows_x = batch * seq
    nb = seq // BLK
    l = 0

    meta_block = jnp.concatenate(
        [jnp.zeros((META_PAD, D_MODEL), x.dtype), meta_tokens.astype(x.dtype)], axis=0)

    w_in_t = jnp.swapaxes(w_in, 1, 2)
    bi = b_in[l]
    b_cat = jnp.concatenate(
        [bi[:LOW0], bi[LOW0 + GLA_RANK:], bi[LOW0:LOW0 + GLA_RANK],
         jnp.zeros((C_LOW - GLA_RANK,), bi.dtype)])[None, :]
    wal = jnp.concatenate(
        [w_alpha[l], jnp.zeros((C_LOW - GLA_RANK, GLA_DK), w_alpha.dtype)], axis=0)
    wal_hi = wal.astype(BF16)
    wal_lo = (wal - wal_hi.astype(F32)).astype(BF16)

    rope = _rope_tables(nb)
    x_groups = x.reshape(N_GROUPS, rows_x // N_GROUPS, D_MODEL)
    h_mid, h_meta, wup, wdn = _mixer(
        x_groups, meta_block, mix_norm[l][None, :], w_in_t, b_cat, wal_hi, wal_lo,
        b_alpha[l][None, :], attn_sinks[l], rope, gla_head_norm[l][None, :],
        w_proj_attn[l].astype(BF16), w_proj_gla[l].astype(BF16), w_out[l].astype(BF16),
        w_up, w_down, l, nb)

    cw = jnp.concatenate([conv_w[l], jnp.zeros((8 - conv_w.shape[1], D_FF), conv_w.dtype)], axis=0)
    out = _ffn(h_mid.reshape(rows_x, D_MODEL), h_meta, ffn_norm[l][None, :], wup,
               cw, conv_b[l][None, :], wdn, final_norm[None, :], seq,
               tm=1024)
    return out.reshape(batch, seq, D_MODEL)
```

```python
import functools

import numpy as np
import jax
import jax.numpy as jnp
from jax import lax
from jax.experimental import pallas as pl
from jax.experimental.pallas import tpu as pltpu

D_MODEL = 1024
N_META = 16
EPS = 1e-6
ATT_HEADS = 8
ATT_KV_HEADS = 2
HEAD_DIM = 64
BLK = 128
ROPE_THETA = 10000.0
GLA_HEADS = 4
GLA_DK = 256
GLA_DV = 512
GLA_DK_HEAD = GLA_DK // GLA_HEADS
GLA_DV_HEAD = GLA_DV // GLA_HEADS
GLA_RANK = 16
GLA_TAU = 16.0
D_FF = 2816
ATT_Q = ATT_HEADS * HEAD_DIM
ATT_KV = ATT_KV_HEADS * HEAD_DIM

META_PAD = BLK - N_META
LANES = 128
HALO = 16
NEG = -1e30
VMEM_LIMIT = 56 * 1024 * 1024

C_QKV = ATT_Q + 2 * ATT_KV
C_GLA = 2 * GLA_DK + GLA_DV
C_GR = GLA_DV
C_GATE = 2 * D_MODEL
C_LOW = LANES
O_QKV = 0
O_GLA = O_QKV + C_QKV
O_GR = O_GLA + C_GLA
O_GATE = O_GR + C_GR
O_LOW = O_GATE + C_GATE
C_ALL = O_LOW + C_LOW
PROJ_CHUNK = 256

GLA_LEVELS = (64, 32, 16, 8, 4, 2, 1)
N_LEVELS = len(GLA_LEVELS)
GLA_FINE = tuple(h for h in GLA_LEVELS if h < 8)

N_GROUPS = 2
MIXER_STARTS = (6, 2, 14, 10)
FFN_PARTS = 2
FFN_LAG = 1

BF16 = jnp.bfloat16
F32 = jnp.float32


def _sigmoid(x):
    return 1.0 / (1.0 + jnp.exp(-x))


def _dot(a, b):
    return jnp.dot(a, b, preferred_element_type=F32)


def _dot_nt(a, b):
    return lax.dot_general(a, b, (((1,), (1,)), ((), ())), preferred_element_type=F32)


def _split3(x):
    hi = x.astype(BF16)
    r1 = x - hi.astype(F32)
    mid = r1.astype(BF16)
    lo = (r1 - mid.astype(F32)).astype(BF16)
    return hi, mid, lo


def _params():
    return pltpu.CompilerParams(dimension_semantics=("arbitrary",), vmem_limit_bytes=VMEM_LIMIT)


LOW0 = ATT_Q + 2 * ATT_KV + 2 * GLA_DK + 2 * GLA_DV
D_IN = LOW0 + GLA_RANK + C_GATE


def _regroup_w_in(wt_ref, out_ref):
    for t in range(C_ALL // LANES):
        col = t * LANES
        if col < LOW0:
            rows = wt_ref[col:col + LANES, :]
        elif col < O_LOW:
            src = LOW0 + GLA_RANK + (col - O_GATE)
            rows = wt_ref[src:src + LANES, :]
        else:
            rows = jnp.concatenate([wt_ref[LOW0:LOW0 + GLA_RANK, :],
                                    jnp.zeros((C_LOW - GLA_RANK, D_MODEL), F32)], axis=0)
        out_ref[:, col:col + LANES] = rows.T.astype(BF16)


def _inproj_stages(x, weights, store):
    g_ref, w_ref, b_ref, wal_hi_ref, wal_lo_ref, bal_ref = weights
    var = jnp.mean(x * x, axis=-1, keepdims=True)
    u = (x * lax.rsqrt(var + EPS) * g_ref[...]).astype(BF16)
    yield

    def proj(off, width):
        return _dot(u, w_ref[:, off:off + width]) + b_ref[:, off:off + width]

    low = proj(O_LOW, C_LOW)
    yield
    silu = lambda y: y * _sigmoid(y)
    plan = ((3, O_GATE, C_GATE, _sigmoid), (2, O_GR, C_GR, silu), (1, O_GLA, C_GLA, None),
            (0, O_QKV, C_QKV, None))
    low_pending = True
    for i, off, width, act in plan:
        for c in range(0, width, PROJ_CHUNK):
            y = proj(off + c, PROJ_CHUNK)
            store(i, c, c + PROJ_CHUNK, (y if act is None else act(y)).astype(BF16))
            yield
            if low_pending:
                low_pending = False
                low_hi = low.astype(BF16)
                low_lo = (low - low_hi.astype(F32)).astype(BF16)
                z = (_dot(low_hi, wal_hi_ref[...]) + _dot(low_lo, wal_hi_ref[...])
                     + _dot(low_hi, wal_lo_ref[...]) + bal_ref[...])
                log_sig = jnp.minimum(z, 0.0) - jnp.log1p(jnp.exp(-jnp.abs(z)))
                store(4, 0, GLA_DK, log_sig * (1.0 / GLA_TAU))
                yield


def _attn_stages(result, is_meta, first_x, sinks_ref, qkv_ref, rope_ref,
                 kprev, vprev, kmeta, vmeta):
    cos = rope_ref[:, :LANES]
    sin = rope_ref[:, LANES:]
    lane = lax.broadcasted_iota(jnp.int32, (BLK, LANES), 1)
    first_half = (lane & (HEAD_DIM - 1)) < HEAD_DIM // 2

    def rope(xc):
        rot = jnp.where(first_half, pltpu.roll(xc, LANES - HEAD_DIM // 2, 1),
                        pltpu.roll(xc, HEAD_DIM // 2, 1))
        return xc * cos + rot * sin

    k_cur = rope(qkv_ref[:, ATT_Q:ATT_Q + ATT_KV].astype(F32)).astype(BF16)
    v_cur = qkv_ref[:, ATT_Q + ATT_KV:ATT_Q + 2 * ATT_KV]

    q_rows = []
    for hg in range(ATT_HEADS):
        chunk, off, grp = hg // 2, hg % 2, hg // (ATT_HEADS // ATT_KV_HEADS)
        qc = rope(qkv_ref[:, chunk * LANES:(chunk + 1) * LANES].astype(F32))
        if off != grp:
            qc = pltpu.roll(qc, HEAD_DIM, 1)
        on_group = (lane >= grp * HEAD_DIM) & (lane < (grp + 1) * HEAD_DIM)
        q_rows.append(jnp.where(on_group, qc * (HEAD_DIM ** -0.5), 0.0).astype(BF16))
    q_all = jnp.concatenate(q_rows, axis=0)

    k_prev = jnp.where(first_x, kmeta[...], kprev[...])
    v_prev = jnp.where(first_x, vmeta[...], vprev[...])
    kk = jnp.concatenate([k_prev, k_cur], axis=0)
    vv = jnp.concatenate([v_prev, v_cur], axis=0)
    kprev[...] = k_cur
    vprev[...] = v_cur
    kmeta[...] = jnp.where(is_meta, k_cur, kmeta[...])
    vmeta[...] = jnp.where(is_meta, v_cur, vmeta[...])
    yield

    sc = _dot_nt(q_all, kk).reshape(ATT_HEADS, BLK, 2 * BLK)
    yield

    rq = lax.broadcasted_iota(jnp.int32, (BLK, BLK), 0)
    ck = lax.broadcasted_iota(jnp.int32, (BLK, BLK), 1)
    from_cur = ck <= rq
    cur_min = jnp.where(is_meta, META_PAD, 0)
    prev_min = jnp.where(is_meta, BLK, jnp.where(first_x, META_PAD, 0))
    valid = ck >= jnp.where(from_cur, cur_min, prev_min)
    sc = jnp.where(from_cur[None], sc[:, :, BLK:], sc[:, :, :BLK])
    sc = jnp.where(valid[None], sc, NEG)

    hid = lax.broadcasted_iota(jnp.int32, (ATT_HEADS, 1, 1), 0)
    sink = jnp.zeros((ATT_HEADS, 1, 1), F32)
    for hg in range(ATT_HEADS):
        sink = jnp.where(hid == hg, sinks_ref[hg], sink)

    m = jnp.maximum(jnp.max(sc, axis=-1, keepdims=True), sink)
    p = jnp.exp(sc - m)
    denom = jnp.sum(p, axis=-1, keepdims=True) + jnp.exp(sink - m)
    p = jnp.concatenate([jnp.where(from_cur[None], 0.0, p), jnp.where(from_cur[None], p, 0.0)],
                        axis=-1)
    p = p.astype(BF16).reshape(ATT_HEADS * BLK, 2 * BLK)
    yield

    o = _dot(p, vv)
    yield

    o = o.reshape(ATT_HEADS, BLK, LANES) * (1.0 / denom)
    chunks = []
    for chunk in range(ATT_HEADS // 2):
        grp = (2 * chunk) // (ATT_HEADS // ATT_KV_HEADS)
        o0, o1 = o[2 * chunk], o[2 * chunk + 1]
        if grp != 0:
            o0 = pltpu.roll(o0, HEAD_DIM, 1)
        if grp != 1:
            o1 = pltpu.roll(o1, HEAD_DIM, 1)
        chunks.append(jnp.where(lane < HEAD_DIM, o0, o1).astype(BF16))
    result.append(jnp.concatenate(chunks, axis=1))


def _gla_tables():
    t = np.arange(BLK)
    tri = (t[:, None] >= t[None, :]).astype(np.float32)
    sel = np.zeros((len(GLA_FINE) * BLK, BLK), np.float32)
    for li, half in enumerate(GLA_FINE):
        boundary = (t & ~(2 * half - 1)) + half - 1
        sel[li * BLK + t, boundary] = 1.0
    x = t[:, None] ^ t[None, :]
    top = np.floor(np.log2(np.maximum(x, 1))).astype(np.int64)
    lvl = np.array([GLA_LEVELS.index(1 << int(v)) for v in top.ravel()]).reshape(BLK, BLK)
    lvl = np.where(t[:, None] == t[None, :], N_LEVELS, lvl)
    lvl = np.where(t[:, None] < t[None, :], N_LEVELS + 1, lvl)
    lvl = np.tile(lvl, (1, GLA_HEADS))
    return (jnp.asarray(tri, BF16), jnp.asarray(sel, BF16), jnp.asarray(lvl, jnp.int32))


def _gla_stages(result, is_meta, first_x, gla_ref, loga_ref, gr_ref, tri_ref, sel_ref, lvl_ref,
                hn_ref, state, state_meta):
    row = lax.broadcasted_iota(jnp.int32, (BLK, GLA_DK), 0)
    lane = lax.broadcasted_iota(jnp.int32, (BLK, GLA_DK), 1)
    is_token = row >= jnp.where(is_meta, META_PAD, 0)

    g3 = jnp.concatenate(_split3(loga_ref[...]), axis=1)
    yield
    b3 = _dot(tri_ref[...], g3)
    yield
    b = b3[:, :GLA_DK] + b3[:, GLA_DK:2 * GLA_DK] + b3[:, 2 * GLA_DK:]
    b_last = b[BLK - 1:BLK, :]

    b_hi = b.astype(BF16)
    b_mid = (b - b_hi.astype(F32)).astype(BF16)
    fine = _dot(sel_ref[...], jnp.concatenate([b_hi, b_mid], axis=1))

    qf = gla_ref[:, :GLA_DK].astype(F32) * (GLA_DK_HEAD ** -0.5)
    kf = jnp.where(is_token, gla_ref[:, GLA_DK:2 * GLA_DK].astype(F32), 0.0)

    head_mask = [jnp.where((lane >= h * GLA_DK_HEAD) & (lane < (h + 1) * GLA_DK_HEAD), 1.0, 0.0)
                 .astype(BF16) for h in range(GLA_HEADS)]

    def stack_heads(x):
        xb = x.astype(BF16)
        return jnp.concatenate([xb * hm for hm in head_mask], axis=0)

    state_in = jnp.where(first_x, state_meta[...], state[...])
    q_in = stack_heads(qf * jnp.exp(b))
    q_diag, k_diag = qf.astype(BF16), stack_heads(kf)
    yield
    o_inter = _dot(q_in, state_in.astype(BF16))
    lvl = lvl_ref[...]
    a = jnp.where(lvl == N_LEVELS, _dot_nt(q_diag, k_diag), 0.0)
    zeros = lambda shape: jnp.zeros(shape, F32)
    for li, half in enumerate(GLA_LEVELS):
        if half in GLA_FINE:
            fi = GLA_FINE.index(half)
            r = fine[fi * BLK:(fi + 1) * BLK, :GLA_DK] + fine[fi * BLK:(fi + 1) * BLK, GLA_DK:]
            upper = (row & half) != 0
            t_l = jnp.where(upper, qf, kf) * jnp.exp(-jnp.abs(b - r))
            q_l = jnp.where(upper, t_l, 0.0)
            k_l = jnp.where(upper, 0.0, t_l)
        else:
            grp = (BLK // (2 * half), 2 * half, GLA_DK)
            b_g, q_g, k_g = b.reshape(grp), qf.reshape(grp), kf.reshape(grp)
            r = b_g[:, half - 1:half, :]
            q_up = q_g[:, half:, :] * jnp.exp(b_g[:, half:, :] - r)
            k_lo = k_g[:, :half, :] * jnp.exp(r - b_g[:, :half, :])
            q_l = jnp.concatenate([zeros(q_up.shape), q_up], axis=1).reshape(BLK, GLA_DK)
            k_l = jnp.concatenate([k_lo, zeros(k_lo.shape)], axis=1).reshape(BLK, GLA_DK)
        q_l, k_l = q_l.astype(BF16), stack_heads(k_l)
        yield
        a = jnp.where(lvl == li, _dot_nt(q_l, k_l), a)
    a = a.astype(BF16)

    k_out_t = (kf * jnp.exp(b_last - b)).T.astype(BF16)
    decay_t = jnp.broadcast_to(jnp.exp(b_last), (BLK, GLA_DK)).T
    yield

    hn = hn_ref[...]
    new_state, heads = [], []
    for h in range(GLA_HEADS):
        v_h = gla_ref[:, 2 * GLA_DK + h * GLA_DV_HEAD:2 * GLA_DK + (h + 1) * GLA_DV_HEAD]
        o_h = o_inter[h * BLK:(h + 1) * BLK, :] + _dot(a[:, h * BLK:(h + 1) * BLK], v_h)
        new_state.append(_dot(k_out_t[h * GLA_DK_HEAD:(h + 1) * GLA_DK_HEAD, :], v_h))
        heads.append(o_h)
    yield
    for h in range(GLA_HEADS):
        o_h = heads[h]
        var = jnp.mean(o_h * o_h, axis=-1, keepdims=True)
        o_h = o_h * lax.rsqrt(var + EPS) * hn
        gate = gr_ref[:, h * GLA_DV_HEAD:(h + 1) * GLA_DV_HEAD].astype(F32)
        heads[h] = (o_h * gate).astype(BF16)
    state_new = state_in * decay_t + jnp.concatenate(new_state, axis=0)
    state[...] = state_new
    state_meta[...] = jnp.where(is_meta, state_new, state_meta[...])
    result.append(jnp.concatenate(heads, axis=1))


def _merge_stages(result, y_att, y_gla, gate, h, wpa_ref, wpg_ref, wo_ref):
    pa = _dot(y_att, wpa_ref[...])
    pg = _dot(y_gla, wpg_ref[...])
    yield
    mixed = (gate[:, :D_MODEL].astype(F32) * pa + gate[:, D_MODEL:].astype(F32) * pg).astype(BF16)
    yield
    result.append(h + _dot(mixed, wo_ref[...]))


def _advance(gens):
    alive = []
    for gen in gens:
        try:
            next(gen)
            alive.append(gen)
        except StopIteration:
            pass
    return alive


def _mixer_kernel(nb, sinks_ref, xnext_ref, meta_ref, g_ref, w_in_t_ref, b_ref, wal_hi_ref, wal_lo_ref,
                  bal_ref, rope_ref, tri_ref, sel_ref, lvl_ref, hn_ref, wpa_ref, wpg_ref,
                  wo_ref, wup_f32_ref, wdn_f32_ref, out_ref, outm_ref, wup_bf16_ref, wdn_bf16_ref,
                  w_cat, cur_h, cur_qkv, cur_gla, cur_gr, cur_gate, cur_loga,
                  nxt_qkv, nxt_gla, nxt_gr, nxt_gate, nxt_loga,
                  kprev, vprev, kmeta, vmeta, state, state_meta):
    s = pl.program_id(0)
    is_meta = s == 0
    first_x = jnp.logical_and(s >= 1, lax.rem(jnp.maximum(s - 1, 0), nb) == 0)
    cur = (cur_qkv, cur_gla, cur_gr, cur_gate, cur_loga)
    nxt = (nxt_qkv, nxt_gla, nxt_gr, nxt_gate, nxt_loga)
    weights = (g_ref, w_cat, b_ref, wal_hi_ref, wal_lo_ref, bal_ref)
    groups = range(N_GROUPS)

    @pl.when(is_meta)
    def _():
        for ref in (kprev, vprev, kmeta, vmeta, state, state_meta):
            ref[...] = jnp.zeros_like(ref)
        _regroup_w_in(w_in_t_ref, w_cat)

        def store_all_groups(i, c0, c1, val):
            for g in groups:
                cur[i][g, :, c0:c1] = val

        for g in groups:
            cur_h[g] = meta_ref[...]
        meta_proj = [_inproj_stages(meta_ref[...], weights, store_all_groups)]
        while meta_proj:
            meta_proj = _advance(meta_proj)

    def store_next(i, c0, c1, val):
        nxt[i][:, :, c0:c1] = val.reshape(N_GROUPS, BLK, c1 - c0)

    x_next = xnext_ref[...]
    inproj = [_inproj_stages(x_next.reshape(N_GROUPS * BLK, D_MODEL), weights, store_next)]

    y_att, y_gla, h_mid = [], [], []
    mixers = []
    for g in groups:
        mixers.append(_attn_stages(y_att, is_meta, first_x, sinks_ref, cur_qkv.at[g], rope_ref,
                                   kprev.at[g], vprev.at[g], kmeta.at[g], vmeta.at[g]))
        mixers.append(_gla_stages(y_gla, is_meta, first_x, cur_gla.at[g], cur_loga.at[g],
                                  cur_gr.at[g], tri_ref, sel_ref, lvl_ref, hn_ref,
                                  state.at[g], state_meta.at[g]))
    starts = sorted(zip(MIXER_STARTS, range(len(mixers))))
    active, rounds = [], 0
    while active or starts:
        while starts and starts[0][0] <= rounds:
            active.append(mixers[starts.pop(0)[1]])
        inproj = _advance(inproj)
        active = _advance(active)
        rounds += 1

    gate = cur_gate[...].reshape(N_GROUPS * BLK, C_GATE)
    h = cur_h[...].reshape(N_GROUPS * BLK, D_MODEL)
    merge = [_merge_stages(h_mid, jnp.concatenate(y_att, axis=0), jnp.concatenate(y_gla, axis=0),
                           gate, h, wpa_ref, wpg_ref, wo_ref)]
    while merge or inproj:
        merge = _advance(merge)
        inproj = _advance(inproj)

    h_mid = h_mid[0].reshape(N_GROUPS, BLK, D_MODEL)
    out_ref[...] = h_mid

    wup_bf16_ref[...] = wup_f32_ref[...].astype(BF16)
    wdn_bf16_ref[...] = wdn_f32_ref[...].astype(BF16)

    cur_h[...] = x_next
    for c_ref, n_ref in zip(cur, nxt):
        c_ref[...] = n_ref[...]

    @pl.when(is_meta)
    def _():
        outm_ref[...] = out_ref[0]


def _mixer(x_groups, meta_block, mix_norm, w_in_t, b_cat, wal_hi, wal_lo, b_alpha, sinks, rope,
           head_norm, wpa, wpg, wo, wup_f32, wdn_f32, layer, nb):
    rows_g = x_groups.shape[1]
    nblocks_g = rows_g // BLK

    def slab(w, rows, out):
        n = w.shape[1] // rows
        assert w.shape[1] % rows == 0 and n <= nblocks_g
        idx = lambda s: jnp.minimum(jnp.maximum(s - 1, 0), n - 1)
        if out:
            return pl.BlockSpec((rows, w.shape[2]), lambda s: (idx(s), 0))
        return pl.BlockSpec((None, rows, w.shape[2]), lambda s: (layer, idx(s), 0))

    wup_rows, wdn_rows = 16, 64
    tri, sel, lvl = _gla_tables()
    const = lambda s: (0, 0)
    xnext = lambda s: (0, jnp.minimum(s, nblocks_g - 1), 0)
    xout = lambda s: (0, jnp.maximum(s - 1, 0), 0)
    pos = lambda s: (jnp.where(s == 0, 0, lax.rem(jnp.maximum(s - 1, 0), nb) + 1), 0)
    grouped = lambda cols, dt: pltpu.VMEM((N_GROUPS, BLK, cols), dt)
    proj_bufs = [grouped(C_QKV, BF16), grouped(C_GLA, BF16), grouped(C_GR, BF16),
                 grouped(C_GATE, BF16), grouped(GLA_DK, F32)]
    return pl.pallas_call(
        functools.partial(_mixer_kernel, nb),
        grid=(nblocks_g + 1,),
        in_specs=[
            pl.BlockSpec(memory_space=pltpu.SMEM),
            pl.BlockSpec((N_GROUPS, BLK, D_MODEL), xnext),
            pl.BlockSpec((BLK, D_MODEL), const),
            pl.BlockSpec((1, D_MODEL), const),
            pl.BlockSpec((None, D_IN, D_MODEL), lambda s: (layer, 0, 0),
                         pipeline_mode=pl.Buffered(1)),
            pl.BlockSpec((1, C_ALL), const),
            pl.BlockSpec((LANES, GLA_DK), const),
            pl.BlockSpec((LANES, GLA_DK), const),
            pl.BlockSpec((1, GLA_DK), const),
            pl.BlockSpec((BLK, 2 * LANES), pos),
            pl.BlockSpec((BLK, BLK), const),
            pl.BlockSpec((len(GLA_FINE) * BLK, BLK), const),
            pl.BlockSpec((BLK, GLA_HEADS * BLK), const),
            pl.BlockSpec((1, GLA_DV_HEAD), const),
            pl.BlockSpec((ATT_Q, D_MODEL), const),
            pl.BlockSpec((GLA_DV, D_MODEL), const),
            pl.BlockSpec((D_MODEL, D_MODEL), const),
            slab(wup_f32, wup_rows, False),
            slab(wdn_f32, wdn_rows, False),
        ],
        out_specs=[pl.BlockSpec((N_GROUPS, BLK, D_MODEL), xout),
                   pl.BlockSpec((BLK, D_MODEL), const),
                   slab(wup_f32, wup_rows, True), slab(wdn_f32, wdn_rows, True)],
        out_shape=[jax.ShapeDtypeStruct((N_GROUPS, rows_g, D_MODEL), F32),
                   jax.ShapeDtypeStruct((BLK, D_MODEL), F32),
                   jax.ShapeDtypeStruct(wup_f32.shape[1:], BF16),
                   jax.ShapeDtypeStruct(wdn_f32.shape[1:], BF16)],
        scratch_shapes=[pltpu.VMEM((D_MODEL, C_ALL), BF16), grouped(D_MODEL, F32)]
        + proj_bufs + proj_bufs
        + [grouped(ATT_KV, BF16)] * 4
        + [pltpu.VMEM((N_GROUPS, GLA_DK, GLA_DV_HEAD), F32)] * 2,
        compiler_params=_params(),
        name="mixer",
    )(sinks, x_groups, meta_block, mix_norm, w_in_t, b_cat, wal_hi, wal_lo, b_alpha, rope,
      tri, sel, lvl, head_norm, wpa, wpg, wo, wup_f32, wdn_f32)


def _ffn_kernel(tm, tiles_per_seq, h_ref, halo_ref, halo_meta_ref, g_ref, wup_ref, cw_ref,
                cb_ref, wdn_ref, gf_ref, out_ref, u_scr):
    def normed(x):
        var = jnp.mean(x * x, axis=-1, keepdims=True)
        return (x * lax.rsqrt(var + EPS) * g_ref[...]).astype(BF16)

    seq_start = lax.rem(pl.program_id(0), tiles_per_seq) == 0
    u_scr[0:HALO, :] = normed(jnp.where(seq_start, halo_meta_ref[...], halo_ref[...]))
    u_scr[HALO:, :] = normed(h_ref[...])

    part_rows = tm // FFN_PARTS

    def part_stages(pi):
        r0 = pi * part_rows
        u = u_scr[r0:r0 + HALO + part_rows, :]
        a = _dot(u, wup_ref[:, :D_FF])
        yield
        v = _dot(u[HALO:], wup_ref[:, D_FF:])
        yield
        cw = cw_ref[...]
        conv = (cw[0:1] * pltpu.roll(a, 2, 0) + cw[1:2] * pltpu.roll(a, 1, 0) + cw[2:3] * a
                + cb_ref[...])[HALO:]
        inner = 0.7978845608028654 * (conv + 0.044715 * (conv * conv * conv))
        act = conv * (0.5 * (1.0 + jnp.tanh(inner)))
        gated = (act * v).astype(BF16)
        yield
        h2 = h_ref[r0:r0 + part_rows, :] + _dot(gated, wdn_ref[...])
        yield
        var = jnp.mean(h2 * h2, axis=-1, keepdims=True)
        out_ref[r0:r0 + part_rows, :] = h2 * lax.rsqrt(var + EPS) * gf_ref[...]

    waiting = [part_stages(pi) for pi in range(FFN_PARTS)]
    active, rounds = [], 0
    while active or waiting:
        if waiting and rounds % FFN_LAG == 0:
            active.append(waiting.pop(0))
        active = _advance(active)
        rounds += 1


def _ffn(h_mid, h_meta, ffn_norm, wup, conv_w, conv_b, wdn, final_norm, seq, tm):
    rows_x = h_mid.shape[0]
    assert seq % tm == 0 and rows_x % seq == 0 and tm % (FFN_PARTS * HALO) == 0
    tiles_per_seq = seq // tm
    const = lambda i: (0, 0)
    row = lambda i: (i, 0)
    return pl.pallas_call(
        functools.partial(_ffn_kernel, tm, tiles_per_seq),
        grid=(rows_x // tm,),
        in_specs=[
            pl.BlockSpec((tm, D_MODEL), row),
            pl.BlockSpec((HALO, D_MODEL), lambda i: (jnp.maximum(i * (tm // HALO) - 1, 0), 0)),
            pl.BlockSpec((HALO, D_MODEL), lambda i: (BLK // HALO - 1, 0)),
            pl.BlockSpec((1, D_MODEL), const),
            pl.BlockSpec((D_MODEL, 2 * D_FF), const, pipeline_mode=pl.Buffered(1)),
            pl.BlockSpec((8, D_FF), const),
            pl.BlockSpec((1, D_FF), const),
            pl.BlockSpec((D_FF, D_MODEL), const, pipeline_mode=pl.Buffered(1)),
            pl.BlockSpec((1, D_MODEL), const),
        ],
        out_specs=pl.BlockSpec((tm, D_MODEL), row),
        out_shape=jax.ShapeDtypeStruct((rows_x, D_MODEL), F32),
        scratch_shapes=[pltpu.VMEM((HALO + tm, D_MODEL), BF16)],
        compiler_params=_params(),
        name="conv_ffn",
    )(h_mid, h_mid, h_meta, ffn_norm, wup, conv_w, conv_b, wdn, final_norm)


def _rope_tables(nb):
    half = HEAD_DIM // 2
    inv_freq = ROPE_THETA ** (-np.arange(half, dtype=np.float64) / half)
    pos = (np.arange((nb + 1) * BLK) - META_PAD).astype(np.float64)
    ang = pos[:, None] * inv_freq[None, :]
    cos2 = np.tile(np.cos(ang), (1, LANES // half))
    sin = np.sin(ang)
    sin2 = np.tile(np.concatenate([-sin, sin], axis=-1), (1, LANES // HEAD_DIM))
    return jnp.asarray(np.concatenate([cos2, sin2], axis=1), F32)


def kernel(x, meta_tokens, mix_norm, w_in, b_in, w_alpha, b_alpha, attn_sinks, gla_head_norm,
           w_proj_attn, w_proj_gla, w_out, ffn_norm, w_up, conv_w, conv_b, w_down, final_norm):
    batch, seq, _ = x.shape
    assert batch % N_GROUPS == 0 and seq % BLK == 0
    rows_x = batch * seq
    nb = seq // BLK
    l = 0

    meta_block = jnp.concatenate(
        [jnp.zeros((META_PAD, D_MODEL), x.dtype), meta_tokens.astype(x.dtype)], axis=0)

    w_in_t = jnp.swapaxes(w_in, 1, 2)
    bi = b_in[l]
    b_cat = jnp.concatenate(
        [bi[:LOW0], bi[LOW0 + GLA_RANK:], bi[LOW0:LOW0 + GLA_RANK],
         jnp.zeros((C_LOW - GLA_RANK,), bi.dtype)])[None, :]
    wal = jnp.concatenate(
        [w_alpha[l], jnp.zeros((C_LOW - GLA_RANK, GLA_DK), w_alpha.dtype)], axis=0)
    wal_hi = wal.astype(BF16)
    wal_lo = (wal - wal_hi.astype(F32)).astype(BF16)

    rope = _rope_tables(nb)
    x_groups = x.reshape(N_GROUPS, rows_x // N_GROUPS, D_MODEL)
    h_mid, h_meta, wup, wdn = _mixer(
        x_groups, meta_block, mix_norm[l][None, :], w_in_t, b_cat, wal_hi, wal_lo,
        b_alpha[l][None, :], attn_sinks[l], rope, gla_head_norm[l][None, :],
        w_proj_attn[l].astype(BF16), w_proj_gla[l].astype(BF16), w_out[l].astype(BF16),
        w_up, w_down, l, nb)

    cw = jnp.concatenate([conv_w[l], jnp.zeros((8 - conv_w.shape[1], D_FF), conv_w.dtype)], axis=0)
    out = _ffn(h_mid.reshape(rows_x, D_MODEL), h_meta, ffn_norm[l][None, :], wup,
               cw, conv_b[l][None, :], wdn, final_norm[None, :], seq,
               tm=1024)
    return out.reshape(batch, seq, D_MODEL)
```
